```python
import jax, jax.numpy as jnp
from jax import lax
import numpy as np

D_MODEL = 1024
BATCH = 8
SEQ = 2048
DEPTH = 2

HEAD_DIM = 64
A_HEADS = 8
A_KV_HEADS = 2
B_HEADS = 8
B_KV_HEADS = 2
IDX_HEADS = 8
IDX_DIM = 64
WINDOW = 128
BLOCK = 128
TOPK_MAX = 256
ROPE_THETA = 10000.0
D_FF = ((-(-8 * D_MODEL // 3)) + 255) // 256 * 256
ALPHA = (2.0 * DEPTH) ** 0.25
BETA = (8.0 * DEPTH) ** -0.25

A_Q = A_HEADS * HEAD_DIM
A_KV = A_KV_HEADS * HEAD_DIM
B_Q = B_HEADS * HEAD_DIM
B_KV = B_KV_HEADS * HEAD_DIM
I_Q = IDX_HEADS * IDX_DIM
I_K = IDX_DIM
I_W = IDX_HEADS
_PIECES = [A_Q, A_KV, A_KV, B_Q, B_KV, B_KV, I_Q, I_K]
SPLITS = [int(v) for v in np.cumsum(_PIECES)]
D_IN = SPLITS[-1] + I_W

kernel_name = "hybrid_swa_sink_dsa_deepnorm_adaln"


def layer_norm(x, g, b, eps=1e-5):
    xf = x.astype(jnp.float32)
    mu = jnp.mean(xf, -1, keepdims=True)
    var = jnp.mean(jnp.square(xf - mu), -1, keepdims=True)
    return ((xf - mu) * lax.rsqrt(var + eps) * g.astype(jnp.float32) + b.astype(jnp.float32)).astype(x.dtype)


def rms_norm(x, g, eps=1e-6):
    xf = x.astype(jnp.float32)
    ms = jnp.mean(jnp.square(xf), -1, keepdims=True)
    return (xf * lax.rsqrt(ms + eps) * g.astype(jnp.float32)).astype(x.dtype)


def rope(x, pos):
    half = x.shape[-1] // 2
    inv = ROPE_THETA ** (-jnp.arange(half, dtype=jnp.float32) / half)
    ang = pos.astype(jnp.float32)[..., None] * inv
    cos = jnp.cos(ang)[:, :, None, :]
    sin = jnp.sin(ang)[:, :, None, :]
    xf = x.astype(jnp.float32)
    x1, x2 = xf[..., :half], xf[..., half:]
    return jnp.concatenate([x1 * cos - x2 * sin, x2 * cos + x1 * sin], -1).astype(x.dtype)


def swa_with_sinks(q, k, v, sinks):
    B, S, H, D = q.shape
    KVH = k.shape[2]
    G = H // KVH
    nb = S // BLOCK
    qb = q.reshape(B, nb, BLOCK, KVH, G, D)
    kb = k.reshape(B, nb, BLOCK, KVH, D)
    vb = v.reshape(B, nb, BLOCK, KVH, D)
    pad = jnp.zeros_like(kb[:, :1])
    kk = jnp.concatenate([jnp.concatenate([pad, kb[:, :-1]], 1), kb], axis=2)
    vv = jnp.concatenate([jnp.concatenate([pad, vb[:, :-1]], 1), vb], axis=2)
    s = jnp.einsum('bnqhgd,bnkhd->bnhgqk', qb, kk).astype(jnp.float32) * (D ** -0.5)
    qi = jnp.arange(BLOCK)[:, None]
    kj = jnp.arange(2 * BLOCK)[None, :]
    delta = BLOCK + qi - kj
    band = (delta >= 0) & (delta < WINDOW)
    has_prev = (jnp.arange(nb) > 0)[:, None, None] | (kj >= BLOCK)[None]
    mask = band[None] & has_prev
    s = jnp.where(mask[None, :, None, None], s, -jnp.inf)
    sink = jnp.broadcast_to(sinks.astype(jnp.float32).reshape(1, 1, KVH, G, 1, 1), s.shape[:-1] + (1,))
    p = jax.nn.softmax(jnp.concatenate([s, sink], -1), axis=-1)[..., :-1].astype(v.dtype)
    o = jnp.einsum('bnhgqk,bnkhd->bnqhgd', p, vv)
    return o.reshape(B, S, H * D)


def dsa_sparse_attention(q, k, v, qi, ki, wi):
    B, S, H, D = q.shape
    KVH = k.shape[2]
    G = H // KVH
    nb = S // BLOCK
    topk = min(TOPK_MAX, S // 4)
    w_scaled = wi.astype(jnp.float32) * (IDX_HEADS ** -0.5 * IDX_DIM ** -0.5)
    bidx = jnp.arange(B)[:, None, None]
    key_pos = jnp.arange(S)

    def to_blocks(a):
        return a.reshape((B, nb, BLOCK) + a.shape[2:]).swapaxes(0, 1)

    def one_block(args):
        qb, qib, wb, n = args
        t = n * BLOCK + jnp.arange(BLOCK)
        rel = jax.nn.relu(jnp.einsum('bqhd,bsd->bqhs', qib, ki).astype(jnp.float32))
        score = jnp.einsum('bqhs,bqh->bqs', rel, wb)
        causal = key_pos[None, :] <= t[:, None]
        score = jnp.where(causal[None], score, -jnp.inf)
        _, sel = lax.top_k(score, topk)
        ks = k[bidx, sel]
        vs = v[bidx, sel]
        qg = qb.reshape(B, BLOCK, KVH, G, D)
        s = jnp.einsum('bqhgd,bqnhd->bqhgn', qg, ks).astype(jnp.float32) * (D ** -0.5)
        valid = (sel <= t[None, :, None])[:, :, None, None, :]
        s = jnp.where(valid, s, -jnp.inf)
        p = jax.nn.softmax(s, axis=-1).astype(v.dtype)
        o = jnp.einsum('bqhgn,bqnhd->bqhgd', p, vs)
        return o.reshape(B, BLOCK, H * D)

    out = lax.map(one_block, (to_blocks(q), to_blocks(qi), to_blocks(w_scaled), jnp.arange(nb)))
    return out.swapaxes(0, 1).reshape(B, S, H * D)


def hybrid_mixer(h, pos, w_in, sinks, g_a, g_b, w_o):
    B, S, _ = h.shape
    proj = h @ w_in
    qa, ka, va, qb, kb, vb, qi, ki, wi = jnp.split(proj, SPLITS, axis=-1)

    def heads(a, n):
        return a.reshape(B, S, n, -1)

    qa = rope(heads(qa, A_HEADS), pos)
    ka = rope(heads(ka, A_KV_HEADS), pos)
    va = heads(va, A_KV_HEADS)
    qb = rope(heads(qb, B_HEADS), pos)
    kb = rope(heads(kb, B_KV_HEADS), pos)
    vb = heads(vb, B_KV_HEADS)
    qi = rope(heads(qi, IDX_HEADS), pos)
    ki = rope(ki[:, :, None, :], pos)[:, :, 0, :]
    ya = swa_with_sinks(qa, ka, va, sinks)
    yb = dsa_sparse_attention(qb, kb, vb, qi, ki, wi)
    y = jnp.concatenate([rms_norm(ya, g_a), rms_norm(yb, g_b)], -1)
    return y @ w_o


def swiglu(h, w_gate, w_up, w_down):
    return (jax.nn.silu(h @ w_gate) * (h @ w_up)) @ w_down


def setup_inputs(seed: int = 0) -> dict:
    key = jax.random.key(seed)
    ks = jax.random.split(key, 20)
    f32 = jnp.float32
    nrm = lambda k, shape, s: jax.random.normal(k, shape, f32) * s
    x = nrm(ks[0], (BATCH, SEQ, D_MODEL), 1.0)
    c = nrm(ks[1], (BATCH, D_MODEL), 1.0)
    offset = jax.random.randint(ks[2], (BATCH, 1), 0, 1024, dtype=jnp.int32)
    positions = offset + jnp.arange(SEQ, dtype=jnp.int32)[None, :]
    w_ada = nrm(ks[3], (DEPTH, D_MODEL, 6 * D_MODEL), 0.5 * D_MODEL ** -0.5)
    b_ada = nrm(ks[4], (DEPTH, 6 * D_MODEL), 0.01)
    col_scale = np.ones((D_IN,), np.float32)
    col_scale[SPLITS[1]:SPLITS[2]] = BETA
    col_scale[SPLITS[4]:SPLITS[5]] = BETA
    w_in = nrm(ks[5], (DEPTH, D_MODEL, D_IN), D_MODEL ** -0.5) * jnp.asarray(col_scale)
    attn_sinks = nrm(ks[6], (DEPTH, A_HEADS), 1.0)
    g_a = 1.0 + nrm(ks[7], (DEPTH, A_Q), 0.02)
    g_b = 1.0 + nrm(ks[8], (DEPTH, B_Q), 0.02)
    w_o = nrm(ks[9], (DEPTH, A_Q + B_Q, D_MODEL), BETA * (A_Q + B_Q) ** -0.5)
    ln1_g = 1.0 + nrm(ks[10], (DEPTH, D_MODEL), 0.02)
    ln1_b = nrm(ks[11], (DEPTH, D_MODEL), 0.02)
    w_gate = nrm(ks[12], (DEPTH, D_MODEL, D_FF), D_MODEL ** -0.5)
    w_up = nrm(ks[13], (DEPTH, D_MODEL, D_FF), D_MODEL ** -0.5)
    w_down = nrm(ks[14], (DEPTH, D_FF, D_MODEL), BETA * D_FF ** -0.5)
    ln2_g = 1.0 + nrm(ks[15], (DEPTH, D_MODEL), 0.02)
    ln2_b = nrm(ks[16], (DEPTH, D_MODEL), 0.02)
    return {"x": x, "c": c, "positions": positions, "w_ada": w_ada, "b_ada": b_ada,
            "w_in": w_in, "attn_sinks": attn_sinks, "g_a": g_a, "g_b": g_b, "w_o": w_o,
            "ln1_g": ln1_g, "ln1_b": ln1_b, "w_gate": w_gate, "w_up": w_up, "w_down": w_down,
            "ln2_g": ln2_g, "ln2_b": ln2_b}


def reference(x, c, positions, w_ada, b_ada, w_in, attn_sinks, g_a, g_b, w_o,
              ln1_g, ln1_b, w_gate, w_up, w_down, ln2_g, ln2_b):
    c_act = jax.nn.silu(c)
    for l in range(DEPTH):
        mod = (c_act @ w_ada[l] + b_ada[l])[:, None, :]
        sh1, sc1, gt1, sh2, sc2, gt2 = jnp.split(mod, 6, axis=-1)
        h = x * (1.0 + sc1) + sh1
        y = hybrid_mixer(h, positions, w_in[l], attn_sinks[l], g_a[l], g_b[l], w_o[l])
        x = layer_norm(ALPHA * x + (1.0 + gt1) * y, ln1_g[l], ln1_b[l])
        h = x * (1.0 + sc2) + sh2
        y = swiglu(h, w_gate[l], w_up[l], w_down[l])
        x = layer_norm(ALPHA * x + (1.0 + gt2) * y, ln2_g[l], ln2_b[l])
    return x
```

```python
import functools

import jax
import jax.numpy as jnp
import numpy as np
from jax import lax
from jax.experimental import pallas as pl
from jax.experimental.pallas import tpu as pltpu

HEAD_DIM = 64
HALF = HEAD_DIM // 2
A_HEADS = 8
A_KV_HEADS = 2
B_HEADS = 8
B_KV_HEADS = 2
IDX_HEADS = 8
IDX_DIM = 64
WINDOW = 128
BLOCK = 128
TOPK_MAX = 256
ROPE_THETA = 10000.0
LN_EPS = 1e-5
RMS_EPS = 1e-6

A_Q = A_HEADS * HEAD_DIM
A_KV = A_KV_HEADS * HEAD_DIM
B_Q = B_HEADS * HEAD_DIM
B_KV = B_KV_HEADS * HEAD_DIM
I_Q = IDX_HEADS * IDX_DIM
GROUP = A_HEADS // A_KV_HEADS

LANES = 128
KEY_CHUNK = 256
NEG_BIG = -1e30
INT_MIN = -(2 ** 31)
VMEM_LIMIT = 56 * 1024 * 1024

Q_ROWS = A_Q + B_Q + I_Q
K_ROWS = 3 * LANES
V_ROWS = A_KV + B_KV
W_ROWS = 8
P_ROWS = Q_ROWS + K_ROWS + V_ROWS + 2 * W_ROWS


def _silu(x):
    return x * (1.0 / (1.0 + jnp.exp(-x)))


def _layer_norm(z, g, b):
    mu = jnp.mean(z, axis=-1, keepdims=True)
    zc = z - mu
    var = jnp.mean(zc * zc, axis=-1, keepdims=True)
    return zc * lax.rsqrt(var + LN_EPS) * g + b


def _mod_kernel(c_ref, w_ref, b_ref, o_ref):
    c_act = _silu(c_ref[...]).astype(jnp.bfloat16)
    o_ref[...] = jnp.dot(c_act, w_ref[...].astype(jnp.bfloat16),
                         preferred_element_type=jnp.float32) + b_ref[...]


def _adaln_mod(c, w_ada, b_ada):
    depth, d, d6 = w_ada.shape
    bsz = c.shape[0]
    tn = 1024
    return pl.pallas_call(
        _mod_kernel,
        out_shape=jax.ShapeDtypeStruct((depth, bsz, d6), jnp.float32),
        grid=(depth, d6 // tn),
        in_specs=[
            pl.BlockSpec((bsz, d), lambda l, j: (0, 0)),
            pl.BlockSpec((None, d, tn), lambda l, j: (l, 0, j)),
            pl.BlockSpec((None, 1, tn), lambda l, j: (l, 0, j)),
        ],
        out_specs=pl.BlockSpec((None, bsz, tn), lambda l, j: (l, 0, j)),
        compiler_params=pltpu.CompilerParams(
            dimension_semantics=("arbitrary", "arbitrary"),
            vmem_limit_bytes=VMEM_LIMIT),
        name="adaln_mod",
    )(c, w_ada, b_ada.reshape(depth, 1, d6))


def _rope_head(x, r, cos, sin, scale=None):
    x1, x2 = x[r:r + HALF], x[r + HALF:r + HEAD_DIM]
    o1 = x1 * cos - x2 * sin
    o2 = x2 * cos + x1 * sin
    if scale is not None:
        o1, o2 = o1 * scale, o2 * scale
    return o1, o2


def _inproj_kernel(x_ref, sc_ref, sh_ref, pos_ref, inv_ref, w_ref,
                   qT_ref, kN_ref, vT_ref, wT_ref, *, w_scale):
    h = x_ref[...] * (1.0 + sc_ref[...]) + sh_ref[...]
    pT = lax.dot_general(w_ref[...], h.astype(jnp.bfloat16),
                         (((1,), (1,)), ((), ())),
                         preferred_element_type=jnp.float32)
    ang = inv_ref[...] * pos_ref[...].astype(jnp.float32)
    cos, sin = jnp.cos(ang), jnp.sin(ang)

    for hd in range(Q_ROWS // HEAD_DIM):
        r = hd * HEAD_DIM
        scale = HEAD_DIM ** -0.5 if r < A_Q + B_Q else None
        o1, o2 = _rope_head(pT, r, cos, sin, scale)
        qT_ref[r:r + HALF, :] = o1.astype(qT_ref.dtype)
        qT_ref[r + HALF:r + HEAD_DIM, :] = o2.astype(qT_ref.dtype)

    k_rows = []
    for hd in range(K_ROWS // HEAD_DIM):
        k_rows.extend(_rope_head(pT, Q_ROWS + hd * HEAD_DIM, cos, sin))
    kN_ref[...] = jnp.concatenate(k_rows, axis=0).T.astype(kN_ref.dtype)

    v0 = Q_ROWS + K_ROWS
    v = pT[v0:v0 + V_ROWS].astype(vT_ref.dtype)
    for j in range(vT_ref.shape[0]):
        vT_ref[j] = v[:, j * LANES:(j + 1) * LANES]

    w0 = v0 + V_ROWS
    wT_ref[...] = pT[w0:w0 + W_ROWS] * w_scale


def _inproj(x, sc, sh, pos3, inv, w_t, *, tm):
    bsz, s, d = x.shape
    kern = functools.partial(_inproj_kernel, w_scale=IDX_HEADS ** -0.5 * IDX_DIM ** -0.5)
    return pl.pallas_call(
        kern,
        out_shape=(
            jax.ShapeDtypeStruct((bsz, Q_ROWS, s), jnp.bfloat16),
            jax.ShapeDtypeStruct((bsz, s, K_ROWS), jnp.bfloat16),
            jax.ShapeDtypeStruct((bsz, s // LANES, V_ROWS, LANES), jnp.bfloat16),
            jax.ShapeDtypeStruct((bsz, W_ROWS, s), jnp.float32),
        ),
        grid=(bsz, s // tm),
        in_specs=[
            pl.BlockSpec((None, tm, d), lambda b, i: (b, i, 0)),
            pl.BlockSpec((None, 1, d), lambda b, i: (b, 0, 0)),
            pl.BlockSpec((None, 1, d), lambda b, i: (b, 0, 0)),
            pl.BlockSpec((None, 1, tm), lambda b, i: (b, 0, i)),
            pl.BlockSpec((HALF, 1), lambda b, i: (0, 0)),
            pl.BlockSpec((P_ROWS, d), lambda b, i: (0, 0)),
        ],
        out_specs=(
            pl.BlockSpec((None, Q_ROWS, tm), lambda b, i: (b, 0, i)),
            pl.BlockSpec((None, tm, K_ROWS), lambda b, i: (b, i, 0)),
            pl.BlockSpec((None, tm // LANES, V_ROWS, LANES), lambda b, i: (b, i, 0, 0)),
            pl.BlockSpec((None, W_ROWS, tm), lambda b, i: (b, 0, i)),
        ),
        compiler_params=pltpu.CompilerParams(
            dimension_semantics=("arbitrary", "arbitrary"),
            vmem_limit_bytes=VMEM_LIMIT),
        name="in_proj",
    )(x, sc, sh, pos3, inv, w_t)


def _ordered_to_f32(c):
    bits = jnp.where(c >= 0, c, c ^ jnp.int32(0x7FFFFFFF))
    return lax.bitcast_convert_type(bits, jnp.float32)


def _head_rhs(qT, row0):
    zeros = jnp.zeros((HEAD_DIM, LANES), qT.dtype)
    cols = []
    for h in range(A_HEADS):
        blk = qT[row0 + h * HEAD_DIM:row0 + (h + 1) * HEAD_DIM, :]
        g = h // GROUP
        cols.append(jnp.concatenate([blk, zeros] if g == 0 else [zeros, blk], axis=0))
    return jnp.concatenate(cols, axis=1)


def _attn_kernel(sink_ref, qT_ref, kN_ref, vT_ref, wT_ref, ga_ref, gb_ref,
                 y_ref, score_ref, thr_ref, jt_ref, m_ref, l_ref, o_ref, yT_ref,
                 *, seq, topk):
    n = pl.program_id(1)
    q0 = n * BLOCK
    nk = (n + 2) // 2
    qidx = q0 + lax.broadcasted_iota(jnp.int32, (KEY_CHUNK, LANES), 1)
    row_iota = lax.broadcasted_iota(jnp.int32, (KEY_CHUNK, LANES), 0)
    nh = B_HEADS

    qi_rhs = jnp.concatenate(
        [qT_ref[A_Q + B_Q + h * IDX_DIM:A_Q + B_Q + (h + 1) * IDX_DIM, :]
         for h in range(IDX_HEADS)], axis=1)
    w_rows = wT_ref[...]

    def score_body(c, carry):
        r0 = pl.multiple_of(c * KEY_CHUNK, KEY_CHUNK)
        ki = kN_ref[pl.ds(r0, KEY_CHUNK), 2 * LANES:2 * LANES + IDX_DIM]
        rel = jnp.dot(ki, qi_rhs, preferred_element_type=jnp.float32)
        sc = jnp.zeros((KEY_CHUNK, LANES), jnp.float32)
        for h in range(IDX_HEADS):
            sc = sc + jnp.maximum(rel[:, h * LANES:(h + 1) * LANES], 0.0) * w_rows[h:h + 1, :]
        kidx = r0 + row_iota
        score_ref[pl.ds(r0, KEY_CHUNK), :] = jnp.where(kidx <= qidx, sc, -jnp.inf)
        return carry

    lax.fori_loop(0, nk, score_body, 0)

    thr_ref[...] = jnp.full((1, LANES), -jnp.inf, jnp.float32)
    jt_ref[...] = jnp.full((1, LANES), -1, jnp.int32)

    def fold_rows(x):
        return x.reshape(KEY_CHUNK // 8, 8, LANES).sum(axis=0)

    def count(pred_fn):
        def body(c, acc):
            r0 = pl.multiple_of(c * KEY_CHUNK, KEY_CHUNK)
            blk = score_ref[pl.ds(r0, KEY_CHUNK), :]
            return acc + fold_rows(pred_fn(blk, r0).astype(jnp.int32))
        acc = lax.fori_loop(0, nk, body, jnp.zeros((8, LANES), jnp.int32))
        return acc.sum(axis=0, keepdims=True)

    def masked_min(pred_fn):
        def body(c, acc):
            r0 = pl.multiple_of(c * KEY_CHUNK, KEY_CHUNK)
            blk = score_ref[pl.ds(r0, KEY_CHUNK), :]
            v = jnp.where(pred_fn(blk, r0), blk, jnp.inf)
            return jnp.minimum(acc, v.reshape(KEY_CHUNK // 8, 8, LANES).min(axis=0))
        acc = lax.fori_loop(0, nk, body, jnp.full((8, LANES), jnp.inf, jnp.float32))
        return acc.min(axis=0, keepdims=True)

    @pl.when(n >= 2)
    def _search():
        c0 = count(lambda blk, r0: blk >= 0.0)
        pfx0 = jnp.where(c0 >= topk, jnp.int32(0), jnp.int32(INT_MIN))

        def bit_body(i, pfx):
            cand = pfx + (jnp.int32(1) << (30 - i))
            cf = _ordered_to_f32(cand)
            cnt = count(lambda blk, r0: blk >= cf)
            return jnp.where(cnt >= topk, cand, pfx)

        pfx = lax.fori_loop(0, 31, bit_body, pfx0)
        lo = _ordered_to_f32(pfx)

        def counts(bv):
            return (count(lambda blk, r0: blk > bv), count(lambda blk, r0: blk >= bv))

        b0 = masked_min(lambda blk, r0: blk >= lo)
        cgt0, cge0 = counts(b0)

        def w_cond(st):
            return jnp.max(st[1], axis=1, keepdims=True)[0, 0] >= topk

        def w_body(st):
            bv, cgt, _ = st
            nxt = masked_min(lambda blk, r0: blk > bv)
            bn = jnp.where(cgt >= topk, nxt, bv)
            cgt2, cge2 = counts(bn)
            return bn, cgt2, cge2

        bv, cgt, cge = lax.while_loop(w_cond, w_body, (b0, cgt0, cge0))
        thr_ref[...] = bv
        jt_ref[...] = jnp.full((1, LANES), seq, jnp.int32)
        need = topk - cgt

        @pl.when(jnp.max(cge, axis=1, keepdims=True)[0, 0] > topk)
        def _ties():
            nbits = max(1, int(np.ceil(np.log2(seq))))

            def jbody(i, pj):
                step = jnp.int32(1) << (nbits - 1 - i)
                cand = pj + step - 1
                f = count(lambda blk, r0: (blk == bv) & ((r0 + row_iota) <= cand))
                return jnp.where(f >= need, pj, pj + step)

            jt_ref[...] = lax.fori_loop(0, nbits, jbody, jnp.zeros((1, LANES), jnp.int32))

    thr = thr_ref[...]
    jt = jt_ref[...]

    qT_all = qT_ref[...]
    rhs_b = _head_rhs(qT_all, A_Q)
    m_ref[...] = jnp.full(m_ref.shape, NEG_BIG, jnp.float32)
    l_ref[...] = jnp.zeros(l_ref.shape, jnp.float32)
    o_ref[...] = jnp.zeros(o_ref.shape, jnp.float32)

    def dsa_body(c, carry):
        r0 = pl.multiple_of(c * KEY_CHUNK, KEY_CHUNK)
        sc = score_ref[pl.ds(r0, KEY_CHUNK), :]
        kidx = r0 + row_iota
        sel = (sc > thr) | ((sc == thr) & (kidx <= jt))
        kb = kN_ref[pl.ds(r0, KEY_CHUNK), LANES:2 * LANES]
        s_all = jnp.dot(kb, rhs_b, preferred_element_type=jnp.float32)
        p_heads, alphas = [], []
        for h in range(nh):
            sl = slice(h * LANES, (h + 1) * LANES)
            s_h = s_all[:, sl]
            m_old = m_ref[:, sl]
            m_new = jnp.maximum(m_old, jnp.max(jnp.where(sel, s_h, NEG_BIG), axis=0, keepdims=True))
            p = jnp.where(sel, jnp.exp(s_h - m_new), 0.0)
            alpha = jnp.exp(m_old - m_new)
            l_ref[:, sl] = alpha * l_ref[:, sl] + jnp.sum(p, axis=0, keepdims=True)
            m_ref[:, sl] = m_new
            alphas.append(alpha)
            p_heads.append(p.astype(jnp.bfloat16))
        for g in range(B_KV_HEADS):
            pT = jnp.concatenate(p_heads[g * GROUP:(g + 1) * GROUP], axis=1)
            a_g = jnp.concatenate(alphas[g * GROUP:(g + 1) * GROUP], axis=1)
            rows = slice(A_KV + g * HEAD_DIM, A_KV + (g + 1) * HEAD_DIM)
            vg = jnp.concatenate([vT_ref[2 * c, rows, :], vT_ref[2 * c + 1, rows, :]], axis=1)
            csl = slice(g * GROUP * LANES, (g + 1) * GROUP * LANES)
            o_ref[:, csl] = o_ref[:, csl] * a_g + jnp.dot(vg, pT, preferred_element_type=jnp.float32)
        return carry

    lax.fori_loop(0, nk, dsa_body, 0)

    inv_l = 1.0 / l_ref[...]
    ob = o_ref[...] * inv_l
    for h in range(nh):
        yT_ref[B_Q + h * HEAD_DIM:B_Q + (h + 1) * HEAD_DIM, :] = ob[:, h * LANES:(h + 1) * LANES]

    rhs_a = _head_rhs(qT_all, 0)
    wblk = jnp.maximum(n - 1, 0)
    w0 = pl.multiple_of(wblk * BLOCK, BLOCK)
    ka = kN_ref[pl.ds(w0, 2 * BLOCK), 0:LANES]
    s_all = jnp.dot(ka, rhs_a, preferred_element_type=jnp.float32)
    delta = qidx - (w0 + row_iota)
    band = (delta >= 0) & (delta < WINDOW)
    p_heads, inv_ls = [], []
    for h in range(A_HEADS):
        s_h = s_all[:, h * LANES:(h + 1) * LANES]
        sink = sink_ref[h]
        m = jnp.maximum(jnp.max(jnp.where(band, s_h, NEG_BIG), axis=0, keepdims=True), sink)
        p = jnp.where(band, jnp.exp(s_h - m), 0.0)
        l = jnp.sum(p, axis=0, keepdims=True) + jnp.exp(sink - m)
        inv_ls.append(1.0 / l)
        p_heads.append(p.astype(jnp.bfloat16))
    for g in range(A_KV_HEADS):
        pT = jnp.concatenate(p_heads[g * GROUP:(g + 1) * GROUP], axis=1)
        rows = slice(g * HEAD_DIM, (g + 1) * HEAD_DIM)
        vg = jnp.concatenate([vT_ref[wblk, rows, :], vT_ref[wblk + 1, rows, :]], axis=1)
        og = jnp.dot(vg, pT, preferred_element_type=jnp.float32)
        for j in range(GROUP):
            h = g * GROUP + j
            yT_ref[h * HEAD_DIM:(h + 1) * HEAD_DIM, :] = og[:, j * LANES:(j + 1) * LANES] * inv_ls[h]

    y = yT_ref[...].T
    ya, yb = y[:, 0:A_Q], y[:, A_Q:A_Q + B_Q]
    ya = ya * lax.rsqrt(jnp.mean(ya * ya, axis=-1, keepdims=True) + RMS_EPS) * ga_ref[...]
    yb = yb * lax.rsqrt(jnp.mean(yb * yb, axis=-1, keepdims=True) + RMS_EPS) * gb_ref[...]
    y_ref[:, 0:A_Q] = ya.astype(y_ref.dtype)
    y_ref[:, A_Q:A_Q + B_Q] = yb.astype(y_ref.dtype)


def _attention(sinks, qT, kN, vT, wT, g_a, g_b):
    bsz, _, s = qT.shape
    nb = s // BLOCK
    topk = min(TOPK_MAX, s // 4)
    kern = functools.partial(_attn_kernel, seq=s, topk=topk)
    return pl.pallas_call(
        kern,
        out_shape=jax.ShapeDtypeStruct((bsz, s, A_Q + B_Q), jnp.bfloat16),
        grid=(bsz, nb),
        in_specs=[
            pl.BlockSpec(memory_space=pltpu.SMEM),
            pl.BlockSpec((None, Q_ROWS, BLOCK), lambda b, n: (b, 0, n)),
            pl.BlockSpec((None, s, K_ROWS), lambda b, n: (b, 0, 0)),
            pl.BlockSpec((None, nb, V_ROWS, LANES), lambda b, n: (b, 0, 0, 0)),
            pl.BlockSpec((None, W_ROWS, BLOCK), lambda b, n: (b, 0, n)),
            pl.BlockSpec((1, A_Q), lambda b, n: (0, 0)),
            pl.BlockSpec((1, B_Q), lambda b, n: (0, 0)),
        ],
        out_specs=pl.BlockSpec((None, BLOCK, A_Q + B_Q), lambda b, n: (b, n, 0)),
        scratch_shapes=[
            pltpu.VMEM((s, LANES), jnp.float32),
            pltpu.VMEM((1, LANES), jnp.float32),
            pltpu.VMEM((1, LANES), jnp.int32),
            pltpu.VMEM((1, B_HEADS * LANES), jnp.float32),
            pltpu.VMEM((1, B_HEADS * LANES), jnp.float32),
            pltpu.VMEM((HEAD_DIM, B_HEADS * LANES), jnp.float32),
            pltpu.VMEM((A_Q + B_Q, LANES), jnp.float32),
        ],
        compiler_params=pltpu.CompilerParams(
            dimension_semantics=("arbitrary", "arbitrary"),
            vmem_limit_bytes=VMEM_LIMIT),
        name="attention",
    )(sinks, qT, kN, vT, wT, g_a, g_b)


def _outproj_kernel(y_ref, x_ref, gt_ref, w_ref, g_ref, b_ref, o_ref, *, alpha):
    proj = jnp.dot(y_ref[...], w_ref[...], preferred_element_type=jnp.float32)
    z = alpha * x_ref[...] + (1.0 + gt_ref[...]) * proj
    o_ref[...] = _layer_norm(z, g_ref[...], b_ref[...])


def _outproj_ln(y, x, gt, w_o, g, b, *, alpha, tm):
    bsz, s, d = x.shape
    dy = y.shape[-1]
    return pl.pallas_call(
        functools.partial(_outproj_kernel, alpha=alpha),
        out_shape=jax.ShapeDtypeStruct((bsz, s, d), jnp.float32),
        grid=(bsz, s // tm),
        in_specs=[
            pl.BlockSpec((None, tm, dy), lambda b_, i: (b_, i, 0)),
            pl.BlockSpec((None, tm, d), lambda b_, i: (b_, i, 0)),
            pl.BlockSpec((None, 1, d), lambda b_, i: (b_, 0, 0)),
            pl.BlockSpec((dy, d), lambda b_, i: (0, 0)),
            pl.BlockSpec((1, d), lambda b_, i: (0, 0)),
            pl.BlockSpec((1, d), lambda b_, i: (0, 0)),
        ],
        out_specs=pl.BlockSpec((None, tm, d), lambda b_, i: (b_, i, 0)),
        compiler_params=pltpu.CompilerParams(
            dimension_semantics=("arbitrary", "arbitrary"),
            vmem_limit_bytes=VMEM_LIMIT),
        name="out_proj_ln",
    )(y, x, gt, w_o, g, b)


def _ffn_kernel(x_ref, sc_ref, sh_ref, gt_ref, wg_ref, wu_ref, wd_ref, g_ref, b_ref, o_ref, *, alpha):
    x = x_ref[...]
    h = (x * (1.0 + sc_ref[...]) + sh_ref[...]).astype(jnp.bfloat16)
    gate = jnp.dot(h, wg_ref[...], preferred_element_type=jnp.float32)
    up = jnp.dot(h, wu_ref[...], preferred_element_type=jnp.float32)
    act = (_silu(gate) * up).astype(jnp.bfloat16)
    y = jnp.dot(act, wd_ref[...], preferred_element_type=jnp.float32)
    z = alpha * x + (1.0 + gt_ref[...]) * y
    o_ref[...] = _layer_norm(z, g_ref[...], b_ref[...])


def _ffn_ln(x, sc, sh, gt, w_gate, w_up, w_down, g, b, *, alpha, tm):
    bsz, s, d = x.shape
    dff = w_gate.shape[-1]
    resident = pl.Buffered(1)
    return pl.pallas_call(
        functools.partial(_ffn_kernel, alpha=alpha),
        out_shape=jax.ShapeDtypeStruct((bsz, s, d), jnp.float32),
        grid=(bsz, s // tm),
        in_specs=[
            pl.BlockSpec((None, tm, d), lambda b_, i: (b_, i, 0)),
            pl.BlockSpec((None, 1, d), lambda b_, i: (b_, 0, 0)),
            pl.BlockSpec((None, 1, d), lambda b_, i: (b_, 0, 0)),
            pl.BlockSpec((None, 1, d), lambda b_, i: (b_, 0, 0)),
            pl.BlockSpec((d, dff), lambda b_, i: (0, 0), pipeline_mode=resident),
            pl.BlockSpec((d, dff), lambda b_, i: (0, 0), pipeline_mode=resident),
            pl.BlockSpec((dff, d), lambda b_, i: (0, 0), pipeline_mode=resident),
            pl.BlockSpec((1, d), lambda b_, i: (0, 0)),
            pl.BlockSpec((1, d), lambda b_, i: (0, 0)),
        ],
        out_specs=pl.BlockSpec((None, tm, d), lambda b_, i: (b_, i, 0)),
        compiler_params=pltpu.CompilerParams(
            dimension_semantics=("arbitrary", "arbitrary"),
            vmem_limit_bytes=VMEM_LIMIT),
        name="ffn_ln",
    )(x, sc, sh, gt, w_gate, w_up, w_down, g, b)


def _regroup_w_in(w_in_l):
    d = w_in_l.shape[0]
    splits = np.cumsum([A_Q, A_KV, A_KV, B_Q, B_KV, B_KV, I_Q, IDX_DIM, IDX_HEADS])
    qa, ka, va, qb, kb, vb, qi, ki, wi = jnp.split(w_in_l, splits[:-1], axis=1)
    zk = jnp.zeros((d, K_ROWS - A_KV - B_KV - IDX_DIM), w_in_l.dtype)
    zw = jnp.zeros((d, 2 * W_ROWS - IDX_HEADS), w_in_l.dtype)
    cols = jnp.concatenate([qa, qb, qi, ka, kb, ki, zk, va, vb, wi, zw], axis=1)
    return cols.T.astype(jnp.bfloat16)


def kernel(x, c, positions, w_ada, b_ada, w_in, attn_sinks, g_a, g_b, w_o,
           ln1_g, ln1_b, w_gate, w_up, w_down, ln2_g, ln2_b):
    bsz, s, d = x.shape
    depth = w_ada.shape[0]
    alpha = (2.0 * depth) ** 0.25
    bf = jnp.bfloat16

    mod = _adaln_mod(c, w_ada, b_ada)
    inv = (ROPE_THETA ** (-jnp.arange(HALF, dtype=jnp.float32) / HALF)).reshape(HALF, 1)
    pos3 = positions.reshape(bsz, 1, s)

    for l in range(depth):
        m = mod[l].reshape(bsz, 1, 6 * d)
        sh1, sc1, gt1, sh2, sc2, gt2 = [m[:, :, i * d:(i + 1) * d] for i in range(6)]
        qT, kN, vT, wT = _inproj(x, sc1, sh1, pos3, inv, _regroup_w_in(w_in[l]), tm=512)
        y = _attention(attn_sinks[l], qT, kN, vT, wT,
                       g_a[l].reshape(1, -1), g_b[l].reshape(1, -1))
        x = _outproj_ln(y, x, gt1, w_o[l].astype(bf), ln1_g[l].reshape(1, d),
                        ln1_b[l].reshape(1, d), alpha=alpha, tm=512)
        x = _ffn_ln(x, sc2, sh2, gt2, w_gate[l].astype(bf), w_up[l].astype(bf),
                    w_down[l].astype(bf), ln2_g[l].reshape(1, d), ln2_b[l].reshape(1, d),
                    alpha=alpha, tm=512)
    return x
```

```python
import functools

import jax
import jax.numpy as jnp
import numpy as np
from jax import lax
from jax.experimental import pallas as pl
from jax.experimental.pallas import tpu as pltpu

HEAD_DIM = 64
HALF = HEAD_DIM // 2
A_HEADS = 8
A_KV_HEADS = 2
B_HEADS = 8
B_KV_HEADS = 2
IDX_HEADS = 8
IDX_DIM = 64
WINDOW = 128
BLOCK = 128
TOPK_MAX = 256
ROPE_THETA = 10000.0
LN_EPS = 1e-5
RMS_EPS = 1e-6

A_Q = A_HEADS * HEAD_DIM
A_KV = A_KV_HEADS * HEAD_DIM
B_Q = B_HEADS * HEAD_DIM
B_KV = B_KV_HEADS * HEAD_DIM
I_Q = IDX_HEADS * IDX_DIM
GROUP = A_HEADS // A_KV_HEADS

LANES = 128
KEY_CHUNK = 256
ONES_ROWS = 16
O_ROWS = HEAD_DIM + 8
NEG_BIG = -1e30
INT_MIN = -(2 ** 31)
VMEM_LIMIT = 56 * 1024 * 1024

Q_ROWS = A_Q + B_Q + I_Q
K_ROWS = 3 * LANES
V_ROWS = A_KV + B_KV
W_ROWS = 8
P_ROWS = Q_ROWS + K_ROWS + V_ROWS + 2 * W_ROWS


def _silu(x):
    return x * (1.0 / (1.0 + jnp.exp(-x)))


def _layer_norm(z, g, b):
    mu = jnp.mean(z, axis=-1, keepdims=True)
    zc = z - mu
    var = jnp.mean(zc * zc, axis=-1, keepdims=True)
    return zc * lax.rsqrt(var + LN_EPS) * g + b


def _mod_kernel(c_ref, w_ref, b_ref, o_ref):
    c_act = _silu(c_ref[...]).astype(jnp.bfloat16)
    o_ref[...] = jnp.dot(c_act, w_ref[...].astype(jnp.bfloat16),
                         preferred_element_type=jnp.float32) + b_ref[...]


def _adaln_mod(c, w_ada, b_ada):
    depth, d, d6 = w_ada.shape
    bsz = c.shape[0]
    tn = 1024
    return pl.pallas_call(
        _mod_kernel,
        out_shape=jax.ShapeDtypeStruct((depth, bsz, d6), jnp.float32),
        grid=(depth, d6 // tn),
        in_specs=[
            pl.BlockSpec((bsz, d), lambda l, j: (0, 0)),
            pl.BlockSpec((None, d, tn), lambda l, j: (l, 0, j)),
            pl.BlockSpec((None, 1, tn), lambda l, j: (l, 0, j)),
        ],
        out_specs=pl.BlockSpec((None, bsz, tn), lambda l, j: (l, 0, j)),
        compiler_params=pltpu.CompilerParams(
            dimension_semantics=("arbitrary", "arbitrary"),
            vmem_limit_bytes=VMEM_LIMIT),
        name="adaln_mod",
    )(c, w_ada, b_ada.reshape(depth, 1, d6))


def _rope_head(x, r, cos, sin, scale=None):
    x1, x2 = x[r:r + HALF], x[r + HALF:r + HEAD_DIM]
    o1 = x1 * cos - x2 * sin
    o2 = x2 * cos + x1 * sin
    if scale is not None:
        o1, o2 = o1 * scale, o2 * scale
    return o1, o2


def _inproj_kernel(x_ref, sc_ref, sh_ref, pos_ref, inv_ref, w_ref,
                   qT_ref, kN_ref, vT_ref, wT_ref, *, w_scale):
    h = x_ref[...] * (1.0 + sc_ref[...]) + sh_ref[...]
    pT = lax.dot_general(w_ref[...], h.astype(jnp.bfloat16),
                         (((1,), (1,)), ((), ())),
                         preferred_element_type=jnp.float32)
    ang = inv_ref[...] * pos_ref[...].astype(jnp.float32)
    cos, sin = jnp.cos(ang), jnp.sin(ang)

    for hd in range(Q_ROWS // HEAD_DIM):
        r = hd * HEAD_DIM
        scale = HEAD_DIM ** -0.5 if r < A_Q + B_Q else None
        o1, o2 = _rope_head(pT, r, cos, sin, scale)
        qT_ref[r:r + HALF, :] = o1.astype(qT_ref.dtype)
        qT_ref[r + HALF:r + HEAD_DIM, :] = o2.astype(qT_ref.dtype)

    k_rows = []
    for hd in range(K_ROWS // HEAD_DIM):
        k_rows.extend(_rope_head(pT, Q_ROWS + hd * HEAD_DIM, cos, sin))
    kN_ref[...] = jnp.concatenate(k_rows, axis=0).T.astype(kN_ref.dtype)

    v0 = Q_ROWS + K_ROWS
    v = pT[v0:v0 + V_ROWS].astype(vT_ref.dtype)
    for j in range(vT_ref.shape[0]):
        vT_ref[j] = v[:, j * LANES:(j + 1) * LANES]

    w0 = v0 + V_ROWS
    wT_ref[...] = pT[w0:w0 + W_ROWS] * w_scale


def _inproj(x, sc, sh, pos3, inv, w_t, *, tm):
    bsz, s, d = x.shape
    kern = functools.partial(_inproj_kernel, w_scale=IDX_HEADS ** -0.5 * IDX_DIM ** -0.5)
    return pl.pallas_call(
        kern,
        out_shape=(
            jax.ShapeDtypeStruct((bsz, Q_ROWS, s), jnp.bfloat16),
            jax.ShapeDtypeStruct((bsz, s, K_ROWS), jnp.bfloat16),
            jax.ShapeDtypeStruct((bsz, s // LANES, V_ROWS, LANES), jnp.bfloat16),
            jax.ShapeDtypeStruct((bsz, W_ROWS, s), jnp.float32),
        ),
        grid=(bsz, s // tm),
        in_specs=[
            pl.BlockSpec((None, tm, d), lambda b, i: (b, i, 0)),
            pl.BlockSpec((None, 1, d), lambda b, i: (b, 0, 0)),
            pl.BlockSpec((None, 1, d), lambda b, i: (b, 0, 0)),
            pl.BlockSpec((None, 1, tm), lambda b, i: (b, 0, i)),
            pl.BlockSpec((HALF, 1), lambda b, i: (0, 0)),
            pl.BlockSpec((P_ROWS, d), lambda b, i: (0, 0)),
        ],
        out_specs=(
            pl.BlockSpec((None, Q_ROWS, tm), lambda b, i: (b, 0, i)),
            pl.BlockSpec((None, tm, K_ROWS), lambda b, i: (b, i, 0)),
            pl.BlockSpec((None, tm // LANES, V_ROWS, LANES), lambda b, i: (b, i, 0, 0)),
            pl.BlockSpec((None, W_ROWS, tm), lambda b, i: (b, 0, i)),
        ),
        compiler_params=pltpu.CompilerParams(
            dimension_semantics=("arbitrary", "arbitrary"),
            vmem_limit_bytes=VMEM_LIMIT),
        name="in_proj",
    )(x, sc, sh, pos3, inv, w_t)


def _ordered_to_f32(c):
    bits = jnp.where(c >= 0, c, c ^ jnp.int32(0x7FFFFFFF))
    return lax.bitcast_convert_type(bits, jnp.float32)


def _head_rhs(qT, row0):
    zeros = jnp.zeros((HEAD_DIM, LANES), qT.dtype)
    cols = []
    for h in range(A_HEADS):
        blk = qT[row0 + h * HEAD_DIM:row0 + (h + 1) * HEAD_DIM, :]
        g = h // GROUP
        cols.append(jnp.concatenate([blk, zeros] if g == 0 else [zeros, blk], axis=0))
    return jnp.concatenate(cols, axis=1)


def _attn_kernel(sink_ref, qT_ref, kN_ref, vT_ref, wT_ref, ga_ref, gb_ref,
                 y_ref, score_ref, thr_ref, jt_ref, m_ref, o_ref, yT_ref,
                 sa_ref, sb_ref, pa_ref, pb_ref, aa_ref, ab_ref, rhs_ref,
                 *, seq, topk):
    n = pl.program_id(1)
    q0 = n * BLOCK
    nk = (n + 2) // 2
    qidx = q0 + lax.broadcasted_iota(jnp.int32, (KEY_CHUNK, LANES), 1)
    row_iota = lax.broadcasted_iota(jnp.int32, (KEY_CHUNK, LANES), 0)
    nh = B_HEADS

    for h in range(IDX_HEADS):
        rhs_ref[0:IDX_DIM, h * LANES:(h + 1) * LANES] = (
            qT_ref[A_Q + B_Q + h * IDX_DIM:A_Q + B_Q + (h + 1) * IDX_DIM, :])

    npair = (nk + 1) // 2
    last = nk - 1

    def rows_of(c):
        return pl.ds(pl.multiple_of(c * KEY_CHUNK, KEY_CHUNK), KEY_CHUNK)

    def rel_of(c):
        ki = kN_ref[rows_of(c), 2 * LANES:2 * LANES + IDX_DIM]
        return jnp.dot(ki, rhs_ref[0:IDX_DIM, :], preferred_element_type=jnp.float32)

    def score_from(s_ref, c):
        sc = jnp.zeros((KEY_CHUNK, LANES), jnp.float32)
        for h in range(IDX_HEADS):
            sc = sc + jnp.maximum(s_ref[:, h * LANES:(h + 1) * LANES], 0.0) * wT_ref[h:h + 1, :]
        kidx = c * KEY_CHUNK + row_iota
        score_ref[rows_of(c), :] = jnp.where(kidx <= qidx, sc, -jnp.inf)

    sa_ref[...] = rel_of(0)

    def score_body(i, carry):
        c0 = 2 * i
        c1 = jnp.minimum(c0 + 1, last)
        sb_ref[...] = rel_of(c1)
        score_from(sa_ref, c0)
        sa_ref[...] = rel_of(jnp.minimum(c0 + 2, last))
        score_from(sb_ref, c1)
        return carry

    lax.fori_loop(0, npair, score_body, 0)

    thr_ref[...] = jnp.full((1, LANES), -jnp.inf, jnp.float32)
    jt_ref[...] = jnp.full((1, LANES), -1, jnp.int32)

    def fold_rows(x):
        return x.reshape(KEY_CHUNK // 8, 8, LANES).sum(axis=0)

    def count(pred_fn):
        def body(c, acc):
            r0 = pl.multiple_of(c * KEY_CHUNK, KEY_CHUNK)
            blk = score_ref[pl.ds(r0, KEY_CHUNK), :]
            return acc + fold_rows(pred_fn(blk, r0).astype(jnp.int32))
        acc = lax.fori_loop(0, nk, body, jnp.zeros((8, LANES), jnp.int32))
        return acc.sum(axis=0, keepdims=True)

    def masked_min(pred_fn):
        def body(c, acc):
            r0 = pl.multiple_of(c * KEY_CHUNK, KEY_CHUNK)
            blk = score_ref[pl.ds(r0, KEY_CHUNK), :]
            v = jnp.where(pred_fn(blk, r0), blk, jnp.inf)
            return jnp.minimum(acc, v.reshape(KEY_CHUNK // 8, 8, LANES).min(axis=0))
        acc = lax.fori_loop(0, nk, body, jnp.full((8, LANES), jnp.inf, jnp.float32))
        return acc.min(axis=0, keepdims=True)

    def search_static(nkk):
        def count_static(preds_fn, nacc):
            accs = [jnp.zeros((8, LANES), jnp.int32) for _ in range(nacc)]
            for c in range(nkk):
                blk = score_ref[c * KEY_CHUNK:(c + 1) * KEY_CHUNK, :]
                for j, p in enumerate(preds_fn(blk)):
                    accs[j] = accs[j] + fold_rows(p.astype(jnp.int32))
            return [a.sum(axis=0, keepdims=True) for a in accs]

        def run():
            (c0,) = count_static(lambda blk: [blk >= 0.0], 1)
            pfx0 = jnp.where(c0 >= topk, jnp.int32(0), jnp.int32(INT_MIN))

            def bit_body(i, pfx):
                cand = pfx + (jnp.int32(1) << (30 - i))
                cf = _ordered_to_f32(cand)
                (cnt,) = count_static(lambda blk: [blk >= cf], 1)
                return jnp.where(cnt >= topk, cand, pfx)

            pfx = lax.fori_loop(0, 31, bit_body, pfx0)
            lo = _ordered_to_f32(pfx)
            acc = jnp.full((8, LANES), jnp.inf, jnp.float32)
            for c in range(nkk):
                blk = score_ref[c * KEY_CHUNK:(c + 1) * KEY_CHUNK, :]
                v = jnp.where(blk >= lo, blk, jnp.inf)
                acc = jnp.minimum(acc, v.reshape(KEY_CHUNK // 8, 8, LANES).min(axis=0))
            b0 = acc.min(axis=0, keepdims=True)
            cgt0, cge0 = count_static(lambda blk: [blk > b0, blk >= b0], 2)
            return b0, cgt0, cge0

        return run

    @pl.when(n >= 2)
    def _search():
        max_nk = (seq // BLOCK + 1) // 2
        b0, cgt0, cge0 = lax.switch(nk - 2, [search_static(j) for j in range(2, max_nk + 1)])

        def counts(bv):
            return (count(lambda blk, r0: blk > bv), count(lambda blk, r0: blk >= bv))

        thr_ref[...] = b0
        jt_ref[...] = jnp.full((1, LANES), seq, jnp.int32)
        slow = jnp.where((cgt0 >= topk) | (cge0 > topk), 1, 0)

        @pl.when(jnp.max(slow, axis=1, keepdims=True)[0, 0] > 0)
        def _slow_path():
            def w_cond(st):
                return jnp.max(st[1], axis=1, keepdims=True)[0, 0] >= topk

            def w_body(st):
                bv, cgt, _ = st
                nxt = masked_min(lambda blk, r0: blk > bv)
                bn = jnp.where(cgt >= topk, nxt, bv)
                cgt2, cge2 = counts(bn)
                return bn, cgt2, cge2

            bv, cgt, _ = lax.while_loop(w_cond, w_body, (b0, cgt0, cge0))
            thr_ref[...] = bv
            need = topk - cgt
            nbits = max(1, int(np.ceil(np.log2(seq))))

            def jbody(i, pj):
                step = jnp.int32(1) << (nbits - 1 - i)
                cand = pj + step - 1
                f = count(lambda blk, r0: (blk == bv) & ((r0 + row_iota) <= cand))
                return jnp.where(f >= need, pj, pj + step)

            jt_ref[...] = lax.fori_loop(0, nbits, jbody, jnp.zeros((1, LANES), jnp.int32))

    thr = thr_ref[...]
    jt = jt_ref[...]

    qT_all = qT_ref[...]
    eye = (lax.broadcasted_iota(jnp.int32, (LANES, LANES), 0)
           == lax.broadcasted_iota(jnp.int32, (LANES, LANES), 1)).astype(jnp.bfloat16)
    eye_t = jnp.concatenate([eye] * B_HEADS, axis=1)
    rhs_ref[...] = jnp.concatenate([_head_rhs(qT_all, A_Q), eye_t], axis=0)
    ones_rows = jnp.ones((ONES_ROWS, KEY_CHUNK), jnp.bfloat16)
    m_ref[...] = jnp.full(m_ref.shape, NEG_BIG, jnp.float32)
    o_ref[...] = jnp.zeros(o_ref.shape, jnp.float32)

    def softmax_heads(s_all, m_of):
        p_heads, maxes = [], []
        for h in range(nh):
            sl = slice(h * LANES, (h + 1) * LANES)
            s_h = s_all[:, sl]
            m_old = m_of(h, sl)
            m_new = jnp.maximum(m_old, jnp.max(s_h, axis=0, keepdims=True))
            p_heads.append(jnp.exp(s_h - m_new).astype(jnp.bfloat16))
            maxes.append((m_old, m_new))
        return p_heads, maxes

    def qk_of(c, valid):
        sc = score_ref[rows_of(c), :]
        kidx = c * KEY_CHUNK + row_iota
        sel = (sc > thr) | ((sc == thr) & (kidx <= jt))
        keep = jnp.where(valid, 0.0, NEG_BIG)
        bias = jnp.where(sel, keep, NEG_BIG).astype(jnp.bfloat16)
        lhs = jnp.concatenate([kN_ref[rows_of(c), LANES:2 * LANES], bias], axis=1)
        return jnp.dot(lhs, rhs_ref[...], preferred_element_type=jnp.float32)

    def softmax_stage(s_ref, p_ref, a_ref):
        p_heads, ms = softmax_heads(s_ref, lambda h, sl: m_ref[:, sl])
        for h in range(nh):
            sl = slice(h * LANES, (h + 1) * LANES)
            m_ref[:, sl] = ms[h][1]
            a_ref[:, sl] = jnp.exp(ms[h][0] - ms[h][1])
            p_ref[:, sl] = p_heads[h]

    def pv_stage(p_ref, a_ref, c):
        for g in range(B_KV_HEADS):
            rows = slice(A_KV + g * HEAD_DIM, A_KV + (g + 1) * HEAD_DIM)
            vg = jnp.concatenate([vT_ref[2 * c, rows, :], vT_ref[2 * c + 1, rows, :]], axis=1)
            vg = jnp.concatenate([vg, ones_rows], axis=0)
            csl = slice(g * GROUP * LANES, (g + 1) * GROUP * LANES)
            pv = jnp.dot(vg, p_ref[:, csl], preferred_element_type=jnp.float32)
            o_ref[:, csl] = o_ref[:, csl] * a_ref[:, csl] + pv[0:O_ROWS]

    sa_ref[...] = qk_of(0, True)
    pb_ref[...] = jnp.zeros(pb_ref.shape, pb_ref.dtype)
    ab_ref[...] = jnp.ones(ab_ref.shape, ab_ref.dtype)

    def dsa_body(i, carry):
        c0 = 2 * i
        c1 = jnp.minimum(c0 + 1, last)
        sb_ref[...] = qk_of(c1, c0 + 1 <= last)
        pv_stage(pb_ref, ab_ref, jnp.maximum(c0 - 1, 0))
        softmax_stage(sa_ref, pa_ref, aa_ref)
        sa_ref[...] = qk_of(jnp.minimum(c0 + 2, last), c0 + 2 <= last)
        pv_stage(pa_ref, aa_ref, c0)
        softmax_stage(sb_ref, pb_ref, ab_ref)
        return carry

    lax.fori_loop(0, npair, dsa_body, 0)
    pv_stage(pb_ref, ab_ref, jnp.minimum(2 * npair - 1, last))

    ob = o_ref[0:HEAD_DIM, :] * (1.0 / o_ref[HEAD_DIM:HEAD_DIM + 1, :])
    for h in range(nh):
        yT_ref[A_Q + h * HEAD_DIM:A_Q + (h + 1) * HEAD_DIM, :] = ob[:, h * LANES:(h + 1) * LANES]

    rhs_a = jnp.concatenate([_head_rhs(qT_all, 0), eye_t], axis=0)
    wblk = jnp.maximum(n - 1, 0)
    w0 = pl.multiple_of(wblk * BLOCK, BLOCK)
    delta = qidx - (w0 + row_iota)
    band = (delta >= 0) & (delta < WINDOW)
    bias = jnp.where(band, 0.0, NEG_BIG).astype(jnp.bfloat16)
    lhs = jnp.concatenate([kN_ref[pl.ds(w0, 2 * BLOCK), 0:LANES], bias], axis=1)
    s_all = jnp.dot(lhs, rhs_a, preferred_element_type=jnp.float32)
    p_heads, ms = softmax_heads(
        s_all, lambda h, sl: jnp.full((1, LANES), sink_ref[h], jnp.float32))
    for g in range(A_KV_HEADS):
        pT = jnp.concatenate(p_heads[g * GROUP:(g + 1) * GROUP], axis=1)
        rows = slice(g * HEAD_DIM, (g + 1) * HEAD_DIM)
        vg = jnp.concatenate([vT_ref[wblk, rows, :], vT_ref[wblk + 1, rows, :]], axis=1)
        vg = jnp.concatenate([vg, ones_rows], axis=0)
        og = jnp.dot(vg, pT, preferred_element_type=jnp.float32)
        for j in range(GROUP):
            h = g * GROUP + j
            sink, m = ms[h]
            csl = slice(j * LANES, (j + 1) * LANES)
            l = og[HEAD_DIM:HEAD_DIM + 1, csl] + jnp.exp(sink - m)
            yT_ref[h * HEAD_DIM:(h + 1) * HEAD_DIM, :] = og[0:HEAD_DIM, csl] * (1.0 / l)

    normed = []
    for r0, width, g_ref in ((0, A_Q, ga_ref), (A_Q, B_Q, gb_ref)):
        blk = yT_ref[r0:r0 + width, :]
        ms = jnp.sum(blk * blk, axis=0, keepdims=True) * (1.0 / width)
        normed.append((blk * lax.rsqrt(ms + RMS_EPS) * g_ref[...]).astype(jnp.bfloat16))
    y_t = jnp.concatenate(normed, axis=0)
    y = lax.dot_general(eye, y_t, (((1,), (1,)), ((), ())), preferred_element_type=jnp.float32)
    y_ref[...] = y.astype(y_ref.dtype)


def _attention(sinks, qT, kN, vT, wT, g_a, g_b):
    bsz, _, s = qT.shape
    nb = s // BLOCK
    topk = min(TOPK_MAX, s // 4)
    kern = functools.partial(_attn_kernel, seq=s, topk=topk)
    return pl.pallas_call(
        kern,
        out_shape=jax.ShapeDtypeStruct((bsz, s, A_Q + B_Q), jnp.bfloat16),
        grid=(bsz, nb),
        in_specs=[
            pl.BlockSpec(memory_space=pltpu.SMEM),
            pl.BlockSpec((None, Q_ROWS, BLOCK), lambda b, n: (b, 0, n)),
            pl.BlockSpec((None, s, K_ROWS), lambda b, n: (b, 0, 0)),
            pl.BlockSpec((None, nb, V_ROWS, LANES), lambda b, n: (b, 0, 0, 0)),
            pl.BlockSpec((None, W_ROWS, BLOCK), lambda b, n: (b, 0, n)),
            pl.BlockSpec((A_Q, LANES), lambda b, n: (0, 0)),
            pl.BlockSpec((B_Q, LANES), lambda b, n: (0, 0)),
        ],
        out_specs=pl.BlockSpec((None, BLOCK, A_Q + B_Q), lambda b, n: (b, n, 0)),
        scratch_shapes=[
            pltpu.VMEM((s, LANES), jnp.float32),
            pltpu.VMEM((1, LANES), jnp.float32),
            pltpu.VMEM((1, LANES), jnp.int32),
            pltpu.VMEM((1, B_HEADS * LANES), jnp.float32),
            pltpu.VMEM((O_ROWS, B_HEADS * LANES), jnp.float32),
            pltpu.VMEM((A_Q + B_Q, LANES), jnp.float32),
            pltpu.VMEM((KEY_CHUNK, B_HEADS * LANES), jnp.float32),
            pltpu.VMEM((KEY_CHUNK, B_HEADS * LANES), jnp.float32),
            pltpu.VMEM((KEY_CHUNK, B_HEADS * LANES), jnp.bfloat16),
            pltpu.VMEM((KEY_CHUNK, B_HEADS * LANES), jnp.bfloat16),
            pltpu.VMEM((1, B_HEADS * LANES), jnp.float32),
            pltpu.VMEM((1, B_HEADS * LANES), jnp.float32),
            pltpu.VMEM((2 * LANES, B_HEADS * LANES), jnp.bfloat16),
        ],
        compiler_params=pltpu.CompilerParams(
            dimension_semantics=("arbitrary", "arbitrary"),
            vmem_limit_bytes=VMEM_LIMIT),
        name="attention",
    )(sinks, qT, kN, vT, wT, g_a, g_b)


def _outproj_kernel(y_ref, x_ref, gt_ref, w_ref, g_ref, b_ref, o_ref, *, alpha):
    proj = jnp.dot(y_ref[...], w_ref[...], preferred_element_type=jnp.float32)
    z = alpha * x_ref[...] + (1.0 + gt_ref[...]) * proj
    o_ref[...] = _layer_norm(z, g_ref[...], b_ref[...])


def _outproj_ln(y, x, gt, w_o, g, b, *, alpha, tm):
    bsz, s, d = x.shape
    dy = y.shape[-1]
    return pl.pallas_call(
        functools.partial(_outproj_kernel, alpha=alpha),
        out_shape=jax.ShapeDtypeStruct((bsz, s, d), jnp.float32),
        grid=(bsz, s // tm),
        in_specs=[
            pl.BlockSpec((None, tm, dy), lambda b_, i: (b_, i, 0)),
            pl.BlockSpec((None, tm, d), lambda b_, i: (b_, i, 0)),
            pl.BlockSpec((None, 1, d), lambda b_, i: (b_, 0, 0)),
            pl.BlockSpec((dy, d), lambda b_, i: (0, 0)),
            pl.BlockSpec((1, d), lambda b_, i: (0, 0)),
            pl.BlockSpec((1, d), lambda b_, i: (0, 0)),
        ],
        out_specs=pl.BlockSpec((None, tm, d), lambda b_, i: (b_, i, 0)),
        compiler_params=pltpu.CompilerParams(
            dimension_semantics=("arbitrary", "arbitrary"),
            vmem_limit_bytes=VMEM_LIMIT),
        name="out_proj_ln",
    )(y, x, gt, w_o, g, b)


def _ffn_kernel(x_ref, sc_ref, sh_ref, gt_ref, wg_ref, wu_ref, wd_ref, g_ref, b_ref, o_ref, *, alpha):
    x = x_ref[...]
    h = (x * (1.0 + sc_ref[...]) + sh_ref[...]).astype(jnp.bfloat16)
    gate = jnp.dot(h, wg_ref[...], preferred_element_type=jnp.float32)
    up = jnp.dot(h, wu_ref[...], preferred_element_type=jnp.float32)
    act = (_silu(gate) * up).astype(jnp.bfloat16)
    y = jnp.dot(act, wd_ref[...], preferred_element_type=jnp.float32)
    z = alpha * x + (1.0 + gt_ref[...]) * y
    o_ref[...] = _layer_norm(z, g_ref[...], b_ref[...])


def _ffn_ln(x, sc, sh, gt, w_gate, w_up, w_down, g, b, *, alpha, tm):
    bsz, s, d = x.shape
    dff = w_gate.shape[-1]
    resident = pl.Buffered(1)
    return pl.pallas_call(
        functools.partial(_ffn_kernel, alpha=alpha),
        out_shape=jax.ShapeDtypeStruct((bsz, s, d), jnp.float32),
        grid=(bsz, s // tm),
        in_specs=[
            pl.BlockSpec((None, tm, d), lambda b_, i: (b_, i, 0)),
            pl.BlockSpec((None, 1, d), lambda b_, i: (b_, 0, 0)),
            pl.BlockSpec((None, 1, d), lambda b_, i: (b_, 0, 0)),
            pl.BlockSpec((None, 1, d), lambda b_, i: (b_, 0, 0)),
            pl.BlockSpec((d, dff), lambda b_, i: (0, 0), pipeline_mode=resident),
            pl.BlockSpec((d, dff), lambda b_, i: (0, 0), pipeline_mode=resident),
            pl.BlockSpec((dff, d), lambda b_, i: (0, 0), pipeline_mode=resident),
            pl.BlockSpec((1, d), lambda b_, i: (0, 0)),
            pl.BlockSpec((1, d), lambda b_, i: (0, 0)),
        ],
        out_specs=pl.BlockSpec((None, tm, d), lambda b_, i: (b_, i, 0)),
        compiler_params=pltpu.CompilerParams(
            dimension_semantics=("arbitrary", "arbitrary"),
            vmem_limit_bytes=VMEM_LIMIT),
        name="ffn_ln",
    )(x, sc, sh, gt, w_gate, w_up, w_down, g, b)


def _regroup_w_in(w_in_l):
    d = w_in_l.shape[0]
    splits = np.cumsum([A_Q, A_KV, A_KV, B_Q, B_KV, B_KV, I_Q, IDX_DIM, IDX_HEADS])
    qa, ka, va, qb, kb, vb, qi, ki, wi = jnp.split(w_in_l, splits[:-1], axis=1)
    zk = jnp.zeros((d, K_ROWS - A_KV - B_KV - IDX_DIM), w_in_l.dtype)
    zw = jnp.zeros((d, 2 * W_ROWS - IDX_HEADS), w_in_l.dtype)
    cols = jnp.concatenate([qa, qb, qi, ka, kb, ki, zk, va, vb, wi, zw], axis=1)
    return cols.T.astype(jnp.bfloat16)


def kernel(x, c, positions, w_ada, b_ada, w_in, attn_sinks, g_a, g_b, w_o,
           ln1_g, ln1_b, w_gate, w_up, w_down, ln2_g, ln2_b):
    bsz, s, d = x.shape
    depth = w_ada.shape[0]
    alpha = (2.0 * depth) ** 0.25
    bf = jnp.bfloat16

    mod = _adaln_mod(c, w_ada, b_ada)
    inv = (ROPE_THETA ** (-jnp.arange(HALF, dtype=jnp.float32) / HALF)).reshape(HALF, 1)
    pos3 = positions.reshape(bsz, 1, s)

    for l in range(depth):
        m = mod[l].reshape(bsz, 1, 6 * d)
        sh1, sc1, gt1, sh2, sc2, gt2 = [m[:, :, i * d:(i + 1) * d] for i in range(6)]
        qT, kN, vT, wT = _inproj(x, sc1, sh1, pos3, inv, _regroup_w_in(w_in[l]), tm=512)
        y = _attention(attn_sinks[l], qT, kN, vT, wT,
                       jnp.broadcast_to(g_a[l][:, None], (A_Q, LANES)),
                       jnp.broadcast_to(g_b[l][:, None], (B_Q, LANES)))
        x = _outproj_ln(y, x, gt1, w_o[l].astype(bf), ln1_g[l].reshape(1, d),
                        ln1_b[l].reshape(1, d), alpha=alpha, tm=512)
        x = _ffn_ln(x, sc2, sh2, gt2, w_gate[l].astype(bf), w_up[l].astype(bf),
                    w_down[l].astype(bf), ln2_g[l].reshape(1, d), ln2_b[l].reshape(1, d),
                    alpha=alpha, tm=512)
    return x
```

```python
import functools

import jax
import jax.numpy as jnp
import numpy as np
from jax import lax
from jax.experimental import pallas as pl
from jax.experimental.pallas import tpu as pltpu

HEAD_DIM = 64
HALF = HEAD_DIM // 2
A_HEADS = 8
A_KV_HEADS = 2
B_HEADS = 8
B_KV_HEADS = 2
IDX_HEADS = 8
IDX_DIM = 64
WINDOW = 128
BLOCK = 128
TOPK_MAX = 256
ROPE_THETA = 10000.0
LN_EPS = 1e-5
RMS_EPS = 1e-6

A_Q = A_HEADS * HEAD_DIM
A_KV = A_KV_HEADS * HEAD_DIM
B_Q = B_HEADS * HEAD_DIM
B_KV = B_KV_HEADS * HEAD_DIM
I_Q = IDX_HEADS * IDX_DIM
GROUP = A_HEADS // A_KV_HEADS

LANES = 128
KEY_CHUNK = 256
ONES_ROWS = 16
O_ROWS = HEAD_DIM + 8
NEG_BIG = -1e30
INT_MIN = -(2 ** 31)
VMEM_LIMIT = 56 * 1024 * 1024

Q_ROWS = A_Q + B_Q + I_Q
K_ROWS = 3 * LANES
V_ROWS = A_KV + B_KV
W_ROWS = 8
P_ROWS = Q_ROWS + K_ROWS + V_ROWS + 2 * W_ROWS


def _silu(x):
    return x * (1.0 / (1.0 + jnp.exp(-x)))


def _layer_norm(z, g, b):
    mu = jnp.mean(z, axis=-1, keepdims=True)
    zc = z - mu
    var = jnp.mean(zc * zc, axis=-1, keepdims=True)
    return zc * lax.rsqrt(var + LN_EPS) * g + b


def _mod_kernel(c_ref, w_ref, b_ref, o_ref):
    c_act = _silu(c_ref[...]).astype(jnp.bfloat16)
    o_ref[...] = jnp.dot(c_act, w_ref[...].astype(jnp.bfloat16),
                         preferred_element_type=jnp.float32) + b_ref[...]


def _adaln_mod(c, w_ada, b_ada):
    depth, d, d6 = w_ada.shape
    bsz = c.shape[0]
    tn = 1024
    return pl.pallas_call(
        _mod_kernel,
        out_shape=jax.ShapeDtypeStruct((depth, bsz, d6), jnp.float32),
        grid=(depth, d6 // tn),
        in_specs=[
            pl.BlockSpec((bsz, d), lambda l, j: (0, 0)),
            pl.BlockSpec((None, d, tn), lambda l, j: (l, 0, j)),
            pl.BlockSpec((None, 1, tn), lambda l, j: (l, 0, j)),
        ],
        out_specs=pl.BlockSpec((None, bsz, tn), lambda l, j: (l, 0, j)),
        compiler_params=pltpu.CompilerParams(
            dimension_semantics=("arbitrary", "arbitrary"),
            vmem_limit_bytes=VMEM_LIMIT),
        name="adaln_mod",
    )(c, w_ada, b_ada.reshape(depth, 1, d6))


def _rope_head(x, r, cos, sin, scale=None):
    x1, x2 = x[r:r + HALF], x[r + HALF:r + HEAD_DIM]
    o1 = x1 * cos - x2 * sin
    o2 = x2 * cos + x1 * sin
    if scale is not None:
        o1, o2 = o1 * scale, o2 * scale
    return o1, o2


def _inproj_kernel(x_ref, sc_ref, sh_ref, pos_ref, inv_ref, w_ref,
                   qT_ref, kN_ref, vT_ref, wT_ref, *, w_scale):
    h = x_ref[...] * (1.0 + sc_ref[...]) + sh_ref[...]
    pT = lax.dot_general(w_ref[...], h.astype(jnp.bfloat16),
                         (((1,), (1,)), ((), ())),
                         preferred_element_type=jnp.float32)
    ang = inv_ref[...] * pos_ref[...].astype(jnp.float32)
    cos, sin = jnp.cos(ang), jnp.sin(ang)

    nblk = vT_ref.shape[0]
    for hd in range(Q_ROWS // HEAD_DIM):
        r = hd * HEAD_DIM
        scale = HEAD_DIM ** -0.5 if r < A_Q + B_Q else None
        o1, o2 = _rope_head(pT, r, cos, sin, scale)
        o1, o2 = o1.astype(qT_ref.dtype), o2.astype(qT_ref.dtype)
        for j in range(nblk):
            qT_ref[j, r:r + HALF, :] = o1[:, j * LANES:(j + 1) * LANES]
            qT_ref[j, r + HALF:r + HEAD_DIM, :] = o2[:, j * LANES:(j + 1) * LANES]

    k_rows = []
    for hd in range(K_ROWS // HEAD_DIM):
        k_rows.extend(_rope_head(pT, Q_ROWS + hd * HEAD_DIM, cos, sin))
    kN_ref[...] = jnp.concatenate(k_rows, axis=0).T.astype(kN_ref.dtype)

    v0 = Q_ROWS + K_ROWS
    v = pT[v0:v0 + V_ROWS].astype(vT_ref.dtype)
    w0 = v0 + V_ROWS
    w = pT[w0:w0 + W_ROWS] * w_scale
    for j in range(nblk):
        vT_ref[j] = v[:, j * LANES:(j + 1) * LANES]
        wT_ref[j] = w[:, j * LANES:(j + 1) * LANES]


def _inproj(x, sc, sh, pos3, inv, w_t, *, tm):
    bsz, s, d = x.shape
    kern = functools.partial(_inproj_kernel, w_scale=IDX_HEADS ** -0.5 * IDX_DIM ** -0.5)
    return pl.pallas_call(
        kern,
        out_shape=(
            jax.ShapeDtypeStruct((bsz, s // LANES, Q_ROWS, LANES), jnp.bfloat16),
            jax.ShapeDtypeStruct((bsz, s, K_ROWS), jnp.bfloat16),
            jax.ShapeDtypeStruct((bsz, s // LANES, V_ROWS, LANES), jnp.bfloat16),
            jax.ShapeDtypeStruct((bsz, s // LANES, W_ROWS, LANES), jnp.float32),
        ),
        grid=(bsz, s // tm),
        in_specs=[
            pl.BlockSpec((None, tm, d), lambda b, i: (b, i, 0)),
            pl.BlockSpec((None, 1, d), lambda b, i: (b, 0, 0)),
            pl.BlockSpec((None, 1, d), lambda b, i: (b, 0, 0)),
            pl.BlockSpec((None, 1, tm), lambda b, i: (b, 0, i)),
            pl.BlockSpec((HALF, 1), lambda b, i: (0, 0)),
            pl.BlockSpec((P_ROWS, d), lambda b, i: (0, 0)),
        ],
        out_specs=(
            pl.BlockSpec((None, tm // LANES, Q_ROWS, LANES), lambda b, i: (b, i, 0, 0)),
            pl.BlockSpec((None, tm, K_ROWS), lambda b, i: (b, i, 0)),
            pl.BlockSpec((None, tm // LANES, V_ROWS, LANES), lambda b, i: (b, i, 0, 0)),
            pl.BlockSpec((None, tm // LANES, W_ROWS, LANES), lambda b, i: (b, i, 0, 0)),
        ),
        compiler_params=pltpu.CompilerParams(
            dimension_semantics=("arbitrary", "arbitrary"),
            vmem_limit_bytes=VMEM_LIMIT),
        name="in_proj",
    )(x, sc, sh, pos3, inv, w_t)


def _ordered_to_f32(c):
    bits = jnp.where(c >= 0, c, c ^ jnp.int32(0x7FFFFFFF))
    return lax.bitcast_convert_type(bits, jnp.float32)


def _head_rhs(qT, row0):
    zeros = jnp.zeros((HEAD_DIM, LANES), qT.dtype)
    cols = []
    for h in range(A_HEADS):
        blk = qT[row0 + h * HEAD_DIM:row0 + (h + 1) * HEAD_DIM, :]
        g = h // GROUP
        cols.append(jnp.concatenate([blk, zeros] if g == 0 else [zeros, blk], axis=0))
    return jnp.concatenate(cols, axis=1)


def _attn_kernel(sink_ref, qT_ref, kN_ref, vT_ref, wT_ref, ga_ref, gb_ref,
                 y_ref, score_ref, thr_ref, jt_ref, m_ref, o_ref, yT_ref,
                 sa_ref, sb_ref, pa_ref, pb_ref, aa_ref, ab_ref, rhs_ref,
                 *, seq, topk):
    n = pl.program_id(1)
    q0 = n * BLOCK
    nk = (n + 2) // 2
    qidx = q0 + lax.broadcasted_iota(jnp.int32, (KEY_CHUNK, LANES), 1)
    row_iota = lax.broadcasted_iota(jnp.int32, (KEY_CHUNK, LANES), 0)
    nh = B_HEADS

    for h in range(IDX_HEADS):
        rhs_ref[0:IDX_DIM, h * LANES:(h + 1) * LANES] = (
            qT_ref[A_Q + B_Q + h * IDX_DIM:A_Q + B_Q + (h + 1) * IDX_DIM, :])

    npair = (nk + 1) // 2
    last = nk - 1

    def rows_of(c):
        return pl.ds(pl.multiple_of(c * KEY_CHUNK, KEY_CHUNK), KEY_CHUNK)

    def rel_of(c):
        ki = kN_ref[rows_of(c), 2 * LANES:2 * LANES + IDX_DIM]
        return jnp.dot(ki, rhs_ref[0:IDX_DIM, :], preferred_element_type=jnp.float32)

    def score_from(s_ref, c):
        sc = jnp.zeros((KEY_CHUNK, LANES), jnp.float32)
        for h in range(IDX_HEADS):
            sc = sc + jnp.maximum(s_ref[:, h * LANES:(h + 1) * LANES], 0.0) * wT_ref[h:h + 1, :]
        kidx = c * KEY_CHUNK + row_iota
        score_ref[rows_of(c), :] = jnp.where(kidx <= qidx, sc, -jnp.inf)

    sa_ref[...] = rel_of(0)

    def score_body(i, carry):
        c0 = 2 * i
        c1 = jnp.minimum(c0 + 1, last)
        sb_ref[...] = rel_of(c1)
        score_from(sa_ref, c0)
        sa_ref[...] = rel_of(jnp.minimum(c0 + 2, last))
        score_from(sb_ref, c1)
        return carry

    lax.fori_loop(0, npair, score_body, 0)

    def fold_rows(x):
        return x.reshape(KEY_CHUNK // 8, 8, LANES).sum(axis=0)

    def count(pred_fn):
        def body(c, acc):
            r0 = pl.multiple_of(c * KEY_CHUNK, KEY_CHUNK)
            blk = score_ref[pl.ds(r0, KEY_CHUNK), :]
            return acc + fold_rows(pred_fn(blk, r0).astype(jnp.int32))
        acc = lax.fori_loop(0, nk, body, jnp.zeros((8, LANES), jnp.int32))
        return acc.sum(axis=0, keepdims=True)

    def masked_min(pred_fn):
        def body(c, acc):
            r0 = pl.multiple_of(c * KEY_CHUNK, KEY_CHUNK)
            blk = score_ref[pl.ds(r0, KEY_CHUNK), :]
            v = jnp.where(pred_fn(blk, r0), blk, jnp.inf)
            return jnp.minimum(acc, v.reshape(KEY_CHUNK // 8, 8, LANES).min(axis=0))
        acc = lax.fori_loop(0, nk, body, jnp.full((8, LANES), jnp.inf, jnp.float32))
        return acc.min(axis=0, keepdims=True)

    def search_static(nkk):
        def count_ge(cf):
            acc = jnp.zeros((8, LANES), jnp.int32)
            for c in range(nkk):
                blk = score_ref[c * KEY_CHUNK:(c + 1) * KEY_CHUNK, :]
                acc = acc + fold_rows((blk >= cf).astype(jnp.int32))
            return acc.sum(axis=0, keepdims=True)

        def run():
            c0 = count_ge(0.0)
            nonneg = c0 >= topk
            st0 = (jnp.where(nonneg, jnp.int32(0), jnp.int32(INT_MIN)),
                   jnp.where(nonneg, c0, jnp.int32(nkk * KEY_CHUNK)))

            def bit_body(i, st):
                pfx, c_pfx = st
                cand = pfx + (jnp.int32(1) << (30 - i))
                cnt = count_ge(_ordered_to_f32(cand))
                take = cnt >= topk
                return jnp.where(take, cand, pfx), jnp.where(take, cnt, c_pfx)

            pfx, c_pfx = lax.fori_loop(0, 31, bit_body, st0)
            return _ordered_to_f32(pfx), c_pfx

        return run

    max_nk = (seq // BLOCK + 1) // 2
    lo, c_lo = lax.cond(
        n >= 2,
        lambda: lax.switch(nk - 2, [search_static(j) for j in range(2, max_nk + 1)]),
        lambda: (jnp.full((1, LANES), -jnp.inf, jnp.float32), jnp.zeros((1, LANES), jnp.int32)))
    thr_ref[...] = lo
    jt_ref[...] = jnp.full((1, LANES), jnp.where(n >= 2, seq, -1), jnp.int32)
    unresolved = jnp.max(jnp.where(c_lo > topk, 1, 0), axis=1, keepdims=True)[0, 0] > 0

    qT_all = qT_ref[...]
    eye = (lax.broadcasted_iota(jnp.int32, (LANES, LANES), 0)
           == lax.broadcasted_iota(jnp.int32, (LANES, LANES), 1)).astype(jnp.bfloat16)
    eye_t = jnp.concatenate([eye] * B_HEADS, axis=1)
    rhs_ref[...] = jnp.concatenate([_head_rhs(qT_all, A_Q), eye_t], axis=0)
    ones_rows = jnp.ones((ONES_ROWS, KEY_CHUNK), jnp.bfloat16)
    m_ref[...] = jnp.full(m_ref.shape, NEG_BIG, jnp.float32)
    o_ref[...] = jnp.zeros(o_ref.shape, jnp.float32)
    pb_ref[...] = jnp.zeros(pb_ref.shape, pb_ref.dtype)
    ab_ref[...] = jnp.ones(ab_ref.shape, ab_ref.dtype)

    @pl.when(unresolved)
    def _resolve_ties():
        def counts(bv):
            return (count(lambda blk, r0: blk > bv), count(lambda blk, r0: blk >= bv))

        b0 = masked_min(lambda blk, r0: blk >= lo)
        cgt0, cge0 = counts(b0)

        def w_cond(st):
            return jnp.max(st[1], axis=1, keepdims=True)[0, 0] >= topk

        def w_body(st):
            bv, cgt, _ = st
            nxt = masked_min(lambda blk, r0: blk > bv)
            bn = jnp.where(cgt >= topk, nxt, bv)
            cgt2, cge2 = counts(bn)
            return bn, cgt2, cge2

        bv, cgt, _ = lax.while_loop(w_cond, w_body, (b0, cgt0, cge0))
        thr_ref[...] = bv
        need = topk - cgt
        nbits = max(1, int(np.ceil(np.log2(seq))))

        def jbody(i, pj):
            step = jnp.int32(1) << (nbits - 1 - i)
            cand = pj + step - 1
            f = count(lambda blk, r0: (blk == bv) & ((r0 + row_iota) <= cand))
            return jnp.where(f >= need, pj, pj + step)

        jt_ref[...] = lax.fori_loop(0, nbits, jbody, jnp.zeros((1, LANES), jnp.int32))

    thr = thr_ref[...]
    jt = jt_ref[...]

    def softmax_heads(s_all, m_of):
        p_heads, maxes = [], []
        for h in range(nh):
            sl = slice(h * LANES, (h + 1) * LANES)
            s_h = s_all[:, sl]
            m_old = m_of(h, sl)
            m_new = jnp.maximum(m_old, jnp.max(s_h, axis=0, keepdims=True))
            p_heads.append(jnp.exp(s_h - m_new).astype(jnp.bfloat16))
            maxes.append((m_old, m_new))
        return p_heads, maxes

    def qk_of(c, valid):
        sc = score_ref[rows_of(c), :]
        kidx = c * KEY_CHUNK + row_iota
        sel = (sc > thr) | ((sc == thr) & (kidx <= jt))
        keep = jnp.where(valid, 0.0, NEG_BIG)
        bias = jnp.where(sel, keep, NEG_BIG).astype(jnp.bfloat16)
        lhs = jnp.concatenate([kN_ref[rows_of(c), LANES:2 * LANES], bias], axis=1)
        return jnp.dot(lhs, rhs_ref[...], preferred_element_type=jnp.float32)

    def softmax_stage(s_ref, p_ref, a_ref):
        p_heads, ms = softmax_heads(s_ref, lambda h, sl: m_ref[:, sl])
        for h in range(nh):
            sl = slice(h * LANES, (h + 1) * LANES)
            m_ref[:, sl] = ms[h][1]
            a_ref[:, sl] = jnp.exp(ms[h][0] - ms[h][1])
            p_ref[:, sl] = p_heads[h]

    def pv_stage(p_ref, a_ref, c):
        for g in range(B_KV_HEADS):
            rows = slice(A_KV + g * HEAD_DIM, A_KV + (g + 1) * HEAD_DIM)
            vg = jnp.concatenate([vT_ref[2 * c, rows, :], vT_ref[2 * c + 1, rows, :]], axis=1)
            vg = jnp.concatenate([vg, ones_rows], axis=0)
            csl = slice(g * GROUP * LANES, (g + 1) * GROUP * LANES)
            pv = jnp.dot(vg, p_ref[:, csl], preferred_element_type=jnp.float32)
            o_ref[:, csl] = o_ref[:, csl] * a_ref[:, csl] + pv[0:O_ROWS]

    sa_ref[...] = qk_of(0, True)

    rhs_a = jnp.concatenate([_head_rhs(qT_all, 0), eye_t], axis=0)
    wblk = jnp.maximum(n - 1, 0)
    w0 = pl.multiple_of(wblk * BLOCK, BLOCK)
    delta = qidx - (w0 + row_iota)
    band = (delta >= 0) & (delta < WINDOW)
    bias = jnp.where(band, 0.0, NEG_BIG).astype(jnp.bfloat16)
    lhs = jnp.concatenate([kN_ref[pl.ds(w0, 2 * BLOCK), 0:LANES], bias], axis=1)
    s_all = jnp.dot(lhs, rhs_a, preferred_element_type=jnp.float32)
    p_heads, ms = softmax_heads(
        s_all, lambda h, sl: jnp.full((1, LANES), sink_ref[h], jnp.float32))
    for g in range(A_KV_HEADS):
        pT = jnp.concatenate(p_heads[g * GROUP:(g + 1) * GROUP], axis=1)
        rows = slice(g * HEAD_DIM, (g + 1) * HEAD_DIM)
        vg = jnp.concatenate([vT_ref[wblk, rows, :], vT_ref[wblk + 1, rows, :]], axis=1)
        vg = jnp.concatenate([vg, ones_rows], axis=0)
        og = jnp.dot(vg, pT, preferred_element_type=jnp.float32)
        for j in range(GROUP):
            h = g * GROUP + j
            sink, m = ms[h]
            csl = slice(j * LANES, (j + 1) * LANES)
            l = og[HEAD_DIM:HEAD_DIM + 1, csl] + jnp.exp(sink - m)
            yT_ref[h * HEAD_DIM:(h + 1) * HEAD_DIM, :] = og[0:HEAD_DIM, csl] * (1.0 / l)

    def dsa_body(i, carry):
        c0 = 2 * i
        c1 = jnp.minimum(c0 + 1, last)
        sb_ref[...] = qk_of(c1, c0 + 1 <= last)
        pv_stage(pb_ref, ab_ref, jnp.maximum(c0 - 1, 0))
        softmax_stage(sa_ref, pa_ref, aa_ref)
        sa_ref[...] = qk_of(jnp.minimum(c0 + 2, last), c0 + 2 <= last)
        pv_stage(pa_ref, aa_ref, c0)
        softmax_stage(sb_ref, pb_ref, ab_ref)
        return carry

    lax.fori_loop(0, npair, dsa_body, 0)
    pv_stage(pb_ref, ab_ref, jnp.minimum(2 * npair - 1, last))

    ob = o_ref[0:HEAD_DIM, :] * (1.0 / o_ref[HEAD_DIM:HEAD_DIM + 1, :])
    for h in range(nh):
        yT_ref[A_Q + h * HEAD_DIM:A_Q + (h + 1) * HEAD_DIM, :] = ob[:, h * LANES:(h + 1) * LANES]

    normed = []
    for r0, width, g_ref in ((0, A_Q, ga_ref), (A_Q, B_Q, gb_ref)):
        blk = yT_ref[r0:r0 + width, :]
        ms = jnp.sum(blk * blk, axis=0, keepdims=True) * (1.0 / width)
        normed.append((blk * lax.rsqrt(ms + RMS_EPS) * g_ref[...]).astype(jnp.bfloat16))
    y_t = jnp.concatenate(normed, axis=0)
    y = lax.dot_general(eye, y_t, (((1,), (1,)), ((), ())), preferred_element_type=jnp.float32)
    y_ref[...] = y.astype(y_ref.dtype)


def _attention(sinks, qT, kN, vT, wT, g_a, g_b):
    bsz, s, _ = kN.shape
    nb = s // BLOCK
    topk = min(TOPK_MAX, s // 4)
    kern = functools.partial(_attn_kernel, seq=s, topk=topk)
    return pl.pallas_call(
        kern,
        out_shape=jax.ShapeDtypeStruct((bsz, s, A_Q + B_Q), jnp.bfloat16),
        grid=(bsz, nb),
        in_specs=[
            pl.BlockSpec(memory_space=pltpu.SMEM),
            pl.BlockSpec((None, None, Q_ROWS, BLOCK), lambda b, n: (b, n, 0, 0)),
            pl.BlockSpec((None, s, K_ROWS), lambda b, n: (b, 0, 0)),
            pl.BlockSpec((None, nb, V_ROWS, LANES), lambda b, n: (b, 0, 0, 0)),
            pl.BlockSpec((None, None, W_ROWS, BLOCK), lambda b, n: (b, n, 0, 0)),
            pl.BlockSpec((A_Q, LANES), lambda b, n: (0, 0)),
            pl.BlockSpec((B_Q, LANES), lambda b, n: (0, 0)),
        ],
        out_specs=pl.BlockSpec((None, BLOCK, A_Q + B_Q), lambda b, n: (b, n, 0)),
        scratch_shapes=[
            pltpu.VMEM((s, LANES), jnp.float32),
            pltpu.VMEM((1, LANES), jnp.float32),
            pltpu.VMEM((1, LANES), jnp.int32),
            pltpu.VMEM((1, B_HEADS * LANES), jnp.float32),
            pltpu.VMEM((O_ROWS, B_HEADS * LANES), jnp.float32),
            pltpu.VMEM((A_Q + B_Q, LANES), jnp.float32),
            pltpu.VMEM((KEY_CHUNK, B_HEADS * LANES), jnp.float32),
            pltpu.VMEM((KEY_CHUNK, B_HEADS * LANES), jnp.float32),
            pltpu.VMEM((KEY_CHUNK, B_HEADS * LANES), jnp.bfloat16),
            pltpu.VMEM((KEY_CHUNK, B_HEADS * LANES), jnp.bfloat16),
            pltpu.VMEM((1, B_HEADS * LANES), jnp.float32),
            pltpu.VMEM((1, B_HEADS * LANES), jnp.float32),
            pltpu.VMEM((2 * LANES, B_HEADS * LANES), jnp.bfloat16),
        ],
        compiler_params=pltpu.CompilerParams(
            dimension_semantics=("arbitrary", "arbitrary"),
            vmem_limit_bytes=VMEM_LIMIT),
        name="attention",
    )(sinks, qT, kN, vT, wT, g_a, g_b)


def _outproj_kernel(y_ref, x_ref, gt_ref, w_ref, g_ref, b_ref, o_ref, *, alpha):
    proj = jnp.dot(y_ref[...], w_ref[...], preferred_element_type=jnp.float32)
    z = alpha * x_ref[...] + (1.0 + gt_ref[...]) * proj
    o_ref[...] = _layer_norm(z, g_ref[...], b_ref[...])


def _outproj_ln(y, x, gt, w_o, g, b, *, alpha, tm):
    bsz, s, d = x.shape
    dy = y.shape[-1]
    return pl.pallas_call(
        functools.partial(_outproj_kernel, alpha=alpha),
        out_shape=jax.ShapeDtypeStruct((bsz, s, d), jnp.float32),
        grid=(bsz, s // tm),
        in_specs=[
            pl.BlockSpec((None, tm, dy), lambda b_, i: (b_, i, 0)),
            pl.BlockSpec((None, tm, d), lambda b_, i: (b_, i, 0)),
            pl.BlockSpec((None, 1, d), lambda b_, i: (b_, 0, 0)),
            pl.BlockSpec((dy, d), lambda b_, i: (0, 0)),
            pl.BlockSpec((1, d), lambda b_, i: (0, 0)),
            pl.BlockSpec((1, d), lambda b_, i: (0, 0)),
        ],
        out_specs=pl.BlockSpec((None, tm, d), lambda b_, i: (b_, i, 0)),
        compiler_params=pltpu.CompilerParams(
            dimension_semantics=("arbitrary", "arbitrary"),
            vmem_limit_bytes=VMEM_LIMIT),
        name="out_proj_ln",
    )(y, x, gt, w_o, g, b)


def _ffn_kernel(x_ref, sc_ref, sh_ref, gt_ref, wg_ref, wu_ref, wd_ref, g_ref, b_ref, o_ref, *, alpha):
    x = x_ref[...]
    h = (x * (1.0 + sc_ref[...]) + sh_ref[...]).astype(jnp.bfloat16)
    gate = jnp.dot(h, wg_ref[...], preferred_element_type=jnp.float32)
    up = jnp.dot(h, wu_ref[...], preferred_element_type=jnp.float32)
    act = (_silu(gate) * up).astype(jnp.bfloat16)
    y = jnp.dot(act, wd_ref[...], preferred_element_type=jnp.float32)
    z = alpha * x + (1.0 + gt_ref[...]) * y
    o_ref[...] = _layer_norm(z, g_ref[...], b_ref[...])


def _ffn_ln(x, sc, sh, gt, w_gate, w_up, w_down, g, b, *, alpha, tm):
    bsz, s, d = x.shape
    dff = w_gate.shape[-1]
    resident = pl.Buffered(1)
    return pl.pallas_call(
        functools.partial(_ffn_kernel, alpha=alpha),
        out_shape=jax.ShapeDtypeStruct((bsz, s, d), jnp.float32),
        grid=(bsz, s // tm),
        in_specs=[
            pl.BlockSpec((None, tm, d), lambda b_, i: (b_, i, 0)),
            pl.BlockSpec((None, 1, d), lambda b_, i: (b_, 0, 0)),
            pl.BlockSpec((None, 1, d), lambda b_, i: (b_, 0, 0)),
            pl.BlockSpec((None, 1, d), lambda b_, i: (b_, 0, 0)),
            pl.BlockSpec((d, dff), lambda b_, i: (0, 0), pipeline_mode=resident),
            pl.BlockSpec((d, dff), lambda b_, i: (0, 0), pipeline_mode=resident),
            pl.BlockSpec((dff, d), lambda b_, i: (0, 0), pipeline_mode=resident),
            pl.BlockSpec((1, d), lambda b_, i: (0, 0)),
            pl.BlockSpec((1, d), lambda b_, i: (0, 0)),
        ],
        out_specs=pl.BlockSpec((None, tm, d), lambda b_, i: (b_, i, 0)),
        compiler_params=pltpu.CompilerParams(
            dimension_semantics=("arbitrary", "arbitrary"),
            vmem_limit_bytes=VMEM_LIMIT),
        name="ffn_ln",
    )(x, sc, sh, gt, w_gate, w_up, w_down, g, b)


def _regroup_w_in(w_in_l):
    d = w_in_l.shape[0]
    splits = np.cumsum([A_Q, A_KV, A_KV, B_Q, B_KV, B_KV, I_Q, IDX_DIM, IDX_HEADS])
    qa, ka, va, qb, kb, vb, qi, ki, wi = jnp.split(w_in_l, splits[:-1], axis=1)
    zk = jnp.zeros((d, K_ROWS - A_KV - B_KV - IDX_DIM), w_in_l.dtype)
    zw = jnp.zeros((d, 2 * W_ROWS - IDX_HEADS), w_in_l.dtype)
    cols = jnp.concatenate([qa, qb, qi, ka, kb, ki, zk, va, vb, wi, zw], axis=1)
    return cols.T.astype(jnp.bfloat16)


def kernel(x, c, positions, w_ada, b_ada, w_in, attn_sinks, g_a, g_b, w_o,
           ln1_g, ln1_b, w_gate, w_up, w_down, ln2_g, ln2_b):
    bsz, s, d = x.shape
    depth = w_ada.shape[0]
    alpha = (2.0 * depth) ** 0.25
    bf = jnp.bfloat16

    mod = _adaln_mod(c, w_ada, b_ada)
    inv = (ROPE_THETA ** (-jnp.arange(HALF, dtype=jnp.float32) / HALF)).reshape(HALF, 1)
    pos3 = positions.reshape(bsz, 1, s)

    for l in range(depth):
        m = mod[l].reshape(bsz, 1, 6 * d)
        sh1, sc1, gt1, sh2, sc2, gt2 = [m[:, :, i * d:(i + 1) * d] for i in range(6)]
        qT, kN, vT, wT = _inproj(x, sc1, sh1, pos3, inv, _regroup_w_in(w_in[l]), tm=512)
        y = _attention(attn_sinks[l], qT, kN, vT, wT,
                       jnp.broadcast_to(g_a[l][:, None], (A_Q, LANES)),
                       jnp.broadcast_to(g_b[l][:, None], (B_Q, LANES)))
        x = _outproj_ln(y, x, gt1, w_o[l].astype(bf), ln1_g[l].reshape(1, d),
                        ln1_b[l].reshape(1, d), alpha=alpha, tm=512)
        x = _ffn_ln(x, sc2, sh2, gt2, w_gate[l].astype(bf), w_up[l].astype(bf),
                    w_down[l].astype(bf), ln2_g[l].reshape(1, d), ln2_b[l].reshape(1, d),
                    alpha=alpha, tm=512)
    return x
```

```python
import functools

import jax
import jax.numpy as jnp
import numpy as np
from jax import lax
from jax.experimental import pallas as pl
from jax.experimental.pallas import tpu as pltpu

HEAD_DIM = 64
HALF = HEAD_DIM // 2
A_HEADS = 8
A_KV_HEADS = 2
B_HEADS = 8
B_KV_HEADS = 2
IDX_HEADS = 8
IDX_DIM = 64
WINDOW = 128
BLOCK = 128
TOPK_MAX = 256
ROPE_THETA = 10000.0
LN_EPS = 1e-5
RMS_EPS = 1e-6

A_Q = A_HEADS * HEAD_DIM
A_KV = A_KV_HEADS * HEAD_DIM
B_Q = B_HEADS * HEAD_DIM
B_KV = B_KV_HEADS * HEAD_DIM
I_Q = IDX_HEADS * IDX_DIM
GROUP = A_HEADS // A_KV_HEADS

LANES = 128
KEY_CHUNK = 256
PIECE = 256
ONES_ROWS = 16
O_ROWS = HEAD_DIM + 8
NEG_BIG = -1e30
INT_MIN = -(2 ** 31)
VMEM_LIMIT = 56 * 1024 * 1024

Q_ROWS = A_Q + B_Q + I_Q
K_ROWS = 3 * LANES
V_ROWS = A_KV + B_KV
W_ROWS = 8
P_ROWS = Q_ROWS + K_ROWS + V_ROWS + 2 * W_ROWS


def _silu(x):
    return x * (1.0 / (1.0 + jnp.exp(-x)))


def _layer_norm(z, g, b):
    mu = jnp.mean(z, axis=-1, keepdims=True)
    zc = z - mu
    var = jnp.mean(zc * zc, axis=-1, keepdims=True)
    return zc * lax.rsqrt(var + LN_EPS) * g + b


def _mod_kernel(c_ref, w_ref, b_ref, o_ref):
    c_act = _silu(c_ref[...]).astype(jnp.bfloat16)
    o_ref[...] = jnp.dot(c_act, w_ref[...].astype(jnp.bfloat16),
                         preferred_element_type=jnp.float32) + b_ref[...]


def _adaln_mod(c, w_ada, b_ada):
    depth, d, d6 = w_ada.shape
    bsz = c.shape[0]
    tn = 1024
    return pl.pallas_call(
        _mod_kernel,
        out_shape=jax.ShapeDtypeStruct((depth, bsz, d6), jnp.float32),
        grid=(depth, d6 // tn),
        in_specs=[
            pl.BlockSpec((bsz, d), lambda l, j: (0, 0)),
            pl.BlockSpec((None, d, tn), lambda l, j: (l, 0, j)),
            pl.BlockSpec((None, 1, tn), lambda l, j: (l, 0, j)),
        ],
        out_specs=pl.BlockSpec((None, bsz, tn), lambda l, j: (l, 0, j)),
        compiler_params=pltpu.CompilerParams(
            dimension_semantics=("arbitrary", "arbitrary"),
            vmem_limit_bytes=VMEM_LIMIT),
        name="adaln_mod",
    )(c, w_ada, b_ada.reshape(depth, 1, d6))


def _rope_head(x, r, cos, sin, scale=None):
    x1, x2 = x[r:r + HALF], x[r + HALF:r + HEAD_DIM]
    o1 = x1 * cos - x2 * sin
    o2 = x2 * cos + x1 * sin
    if scale is not None:
        o1, o2 = o1 * scale, o2 * scale
    return o1, o2


def _inproj_kernel(x_ref, sc_ref, sh_ref, pos_ref, inv_ref, w_ref,
                   qT_ref, kN_ref, vT_ref, wT_ref, *, w_scale):
    h = x_ref[...] * (1.0 + sc_ref[...]) + sh_ref[...]
    pT = lax.dot_general(w_ref[...], h.astype(jnp.bfloat16),
                         (((1,), (1,)), ((), ())),
                         preferred_element_type=jnp.float32)
    ang = inv_ref[...] * pos_ref[...].astype(jnp.float32)
    cos, sin = jnp.cos(ang), jnp.sin(ang)

    nblk = vT_ref.shape[0]
    for hd in range(Q_ROWS // HEAD_DIM):
        r = hd * HEAD_DIM
        scale = HEAD_DIM ** -0.5 if r < A_Q + B_Q else None
        o1, o2 = _rope_head(pT, r, cos, sin, scale)
        o1, o2 = o1.astype(qT_ref.dtype), o2.astype(qT_ref.dtype)
        for j in range(nblk):
            qT_ref[j, r:r + HALF, :] = o1[:, j * LANES:(j + 1) * LANES]
            qT_ref[j, r + HALF:r + HEAD_DIM, :] = o2[:, j * LANES:(j + 1) * LANES]

    k_rows = []
    for hd in range(K_ROWS // HEAD_DIM):
        k_rows.extend(_rope_head(pT, Q_ROWS + hd * HEAD_DIM, cos, sin))
    kN_ref[...] = jnp.concatenate(k_rows, axis=0).T.astype(kN_ref.dtype)

    v0 = Q_ROWS + K_ROWS
    v = pT[v0:v0 + V_ROWS].astype(vT_ref.dtype)
    w0 = v0 + V_ROWS
    w = pT[w0:w0 + W_ROWS] * w_scale
    for j in range(nblk):
        vT_ref[j] = v[:, j * LANES:(j + 1) * LANES]
        wT_ref[j] = w[:, j * LANES:(j + 1) * LANES]


def _inproj(x, sc, sh, pos3, inv, w_t, *, tm):
    bsz, s, d = x.shape
    kern = functools.partial(_inproj_kernel, w_scale=IDX_HEADS ** -0.5 * IDX_DIM ** -0.5)
    return pl.pallas_call(
        kern,
        out_shape=(
            jax.ShapeDtypeStruct((bsz, s // LANES, Q_ROWS, LANES), jnp.bfloat16),
            jax.ShapeDtypeStruct((bsz, s, K_ROWS), jnp.bfloat16),
            jax.ShapeDtypeStruct((bsz, s // LANES, V_ROWS, LANES), jnp.bfloat16),
            jax.ShapeDtypeStruct((bsz, s // LANES, W_ROWS, LANES), jnp.float32),
        ),
        grid=(bsz, s // tm),
        in_specs=[
            pl.BlockSpec((None, tm, d), lambda b, i: (b, i, 0)),
            pl.BlockSpec((None, 1, d), lambda b, i: (b, 0, 0)),
            pl.BlockSpec((None, 1, d), lambda b, i: (b, 0, 0)),
            pl.BlockSpec((None, 1, tm), lambda b, i: (b, 0, i)),
            pl.BlockSpec((HALF, 1), lambda b, i: (0, 0)),
            pl.BlockSpec((P_ROWS, d), lambda b, i: (0, 0)),
        ],
        out_specs=(
            pl.BlockSpec((None, tm // LANES, Q_ROWS, LANES), lambda b, i: (b, i, 0, 0)),
            pl.BlockSpec((None, tm, K_ROWS), lambda b, i: (b, i, 0)),
            pl.BlockSpec((None, tm // LANES, V_ROWS, LANES), lambda b, i: (b, i, 0, 0)),
            pl.BlockSpec((None, tm // LANES, W_ROWS, LANES), lambda b, i: (b, i, 0, 0)),
        ),
        compiler_params=pltpu.CompilerParams(
            dimension_semantics=("arbitrary", "arbitrary"),
            vmem_limit_bytes=VMEM_LIMIT),
        name="in_proj",
    )(x, sc, sh, pos3, inv, w_t)


def _ordered_to_f32(c):
    bits = jnp.where(c >= 0, c, c ^ jnp.int32(0x7FFFFFFF))
    return lax.bitcast_convert_type(bits, jnp.float32)


def _head_rhs(qT, row0):
    zeros = jnp.zeros((HEAD_DIM, LANES), qT.dtype)
    cols = []
    for h in range(A_HEADS):
        blk = qT[row0 + h * HEAD_DIM:row0 + (h + 1) * HEAD_DIM, :]
        g = h // GROUP
        cols.append(jnp.concatenate([blk, zeros] if g == 0 else [zeros, blk], axis=0))
    return jnp.concatenate(cols, axis=1)


def _attn_kernel(sink_ref, qT_ref, kN_ref, vT_ref, wT_ref, ga_ref, gb_ref,
                 y_ref, score_ref, thr_ref, jt_ref, m_ref, o_ref, yT_ref,
                 sa_ref, sb_ref, pa_ref, pb_ref, aa_ref, ab_ref, rhs_ref, ca_ref, cb_ref,
                 *, seq, topk):
    n = pl.program_id(1)
    q0 = n * BLOCK
    nk = (n + 2) // 2
    qidx = q0 + lax.broadcasted_iota(jnp.int32, (KEY_CHUNK, LANES), 1)
    row_iota = lax.broadcasted_iota(jnp.int32, (KEY_CHUNK, LANES), 0)
    nh = B_HEADS

    for h in range(IDX_HEADS):
        rhs_ref[0:IDX_DIM, h * LANES:(h + 1) * LANES] = (
            qT_ref[A_Q + B_Q + h * IDX_DIM:A_Q + B_Q + (h + 1) * IDX_DIM, :])

    npair = (nk + 1) // 2
    last = nk - 1

    def rows_of(c):
        return pl.ds(pl.multiple_of(c * KEY_CHUNK, KEY_CHUNK), KEY_CHUNK)

    n_piece = B_HEADS * LANES // PIECE
    heads_per_piece = PIECE // LANES

    def rel_piece(ki, j, s_ref):
        cs = slice(j * PIECE, (j + 1) * PIECE)
        s_ref[:, cs] = jnp.dot(ki, rhs_ref[0:IDX_DIM, cs], preferred_element_type=jnp.float32)

    def score_piece(s_ref, j, sc):
        for h in range(j * heads_per_piece, (j + 1) * heads_per_piece):
            sc = sc + jnp.maximum(s_ref[:, h * LANES:(h + 1) * LANES], 0.0) * wT_ref[h:h + 1, :]
        return sc

    def score_half(c_next, s_next, c_cur, s_cur):
        ki = kN_ref[rows_of(c_next), 2 * LANES:2 * LANES + IDX_DIM]
        sc = jnp.zeros((KEY_CHUNK, LANES), jnp.float32)
        for j in range(n_piece):
            rel_piece(ki, j, s_next)
            sc = score_piece(s_cur, j, sc)
        kidx = c_cur * KEY_CHUNK + row_iota
        score_ref[rows_of(c_cur), :] = jnp.where(kidx <= qidx, sc, -jnp.inf)

    ki0 = kN_ref[0:KEY_CHUNK, 2 * LANES:2 * LANES + IDX_DIM]
    for j in range(n_piece):
        rel_piece(ki0, j, sa_ref)

    def score_body(i, carry):
        c0 = 2 * i
        c1 = jnp.minimum(c0 + 1, last)
        score_half(c1, sb_ref, c0, sa_ref)
        score_half(jnp.minimum(c0 + 2, last), sa_ref, c1, sb_ref)
        return carry

    lax.fori_loop(0, npair, score_body, 0)

    def fold_rows(x):
        return x.reshape(KEY_CHUNK // 8, 8, LANES).sum(axis=0)

    def count(pred_fn):
        def body(c, acc):
            r0 = pl.multiple_of(c * KEY_CHUNK, KEY_CHUNK)
            blk = score_ref[pl.ds(r0, KEY_CHUNK), :]
            return acc + fold_rows(pred_fn(blk, r0).astype(jnp.int32))
        acc = lax.fori_loop(0, nk, body, jnp.zeros((8, LANES), jnp.int32))
        return acc.sum(axis=0, keepdims=True)

    def masked_min(pred_fn):
        def body(c, acc):
            r0 = pl.multiple_of(c * KEY_CHUNK, KEY_CHUNK)
            blk = score_ref[pl.ds(r0, KEY_CHUNK), :]
            v = jnp.where(pred_fn(blk, r0), blk, jnp.inf)
            return jnp.minimum(acc, v.reshape(KEY_CHUNK // 8, 8, LANES).min(axis=0))
        acc = lax.fori_loop(0, nk, body, jnp.full((8, LANES), jnp.inf, jnp.float32))
        return acc.min(axis=0, keepdims=True)

    def search_static(nkk):
        def count_ge(cf):
            acc = jnp.zeros((8, LANES), jnp.int32)
            for c in range(nkk):
                blk = score_ref[c * KEY_CHUNK:(c + 1) * KEY_CHUNK, :]
                acc = acc + fold_rows((blk >= cf).astype(jnp.int32))
            return acc.sum(axis=0, keepdims=True)

        def run():
            c0 = count_ge(0.0)
            nonneg = c0 >= topk
            st0 = (jnp.where(nonneg, jnp.int32(0), jnp.int32(INT_MIN)),
                   jnp.where(nonneg, c0, jnp.int32(nkk * KEY_CHUNK)))

            def bit_body(i, st):
                pfx, c_pfx = st
                cand = pfx + (jnp.int32(1) << (30 - i))
                cnt = count_ge(_ordered_to_f32(cand))
                take = cnt >= topk
                return jnp.where(take, cand, pfx), jnp.where(take, cnt, c_pfx)

            pfx, c_pfx = lax.fori_loop(0, 31, bit_body, st0)
            return _ordered_to_f32(pfx), c_pfx

        return run

    max_nk = (seq // BLOCK + 1) // 2
    lo, c_lo = lax.cond(
        n >= 2,
        lambda: lax.switch(nk - 2, [search_static(j) for j in range(2, max_nk + 1)]),
        lambda: (jnp.full((1, LANES), -jnp.inf, jnp.float32), jnp.zeros((1, LANES), jnp.int32)))
    thr_ref[...] = lo
    jt_ref[...] = jnp.full((1, LANES), jnp.where(n >= 2, seq, -1), jnp.int32)
    unresolved = jnp.max(jnp.where(c_lo > topk, 1, 0), axis=1, keepdims=True)[0, 0] > 0

    qT_all = qT_ref[...]
    eye = (lax.broadcasted_iota(jnp.int32, (LANES, LANES), 0)
           == lax.broadcasted_iota(jnp.int32, (LANES, LANES), 1)).astype(jnp.bfloat16)
    eye_t = jnp.concatenate([eye] * B_HEADS, axis=1)
    rhs_ref[...] = jnp.concatenate([_head_rhs(qT_all, A_Q), eye_t], axis=0)
    ones_rows = jnp.ones((ONES_ROWS, KEY_CHUNK), jnp.bfloat16)
    m_ref[...] = jnp.full(m_ref.shape, NEG_BIG, jnp.float32)
    o_ref[...] = jnp.zeros(o_ref.shape, jnp.float32)
    pb_ref[...] = jnp.zeros(pb_ref.shape, pb_ref.dtype)
    ab_ref[...] = jnp.ones(ab_ref.shape, ab_ref.dtype)

    @pl.when(unresolved)
    def _resolve_ties():
        def counts(bv):
            return (count(lambda blk, r0: blk > bv), count(lambda blk, r0: blk >= bv))

        b0 = masked_min(lambda blk, r0: blk >= lo)
        cgt0, cge0 = counts(b0)

        def w_cond(st):
            return jnp.max(st[1], axis=1, keepdims=True)[0, 0] >= topk

        def w_body(st):
            bv, cgt, _ = st
            nxt = masked_min(lambda blk, r0: blk > bv)
            bn = jnp.where(cgt >= topk, nxt, bv)
            cgt2, cge2 = counts(bn)
            return bn, cgt2, cge2

        bv, cgt, _ = lax.while_loop(w_cond, w_body, (b0, cgt0, cge0))
        thr_ref[...] = bv
        need = topk - cgt
        nbits = max(1, int(np.ceil(np.log2(seq))))

        def jbody(i, pj):
            step = jnp.int32(1) << (nbits - 1 - i)
            cand = pj + step - 1
            f = count(lambda blk, r0: (blk == bv) & ((r0 + row_iota) <= cand))
            return jnp.where(f >= need, pj, pj + step)

        jt_ref[...] = lax.fori_loop(0, nbits, jbody, jnp.zeros((1, LANES), jnp.int32))

    thr = thr_ref[...]
    jt = jt_ref[...]

    def softmax_heads(s_all, m_of):
        p_heads, maxes = [], []
        for h in range(nh):
            sl = slice(h * LANES, (h + 1) * LANES)
            s_h = s_all[:, sl]
            m_old = m_of(h, sl)
            m_new = jnp.maximum(m_old, jnp.max(s_h, axis=0, keepdims=True))
            p_heads.append(jnp.exp(s_h - m_new).astype(jnp.bfloat16))
            maxes.append((m_old, m_new))
        return p_heads, maxes

    def qk_lhs(c, valid):
        sc = score_ref[rows_of(c), :]
        kidx = c * KEY_CHUNK + row_iota
        sel = (sc > thr) | ((sc == thr) & (kidx <= jt))
        keep = jnp.where(valid, 0.0, NEG_BIG)
        bias = jnp.where(sel, keep, NEG_BIG).astype(jnp.bfloat16)
        return jnp.concatenate([kN_ref[rows_of(c), LANES:2 * LANES], bias], axis=1)

    def qk_piece(lhs, j, s_ref, cm_ref):
        cs = slice(j * PIECE, (j + 1) * PIECE)
        s = jnp.dot(lhs, rhs_ref[:, cs], preferred_element_type=jnp.float32)
        s_ref[:, cs] = s
        for hh in range(heads_per_piece):
            sl = slice(j * PIECE + hh * LANES, j * PIECE + (hh + 1) * LANES)
            cm_ref[:, sl] = jnp.max(s[:, hh * LANES:(hh + 1) * LANES], axis=0, keepdims=True)

    def softmax_piece(j, s_ref, cm_ref, p_ref, a_ref):
        for h in range(j * heads_per_piece, (j + 1) * heads_per_piece):
            sl = slice(h * LANES, (h + 1) * LANES)
            m_old = m_ref[:, sl]
            m_new = jnp.maximum(m_old, cm_ref[:, sl])
            p_ref[:, sl] = jnp.exp(s_ref[:, sl] - m_new).astype(p_ref.dtype)
            m_ref[:, sl] = m_new
            a_ref[:, sl] = jnp.exp(m_old - m_new)

    def pv_group(g, p_ref, a_ref, c):
        rows = slice(A_KV + g * HEAD_DIM, A_KV + (g + 1) * HEAD_DIM)
        vg = jnp.concatenate([vT_ref[2 * c, rows, :], vT_ref[2 * c + 1, rows, :]], axis=1)
        vg = jnp.concatenate([vg, ones_rows], axis=0)
        csl = slice(g * GROUP * LANES, (g + 1) * GROUP * LANES)
        pv = jnp.dot(vg, p_ref[:, csl], preferred_element_type=jnp.float32)
        o_ref[:, csl] = o_ref[:, csl] * a_ref[:, csl] + pv[0:O_ROWS]

    def dsa_half(c_qk, valid, s_qk, cm_qk, s_sm, cm_sm, p_sm, a_sm, p_pv, a_pv, c_pv):
        lhs = qk_lhs(c_qk, valid)
        pieces_per_group = n_piece // B_KV_HEADS
        for j in range(n_piece):
            qk_piece(lhs, j, s_qk, cm_qk)
            softmax_piece(j, s_sm, cm_sm, p_sm, a_sm)
            if (j + 1) % pieces_per_group == 0:
                pv_group(j // pieces_per_group, p_pv, a_pv, c_pv)

    lhs0 = qk_lhs(0, True)
    for j in range(n_piece):
        qk_piece(lhs0, j, sa_ref, ca_ref)

    rhs_a = jnp.concatenate([_head_rhs(qT_all, 0), eye_t], axis=0)
    wblk = jnp.maximum(n - 1, 0)
    w0 = pl.multiple_of(wblk * BLOCK, BLOCK)
    delta = qidx - (w0 + row_iota)
    band = (delta >= 0) & (delta < WINDOW)
    bias = jnp.where(band, 0.0, NEG_BIG).astype(jnp.bfloat16)
    lhs = jnp.concatenate([kN_ref[pl.ds(w0, 2 * BLOCK), 0:LANES], bias], axis=1)
    s_all = jnp.dot(lhs, rhs_a, preferred_element_type=jnp.float32)
    p_heads, ms = softmax_heads(
        s_all, lambda h, sl: jnp.full((1, LANES), sink_ref[h], jnp.float32))
    for g in range(A_KV_HEADS):
        pT = jnp.concatenate(p_heads[g * GROUP:(g + 1) * GROUP], axis=1)
        rows = slice(g * HEAD_DIM, (g + 1) * HEAD_DIM)
        vg = jnp.concatenate([vT_ref[wblk, rows, :], vT_ref[wblk + 1, rows, :]], axis=1)
        vg = jnp.concatenate([vg, ones_rows], axis=0)
        og = jnp.dot(vg, pT, preferred_element_type=jnp.float32)
        for j in range(GROUP):
            h = g * GROUP + j
            sink, m = ms[h]
            csl = slice(j * LANES, (j + 1) * LANES)
            l = og[HEAD_DIM:HEAD_DIM + 1, csl] + jnp.exp(sink - m)
            yT_ref[h * HEAD_DIM:(h + 1) * HEAD_DIM, :] = og[0:HEAD_DIM, csl] * (1.0 / l)

    def dsa_body(i, carry):
        c0 = 2 * i
        c1 = jnp.minimum(c0 + 1, last)
        dsa_half(c1, c0 + 1 <= last, sb_ref, cb_ref, sa_ref, ca_ref, pa_ref, aa_ref,
                 pb_ref, ab_ref, jnp.maximum(c0 - 1, 0))
        dsa_half(jnp.minimum(c0 + 2, last), c0 + 2 <= last, sa_ref, ca_ref, sb_ref, cb_ref,
                 pb_ref, ab_ref, pa_ref, aa_ref, c0)
        return carry

    lax.fori_loop(0, npair, dsa_body, 0)
    for g in range(B_KV_HEADS):
        pv_group(g, pb_ref, ab_ref, jnp.minimum(2 * npair - 1, last))

    ob = o_ref[0:HEAD_DIM, :] * (1.0 / o_ref[HEAD_DIM:HEAD_DIM + 1, :])
    for h in range(nh):
        yT_ref[A_Q + h * HEAD_DIM:A_Q + (h + 1) * HEAD_DIM, :] = ob[:, h * LANES:(h + 1) * LANES]

    normed = []
    for r0, width, g_ref in ((0, A_Q, ga_ref), (A_Q, B_Q, gb_ref)):
        blk = yT_ref[r0:r0 + width, :]
        ms = jnp.sum(blk * blk, axis=0, keepdims=True) * (1.0 / width)
        normed.append((blk * lax.rsqrt(ms + RMS_EPS) * g_ref[...]).astype(jnp.bfloat16))
    y_t = jnp.concatenate(normed, axis=0)
    y = lax.dot_general(eye, y_t, (((1,), (1,)), ((), ())), preferred_element_type=jnp.float32)
    y_ref[...] = y.astype(y_ref.dtype)


def _attention(sinks, qT, kN, vT, wT, g_a, g_b):
    bsz, s, _ = kN.shape
    nb = s // BLOCK
    topk = min(TOPK_MAX, s // 4)
    kern = functools.partial(_attn_kernel, seq=s, topk=topk)
    return pl.pallas_call(
        kern,
        out_shape=jax.ShapeDtypeStruct((bsz, s, A_Q + B_Q), jnp.bfloat16),
        grid=(bsz, nb),
        in_specs=[
            pl.BlockSpec(memory_space=pltpu.SMEM),
            pl.BlockSpec((None, None, Q_ROWS, BLOCK), lambda b, n: (b, n, 0, 0)),
            pl.BlockSpec((None, s, K_ROWS), lambda b, n: (b, 0, 0)),
            pl.BlockSpec((None, nb, V_ROWS, LANES), lambda b, n: (b, 0, 0, 0)),
            pl.BlockSpec((None, None, W_ROWS, BLOCK), lambda b, n: (b, n, 0, 0)),
            pl.BlockSpec((A_Q, LANES), lambda b, n: (0, 0)),
            pl.BlockSpec((B_Q, LANES), lambda b, n: (0, 0)),
        ],
        out_specs=pl.BlockSpec((None, BLOCK, A_Q + B_Q), lambda b, n: (b, n, 0)),
        scratch_shapes=[
            pltpu.VMEM((s, LANES), jnp.float32),
            pltpu.VMEM((1, LANES), jnp.float32),
            pltpu.VMEM((1, LANES), jnp.int32),
            pltpu.VMEM((1, B_HEADS * LANES), jnp.float32),
            pltpu.VMEM((O_ROWS, B_HEADS * LANES), jnp.float32),
            pltpu.VMEM((A_Q + B_Q, LANES), jnp.float32),
            pltpu.VMEM((KEY_CHUNK, B_HEADS * LANES), jnp.float32),
            pltpu.VMEM((KEY_CHUNK, B_HEADS * LANES), jnp.float32),
            pltpu.VMEM((KEY_CHUNK, B_HEADS * LANES), jnp.bfloat16),
            pltpu.VMEM((KEY_CHUNK, B_HEADS * LANES), jnp.bfloat16),
            pltpu.VMEM((1, B_HEADS * LANES), jnp.float32),
            pltpu.VMEM((1, B_HEADS * LANES), jnp.float32),
            pltpu.VMEM((2 * LANES, B_HEADS * LANES), jnp.bfloat16),
            pltpu.VMEM((1, B_HEADS * LANES), jnp.float32),
            pltpu.VMEM((1, B_HEADS * LANES), jnp.float32),
        ],
        compiler_params=pltpu.CompilerParams(
            dimension_semantics=("arbitrary", "arbitrary"),
            vmem_limit_bytes=VMEM_LIMIT),
        name="attention",
    )(sinks, qT, kN, vT, wT, g_a, g_b)


def _post_kernel(y_ref, x_ref, gt1_ref, wo_ref, g1_ref, b1_ref, sc_ref, sh_ref, gt2_ref,
                 wg_ref, wu_ref, wd_ref, g2_ref, b2_ref, o_ref, *, alpha):
    proj = jnp.dot(y_ref[...], wo_ref[...], preferred_element_type=jnp.float32)
    x1 = _layer_norm(alpha * x_ref[...] + (1.0 + gt1_ref[...]) * proj, g1_ref[...], b1_ref[...])
    h = (x1 * (1.0 + sc_ref[...]) + sh_ref[...]).astype(jnp.bfloat16)
    gate = jnp.dot(h, wg_ref[...], preferred_element_type=jnp.float32)
    up = jnp.dot(h, wu_ref[...], preferred_element_type=jnp.float32)
    act = (_silu(gate) * up).astype(jnp.bfloat16)
    y2 = jnp.dot(act, wd_ref[...], preferred_element_type=jnp.float32)
    o_ref[...] = _layer_norm(alpha * x1 + (1.0 + gt2_ref[...]) * y2, g2_ref[...], b2_ref[...])


def _post_attention(y, x, gt1, w_o, g1, b1, sc2, sh2, gt2, w_gate, w_up, w_down, g2, b2, *, alpha, tm):
    bsz, s, d = x.shape
    dy = y.shape[-1]
    dff = w_gate.shape[-1]
    resident = pl.Buffered(1)
    tile = lambda width: pl.BlockSpec((None, tm, width), lambda b_, i: (b_, i, 0))
    per_batch = pl.BlockSpec((None, 1, d), lambda b_, i: (b_, 0, 0))
    row = pl.BlockSpec((1, d), lambda b_, i: (0, 0))
    weight = lambda r, c: pl.BlockSpec((r, c), lambda b_, i: (0, 0), pipeline_mode=resident)
    return pl.pallas_call(
        functools.partial(_post_kernel, alpha=alpha),
        out_shape=jax.ShapeDtypeStruct((bsz, s, d), jnp.float32),
        grid=(bsz, s // tm),
        in_specs=[tile(dy), tile(d), per_batch, weight(dy, d), row, row,
                  per_batch, per_batch, per_batch,
                  weight(d, dff), weight(d, dff), weight(dff, d), row, row],
        out_specs=tile(d),
        compiler_params=pltpu.CompilerParams(
            dimension_semantics=("arbitrary", "arbitrary"),
            vmem_limit_bytes=VMEM_LIMIT),
        name="out_proj_ffn",
    )(y, x, gt1, w_o, g1, b1, sc2, sh2, gt2, w_gate, w_up, w_down, g2, b2)


def _regroup_w_in(w_in_l):
    d = w_in_l.shape[0]
    splits = np.cumsum([A_Q, A_KV, A_KV, B_Q, B_KV, B_KV, I_Q, IDX_DIM, IDX_HEADS])
    qa, ka, va, qb, kb, vb, qi, ki, wi = jnp.split(w_in_l, splits[:-1], axis=1)
    zk = jnp.zeros((d, K_ROWS - A_KV - B_KV - IDX_DIM), w_in_l.dtype)
    zw = jnp.zeros((d, 2 * W_ROWS - IDX_HEADS), w_in_l.dtype)
    cols = jnp.concatenate([qa, qb, qi, ka, kb, ki, zk, va, vb, wi, zw], axis=1)
    return cols.T.astype(jnp.bfloat16)


def kernel(x, c, positions, w_ada, b_ada, w_in, attn_sinks, g_a, g_b, w_o,
           ln1_g, ln1_b, w_gate, w_up, w_down, ln2_g, ln2_b):
    bsz, s, d = x.shape
    depth = w_ada.shape[0]
    alpha = (2.0 * depth) ** 0.25
    bf = jnp.bfloat16

    mod = _adaln_mod(c, w_ada, b_ada)
    inv = (ROPE_THETA ** (-jnp.arange(HALF, dtype=jnp.float32) / HALF)).reshape(HALF, 1)
    pos3 = positions.reshape(bsz, 1, s)

    for l in range(depth):
        m = mod[l].reshape(bsz, 1, 6 * d)
        sh1, sc1, gt1, sh2, sc2, gt2 = [m[:, :, i * d:(i + 1) * d] for i in range(6)]
        qT, kN, vT, wT = _inproj(x, sc1, sh1, pos3, inv, _regroup_w_in(w_in[l]), tm=512)
        y = _attention(attn_sinks[l], qT, kN, vT, wT,
                       jnp.broadcast_to(g_a[l][:, None], (A_Q, LANES)),
                       jnp.broadcast_to(g_b[l][:, None], (B_Q, LANES)))
        x = _post_attention(y, x, gt1, w_o[l].astype(bf), ln1_g[l].reshape(1, d), ln1_b[l].reshape(1, d),
                            sc2, sh2, gt2, w_gate[l].astype(bf), w_up[l].astype(bf), w_down[l].astype(bf),
                            ln2_g[l].reshape(1, d), ln2_b[l].reshape(1, d), alpha=alpha, tm=512)
    return x
```

```python
import functools

import jax
import jax.numpy as jnp
import numpy as np
from jax import lax
from jax.experimental import pallas as pl
from jax.experimental.pallas import tpu as pltpu

HEAD_DIM = 64
HALF = HEAD_DIM // 2
A_HEADS = 8
A_KV_HEADS = 2
B_HEADS = 8
B_KV_HEADS = 2
IDX_HEADS = 8
IDX_DIM = 64
WINDOW = 128
BLOCK = 128
TOPK_MAX = 256
ROPE_THETA = 10000.0
LN_EPS = 1e-5
RMS_EPS = 1e-6

A_Q = A_HEADS * HEAD_DIM
A_KV = A_KV_HEADS * HEAD_DIM
B_Q = B_HEADS * HEAD_DIM
B_KV = B_KV_HEADS * HEAD_DIM
I_Q = IDX_HEADS * IDX_DIM
GROUP = A_HEADS // A_KV_HEADS

LANES = 128
KEY_CHUNK = 256
PIECE = 256
ONES_ROWS = 16
O_ROWS = HEAD_DIM + 8
NEG_BIG = -1e30
INT_MIN = -(2 ** 31)
VMEM_LIMIT = 56 * 1024 * 1024

Q_ROWS = A_Q + B_Q + I_Q
K_ROWS = 3 * LANES
V_ROWS = A_KV + B_KV
W_ROWS = 8
P_ROWS = Q_ROWS + K_ROWS + V_ROWS + 2 * W_ROWS


def _silu(x):
    return x * (1.0 / (1.0 + jnp.exp(-x)))


def _layer_norm(z, g, b):
    mu = jnp.mean(z, axis=-1, keepdims=True)
    zc = z - mu
    var = jnp.mean(zc * zc, axis=-1, keepdims=True)
    return zc * lax.rsqrt(var + LN_EPS) * g + b


def _mod_kernel(c_ref, w_ref, b_ref, o_ref):
    c_act = _silu(c_ref[...]).astype(jnp.bfloat16)
    o_ref[...] = jnp.dot(c_act, w_ref[...].astype(jnp.bfloat16),
                         preferred_element_type=jnp.float32) + b_ref[...]


def _adaln_mod(c, w_ada, b_ada):
    depth, d, d6 = w_ada.shape
    bsz = c.shape[0]
    tn = 1024
    return pl.pallas_call(
        _mod_kernel,
        out_shape=jax.ShapeDtypeStruct((depth, bsz, d6), jnp.float32),
        grid=(depth, d6 // tn),
        in_specs=[
            pl.BlockSpec((bsz, d), lambda l, j: (0, 0)),
            pl.BlockSpec((None, d, tn), lambda l, j: (l, 0, j)),
            pl.BlockSpec((None, 1, tn), lambda l, j: (l, 0, j)),
        ],
        out_specs=pl.BlockSpec((None, bsz, tn), lambda l, j: (l, 0, j)),
        compiler_params=pltpu.CompilerParams(
            dimension_semantics=("arbitrary", "arbitrary"),
            vmem_limit_bytes=VMEM_LIMIT),
        name="adaln_mod",
    )(c, w_ada, b_ada.reshape(depth, 1, d6))


def _rope_head(x, r, cos, sin, scale=None):
    x1, x2 = x[r:r + HALF], x[r + HALF:r + HEAD_DIM]
    o1 = x1 * cos - x2 * sin
    o2 = x2 * cos + x1 * sin
    if scale is not None:
        o1, o2 = o1 * scale, o2 * scale
    return o1, o2


def _inproj_kernel(x_ref, sc_ref, sh_ref, pos_ref, inv_ref, w_ref,
                   qT_ref, kN_ref, vT_ref, wT_ref, *, w_scale):
    h = x_ref[...] * (1.0 + sc_ref[...]) + sh_ref[...]
    pT = lax.dot_general(w_ref[...], h.astype(jnp.bfloat16),
                         (((1,), (1,)), ((), ())),
                         preferred_element_type=jnp.float32)
    ang = inv_ref[...] * pos_ref[...].astype(jnp.float32)
    cos, sin = jnp.cos(ang), jnp.sin(ang)

    nblk = vT_ref.shape[0]
    for hd in range(Q_ROWS // HEAD_DIM):
        r = hd * HEAD_DIM
        scale = HEAD_DIM ** -0.5 if r < A_Q + B_Q else None
        o1, o2 = _rope_head(pT, r, cos, sin, scale)
        o1, o2 = o1.astype(qT_ref.dtype), o2.astype(qT_ref.dtype)
        for j in range(nblk):
            qT_ref[j, r:r + HALF, :] = o1[:, j * LANES:(j + 1) * LANES]
            qT_ref[j, r + HALF:r + HEAD_DIM, :] = o2[:, j * LANES:(j + 1) * LANES]

    k_rows = []
    for hd in range(K_ROWS // HEAD_DIM):
        k_rows.extend(_rope_head(pT, Q_ROWS + hd * HEAD_DIM, cos, sin))
    kN_ref[...] = jnp.concatenate(k_rows, axis=0).T.astype(kN_ref.dtype)

    v0 = Q_ROWS + K_ROWS
    v = pT[v0:v0 + V_ROWS].astype(vT_ref.dtype)
    w0 = v0 + V_ROWS
    w = pT[w0:w0 + W_ROWS] * w_scale
    for j in range(nblk):
        vT_ref[j] = v[:, j * LANES:(j + 1) * LANES]
        wT_ref[j] = w[:, j * LANES:(j + 1) * LANES]


def _inproj(x, sc, sh, pos3, inv, w_t, *, tm):
    bsz, s, d = x.shape
    kern = functools.partial(_inproj_kernel, w_scale=IDX_HEADS ** -0.5 * IDX_DIM ** -0.5)
    return pl.pallas_call(
        kern,
        out_shape=(
            jax.ShapeDtypeStruct((bsz, s // LANES, Q_ROWS, LANES), jnp.bfloat16),
            jax.ShapeDtypeStruct((bsz, s, K_ROWS), jnp.bfloat16),
            jax.ShapeDtypeStruct((bsz, s // LANES, V_ROWS, LANES), jnp.bfloat16),
            jax.ShapeDtypeStruct((bsz, s // LANES, W_ROWS, LANES), jnp.float32),
        ),
        grid=(bsz, s // tm),
        in_specs=[
            pl.BlockSpec((None, tm, d), lambda b, i: (b, i, 0)),
            pl.BlockSpec((None, 1, d), lambda b, i: (b, 0, 0)),
            pl.BlockSpec((None, 1, d), lambda b, i: (b, 0, 0)),
            pl.BlockSpec((None, 1, tm), lambda b, i: (b, 0, i)),
            pl.BlockSpec((HALF, 1), lambda b, i: (0, 0)),
            pl.BlockSpec((P_ROWS, d), lambda b, i: (0, 0)),
        ],
        out_specs=(
            pl.BlockSpec((None, tm // LANES, Q_ROWS, LANES), lambda b, i: (b, i, 0, 0)),
            pl.BlockSpec((None, tm, K_ROWS), lambda b, i: (b, i, 0)),
            pl.BlockSpec((None, tm // LANES, V_ROWS, LANES), lambda b, i: (b, i, 0, 0)),
            pl.BlockSpec((None, tm // LANES, W_ROWS, LANES), lambda b, i: (b, i, 0, 0)),
        ),
        compiler_params=pltpu.CompilerParams(
            dimension_semantics=("arbitrary", "arbitrary"),
            vmem_limit_bytes=VMEM_LIMIT),
        name="in_proj",
    )(x, sc, sh, pos3, inv, w_t)


def _ordered_to_f32(c):
    bits = jnp.where(c >= 0, c, c ^ jnp.int32(0x7FFFFFFF))
    return lax.bitcast_convert_type(bits, jnp.float32)


def _head_rhs(qT, row0):
    zeros = jnp.zeros((HEAD_DIM, LANES), qT.dtype)
    cols = []
    for h in range(A_HEADS):
        blk = qT[row0 + h * HEAD_DIM:row0 + (h + 1) * HEAD_DIM, :]
        g = h // GROUP
        cols.append(jnp.concatenate([blk, zeros] if g == 0 else [zeros, blk], axis=0))
    return jnp.concatenate(cols, axis=1)


def _attn_kernel(sink_ref, qT_ref, kN_ref, vT_ref, wT_ref, ga_ref, gb_ref,
                 y_ref, score_ref, thr_ref, jt_ref, m_ref, o_ref, yT_ref,
                 sa_ref, sb_ref, p_ref, rhs_ref, sq_ref, cmx_ref,
                 *, seq, topk):
    n = pl.program_id(1)
    q0 = n * BLOCK
    nk = (n + 2) // 2
    qidx = q0 + lax.broadcasted_iota(jnp.int32, (KEY_CHUNK, LANES), 1)
    row_iota = lax.broadcasted_iota(jnp.int32, (KEY_CHUNK, LANES), 0)
    nh = B_HEADS

    for h in range(IDX_HEADS):
        rhs_ref[0:IDX_DIM, h * LANES:(h + 1) * LANES] = (
            qT_ref[A_Q + B_Q + h * IDX_DIM:A_Q + B_Q + (h + 1) * IDX_DIM, :])

    npair = (nk + 1) // 2
    last = nk - 1

    def rows_of(c):
        return pl.ds(pl.multiple_of(c * KEY_CHUNK, KEY_CHUNK), KEY_CHUNK)

    n_piece = B_HEADS * LANES // PIECE
    heads_per_piece = PIECE // LANES

    def rel_piece(ki, j, s_ref):
        cs = slice(j * PIECE, (j + 1) * PIECE)
        s_ref[:, cs] = jnp.dot(ki, rhs_ref[0:IDX_DIM, cs], preferred_element_type=jnp.float32)

    def score_piece(s_ref, j, sc):
        for h in range(j * heads_per_piece, (j + 1) * heads_per_piece):
            sc = sc + jnp.maximum(s_ref[:, h * LANES:(h + 1) * LANES], 0.0) * wT_ref[h:h + 1, :]
        return sc

    def score_half(c_next, s_next, c_cur, s_cur):
        ki = kN_ref[rows_of(c_next), 2 * LANES:2 * LANES + IDX_DIM]
        sc = jnp.zeros((KEY_CHUNK, LANES), jnp.float32)
        for j in range(n_piece):
            rel_piece(ki, j, s_next)
            sc = score_piece(s_cur, j, sc)
        kidx = c_cur * KEY_CHUNK + row_iota
        score_ref[rows_of(c_cur), :] = jnp.where(kidx <= qidx, sc, -jnp.inf)

    ki0 = kN_ref[0:KEY_CHUNK, 2 * LANES:2 * LANES + IDX_DIM]
    for j in range(n_piece):
        rel_piece(ki0, j, sa_ref)

    def score_body(i, carry):
        c0 = 2 * i
        c1 = jnp.minimum(c0 + 1, last)
        score_half(c1, sb_ref, c0, sa_ref)
        score_half(jnp.minimum(c0 + 2, last), sa_ref, c1, sb_ref)
        return carry

    lax.fori_loop(0, npair, score_body, 0)

    def fold_rows(x):
        return x.reshape(KEY_CHUNK // 8, 8, LANES).sum(axis=0)

    def count(pred_fn):
        def body(c, acc):
            r0 = pl.multiple_of(c * KEY_CHUNK, KEY_CHUNK)
            blk = score_ref[pl.ds(r0, KEY_CHUNK), :]
            return acc + fold_rows(pred_fn(blk, r0).astype(jnp.int32))
        acc = lax.fori_loop(0, nk, body, jnp.zeros((8, LANES), jnp.int32))
        return acc.sum(axis=0, keepdims=True)

    def masked_min(pred_fn):
        def body(c, acc):
            r0 = pl.multiple_of(c * KEY_CHUNK, KEY_CHUNK)
            blk = score_ref[pl.ds(r0, KEY_CHUNK), :]
            v = jnp.where(pred_fn(blk, r0), blk, jnp.inf)
            return jnp.minimum(acc, v.reshape(KEY_CHUNK // 8, 8, LANES).min(axis=0))
        acc = lax.fori_loop(0, nk, body, jnp.full((8, LANES), jnp.inf, jnp.float32))
        return acc.min(axis=0, keepdims=True)

    def search_static(nkk):
        def count_ge(cf):
            acc = jnp.zeros((8, LANES), jnp.int32)
            for c in range(nkk):
                blk = score_ref[c * KEY_CHUNK:(c + 1) * KEY_CHUNK, :]
                acc = acc + fold_rows((blk >= cf).astype(jnp.int32))
            return acc.sum(axis=0, keepdims=True)

        def run():
            c0 = count_ge(0.0)
            nonneg = c0 >= topk
            st0 = (jnp.where(nonneg, jnp.int32(0), jnp.int32(INT_MIN)),
                   jnp.where(nonneg, c0, jnp.int32(nkk * KEY_CHUNK)))

            def bit_body(i, st):
                pfx, c_pfx = st
                cand = pfx + (jnp.int32(1) << (30 - i))
                cnt = count_ge(_ordered_to_f32(cand))
                take = cnt >= topk
                return jnp.where(take, cand, pfx), jnp.where(take, cnt, c_pfx)

            pfx, c_pfx = lax.fori_loop(0, 31, bit_body, st0)
            return _ordered_to_f32(pfx), c_pfx

        return run

    max_nk = (seq // BLOCK + 1) // 2
    lo, c_lo = lax.cond(
        n >= 2,
        lambda: lax.switch(nk - 2, [search_static(j) for j in range(2, max_nk + 1)]),
        lambda: (jnp.full((1, LANES), -jnp.inf, jnp.float32), jnp.zeros((1, LANES), jnp.int32)))
    thr_ref[...] = lo
    jt_ref[...] = jnp.full((1, LANES), jnp.where(n >= 2, seq, -1), jnp.int32)
    unresolved = jnp.max(jnp.where(c_lo > topk, 1, 0), axis=1, keepdims=True)[0, 0] > 0

    qT_all = qT_ref[...]
    eye = (lax.broadcasted_iota(jnp.int32, (LANES, LANES), 0)
           == lax.broadcasted_iota(jnp.int32, (LANES, LANES), 1)).astype(jnp.bfloat16)
    eye_t = jnp.concatenate([eye] * B_HEADS, axis=1)
    rhs_ref[...] = jnp.concatenate([_head_rhs(qT_all, A_Q), eye_t], axis=0)
    ones_rows = jnp.ones((ONES_ROWS, KEY_CHUNK), jnp.bfloat16)
    o_ref[...] = jnp.zeros(o_ref.shape, jnp.float32)

    @pl.when(unresolved)
    def _resolve_ties():
        def counts(bv):
            return (count(lambda blk, r0: blk > bv), count(lambda blk, r0: blk >= bv))

        b0 = masked_min(lambda blk, r0: blk >= lo)
        cgt0, cge0 = counts(b0)

        def w_cond(st):
            return jnp.max(st[1], axis=1, keepdims=True)[0, 0] >= topk

        def w_body(st):
            bv, cgt, _ = st
            nxt = masked_min(lambda blk, r0: blk > bv)
            bn = jnp.where(cgt >= topk, nxt, bv)
            cgt2, cge2 = counts(bn)
            return bn, cgt2, cge2

        bv, cgt, _ = lax.while_loop(w_cond, w_body, (b0, cgt0, cge0))
        thr_ref[...] = bv
        need = topk - cgt
        nbits = max(1, int(np.ceil(np.log2(seq))))

        def jbody(i, pj):
            step = jnp.int32(1) << (nbits - 1 - i)
            cand = pj + step - 1
            f = count(lambda blk, r0: (blk == bv) & ((r0 + row_iota) <= cand))
            return jnp.where(f >= need, pj, pj + step)

        jt_ref[...] = lax.fori_loop(0, nbits, jbody, jnp.zeros((1, LANES), jnp.int32))

    thr = thr_ref[...]
    jt = jt_ref[...]

    def softmax_heads(s_all, m_of):
        p_heads, maxes = [], []
        for h in range(nh):
            sl = slice(h * LANES, (h + 1) * LANES)
            s_h = s_all[:, sl]
            m_old = m_of(h, sl)
            m_new = jnp.maximum(m_old, jnp.max(s_h, axis=0, keepdims=True))
            p_heads.append(jnp.exp(s_h - m_new).astype(jnp.bfloat16))
            maxes.append((m_old, m_new))
        return p_heads, maxes

    def qk_chunk(c, carry):
        sc = score_ref[rows_of(c), :]
        kidx = c * KEY_CHUNK + row_iota
        sel = (sc > thr) | ((sc == thr) & (kidx <= jt))
        bias = jnp.where(sel, 0.0, NEG_BIG).astype(jnp.bfloat16)
        lhs = jnp.concatenate([kN_ref[rows_of(c), LANES:2 * LANES], bias], axis=1)
        for j in range(n_piece):
            cs = slice(j * PIECE, (j + 1) * PIECE)
            s = jnp.dot(lhs, rhs_ref[:, cs], preferred_element_type=jnp.float32)
            sq_ref[rows_of(c), cs] = s
            for hh in range(heads_per_piece):
                sl = slice(j * PIECE + hh * LANES, j * PIECE + (hh + 1) * LANES)
                cmx_ref[c, :, sl] = jnp.max(s[:, hh * LANES:(hh + 1) * LANES], axis=0, keepdims=True)
        return carry

    def softmax_chunk(c, carry):
        p_ref[rows_of(c), :] = jnp.exp(sq_ref[rows_of(c), :] - m_ref[...]).astype(p_ref.dtype)
        return carry

    ones2 = jnp.ones((ONES_ROWS, 2 * KEY_CHUNK), jnp.bfloat16)

    def pv_pair(i, carry):
        prow = pl.ds(pl.multiple_of(i * 2 * KEY_CHUNK, 2 * KEY_CHUNK), 2 * KEY_CHUNK)
        for g in range(B_KV_HEADS):
            rows = slice(A_KV + g * HEAD_DIM, A_KV + (g + 1) * HEAD_DIM)
            vg = jnp.concatenate([vT_ref[4 * i + t, rows, :] for t in range(4)], axis=1)
            vg = jnp.concatenate([vg, ones2], axis=0)
            csl = slice(g * GROUP * LANES, (g + 1) * GROUP * LANES)
            pv = jnp.dot(vg, p_ref[prow, csl], preferred_element_type=jnp.float32)
            o_ref[:, csl] = o_ref[:, csl] + pv[0:O_ROWS]
        return carry

    rhs_a = jnp.concatenate([_head_rhs(qT_all, 0), eye_t], axis=0)
    wblk = jnp.maximum(n - 1, 0)
    w0 = pl.multiple_of(wblk * BLOCK, BLOCK)
    delta = qidx - (w0 + row_iota)
    band = (delta >= 0) & (delta < WINDOW)
    bias = jnp.where(band, 0.0, NEG_BIG).astype(jnp.bfloat16)
    lhs = jnp.concatenate([kN_ref[pl.ds(w0, 2 * BLOCK), 0:LANES], bias], axis=1)
    s_all = jnp.dot(lhs, rhs_a, preferred_element_type=jnp.float32)
    p_heads, ms = softmax_heads(
        s_all, lambda h, sl: jnp.full((1, LANES), sink_ref[h], jnp.float32))
    for g in range(A_KV_HEADS):
        pT = jnp.concatenate(p_heads[g * GROUP:(g + 1) * GROUP], axis=1)
        rows = slice(g * HEAD_DIM, (g + 1) * HEAD_DIM)
        vg = jnp.concatenate([vT_ref[wblk, rows, :], vT_ref[wblk + 1, rows, :]], axis=1)
        vg = jnp.concatenate([vg, ones_rows], axis=0)
        og = jnp.dot(vg, pT, preferred_element_type=jnp.float32)
        for j in range(GROUP):
            h = g * GROUP + j
            sink, m = ms[h]
            csl = slice(j * LANES, (j + 1) * LANES)
            l = og[HEAD_DIM:HEAD_DIM + 1, csl] + jnp.exp(sink - m)
            yT_ref[h * HEAD_DIM:(h + 1) * HEAD_DIM, :] = og[0:HEAD_DIM, csl] * (1.0 / l)

    def qk_body(i, carry):
        qk_chunk(2 * i, carry)
        return qk_chunk(jnp.minimum(2 * i + 1, last), carry)

    lax.fori_loop(0, npair, qk_body, 0)
    m_ref[...] = lax.fori_loop(0, nk, lambda c, m: jnp.maximum(m, cmx_ref[c]),
                               jnp.full(m_ref.shape, NEG_BIG, jnp.float32))
    lax.fori_loop(0, nk, softmax_chunk, 0)

    @pl.when(nk < 2 * npair)
    def _zero_pad():
        p_ref[rows_of(nk), :] = jnp.zeros((KEY_CHUNK, B_HEADS * LANES), p_ref.dtype)

    lax.fori_loop(0, npair, pv_pair, 0)

    ob = o_ref[0:HEAD_DIM, :] * (1.0 / o_ref[HEAD_DIM:HEAD_DIM + 1, :])
    for h in range(nh):
        yT_ref[A_Q + h * HEAD_DIM:A_Q + (h + 1) * HEAD_DIM, :] = ob[:, h * LANES:(h + 1) * LANES]

    normed = []
    for r0, width, g_ref in ((0, A_Q, ga_ref), (A_Q, B_Q, gb_ref)):
        blk = yT_ref[r0:r0 + width, :]
        ms = jnp.sum(blk * blk, axis=0, keepdims=True) * (1.0 / width)
        normed.append((blk * lax.rsqrt(ms + RMS_EPS) * g_ref[...]).astype(jnp.bfloat16))
    y_t = jnp.concatenate(normed, axis=0)
    y = lax.dot_general(eye, y_t, (((1,), (1,)), ((), ())), preferred_element_type=jnp.float32)
    y_ref[...] = y.astype(y_ref.dtype)


def _attention(sinks, qT, kN, vT, wT, g_a, g_b):
    bsz, s, _ = kN.shape
    nb = s // BLOCK
    topk = min(TOPK_MAX, s // 4)
    kern = functools.partial(_attn_kernel, seq=s, topk=topk)
    return pl.pallas_call(
        kern,
        out_shape=jax.ShapeDtypeStruct((bsz, s, A_Q + B_Q), jnp.bfloat16),
        grid=(bsz, nb),
        in_specs=[
            pl.BlockSpec(memory_space=pltpu.SMEM),
            pl.BlockSpec((None, None, Q_ROWS, BLOCK), lambda b, n: (b, n, 0, 0)),
            pl.BlockSpec((None, s, K_ROWS), lambda b, n: (b, 0, 0)),
            pl.BlockSpec((None, nb, V_ROWS, LANES), lambda b, n: (b, 0, 0, 0)),
            pl.BlockSpec((None, None, W_ROWS, BLOCK), lambda b, n: (b, n, 0, 0)),
            pl.BlockSpec((A_Q, LANES), lambda b, n: (0, 0)),
            pl.BlockSpec((B_Q, LANES), lambda b, n: (0, 0)),
        ],
        out_specs=pl.BlockSpec((None, BLOCK, A_Q + B_Q), lambda b, n: (b, n, 0)),
        scratch_shapes=[
            pltpu.VMEM((s, LANES), jnp.float32),
            pltpu.VMEM((1, LANES), jnp.float32),
            pltpu.VMEM((1, LANES), jnp.int32),
            pltpu.VMEM((1, B_HEADS * LANES), jnp.float32),
            pltpu.VMEM((O_ROWS, B_HEADS * LANES), jnp.float32),
            pltpu.VMEM((A_Q + B_Q, LANES), jnp.float32),
            pltpu.VMEM((KEY_CHUNK, B_HEADS * LANES), jnp.float32),
            pltpu.VMEM((KEY_CHUNK, B_HEADS * LANES), jnp.float32),
            pltpu.VMEM((s, B_HEADS * LANES), jnp.bfloat16),
            pltpu.VMEM((2 * LANES, B_HEADS * LANES), jnp.bfloat16),
            pltpu.VMEM((s, B_HEADS * LANES), jnp.float32),
            pltpu.VMEM(((nb + 1) // 2, 1, B_HEADS * LANES), jnp.float32),
        ],
        compiler_params=pltpu.CompilerParams(
            dimension_semantics=("arbitrary", "arbitrary"),
            vmem_limit_bytes=VMEM_LIMIT),
        name="attention",
    )(sinks, qT, kN, vT, wT, g_a, g_b)


def _post_kernel(y_ref, x_ref, gt1_ref, wo_ref, g1_ref, b1_ref, sc_ref, sh_ref, gt2_ref,
                 wg_ref, wu_ref, wd_ref, g2_ref, b2_ref, o_ref, *, alpha):
    proj = jnp.dot(y_ref[...], wo_ref[...], preferred_element_type=jnp.float32)
    x1 = _layer_norm(alpha * x_ref[...] + (1.0 + gt1_ref[...]) * proj, g1_ref[...], b1_ref[...])
    h = (x1 * (1.0 + sc_ref[...]) + sh_ref[...]).astype(jnp.bfloat16)
    gate = jnp.dot(h, wg_ref[...], preferred_element_type=jnp.float32)
    up = jnp.dot(h, wu_ref[...], preferred_element_type=jnp.float32)
    act = (_silu(gate) * up).astype(jnp.bfloat16)
    y2 = jnp.dot(act, wd_ref[...], preferred_element_type=jnp.float32)
    o_ref[...] = _layer_norm(alpha * x1 + (1.0 + gt2_ref[...]) * y2, g2_ref[...], b2_ref[...])


def _post_attention(y, x, gt1, w_o, g1, b1, sc2, sh2, gt2, w_gate, w_up, w_down, g2, b2, *, alpha, tm):
    bsz, s, d = x.shape
    dy = y.shape[-1]
    dff = w_gate.shape[-1]
    resident = pl.Buffered(1)
    tile = lambda width: pl.BlockSpec((None, tm, width), lambda b_, i: (b_, i, 0))
    per_batch = pl.BlockSpec((None, 1, d), lambda b_, i: (b_, 0, 0))
    row = pl.BlockSpec((1, d), lambda b_, i: (0, 0))
    weight = lambda r, c: pl.BlockSpec((r, c), lambda b_, i: (0, 0), pipeline_mode=resident)
    return pl.pallas_call(
        functools.partial(_post_kernel, alpha=alpha),
        out_shape=jax.ShapeDtypeStruct((bsz, s, d), jnp.float32),
        grid=(bsz, s // tm),
        in_specs=[tile(dy), tile(d), per_batch, weight(dy, d), row, row,
                  per_batch, per_batch, per_batch,
                  weight(d, dff), weight(d, dff), weight(dff, d), row, row],
        out_specs=tile(d),
        compiler_params=pltpu.CompilerParams(
            dimension_semantics=("arbitrary", "arbitrary"),
            vmem_limit_bytes=VMEM_LIMIT),
        name="out_proj_ffn",
    )(y, x, gt1, w_o, g1, b1, sc2, sh2, gt2, w_gate, w_up, w_down, g2, b2)


def _regroup_w_in(w_in_l):
    d = w_in_l.shape[0]
    splits = np.cumsum([A_Q, A_KV, A_KV, B_Q, B_KV, B_KV, I_Q, IDX_DIM, IDX_HEADS])
    qa, ka, va, qb, kb, vb, qi, ki, wi = jnp.split(w_in_l, splits[:-1], axis=1)
    zk = jnp.zeros((d, K_ROWS - A_KV - B_KV - IDX_DIM), w_in_l.dtype)
    zw = jnp.zeros((d, 2 * W_ROWS - IDX_HEADS), w_in_l.dtype)
    cols = jnp.concatenate([qa, qb, qi, ka, kb, ki, zk, va, vb, wi, zw], axis=1)
    return cols.T.astype(jnp.bfloat16)


def kernel(x, c, positions, w_ada, b_ada, w_in, attn_sinks, g_a, g_b, w_o,
           ln1_g, ln1_b, w_gate, w_up, w_down, ln2_g, ln2_b):
    bsz, s, d = x.shape
    depth = w_ada.shape[0]
    alpha = (2.0 * depth) ** 0.25
    bf = jnp.bfloat16

    mod = _adaln_mod(c, w_ada, b_ada)
    inv = (ROPE_THETA ** (-jnp.arange(HALF, dtype=jnp.float32) / HALF)).reshape(HALF, 1)
    pos3 = positions.reshape(bsz, 1, s)

    for l in range(depth):
        m = mod[l].reshape(bsz, 1, 6 * d)
        sh1, sc1, gt1, sh2, sc2, gt2 = [m[:, :, i * d:(i + 1) * d] for i in range(6)]
        qT, kN, vT, wT = _inproj(x, sc1, sh1, pos3, inv, _regroup_w_in(w_in[l]), tm=512)
        y = _attention(attn_sinks[l], qT, kN, vT, wT,
                       jnp.broadcast_to(g_a[l][:, None], (A_Q, LANES)),
                       jnp.broadcast_to(g_b[l][:, None], (B_Q, LANES)))
        x = _post_attention(y, x, gt1, w_o[l].astype(bf), ln1_g[l].reshape(1, d), ln1_b[l].reshape(1, d),
                            sc2, sh2, gt2, w_gate[l].astype(bf), w_up[l].astype(bf), w_down[l].astype(bf),
                            ln2_g[l].reshape(1, d), ln2_b[l].reshape(1, d), alpha=alpha, tm=512)
    return x
```

```python
import functools

import jax
import jax.numpy as jnp
import numpy as np
from jax import lax
from jax.experimental import pallas as pl
from jax.experimental.pallas import tpu as pltpu

HEAD_DIM = 64
HALF = HEAD_DIM // 2
A_HEADS = 8
A_KV_HEADS = 2
B_HEADS = 8
B_KV_HEADS = 2
IDX_HEADS = 8
IDX_DIM = 64
WINDOW = 128
BLOCK = 128
TOPK_MAX = 256
ROPE_THETA = 10000.0
LN_EPS = 1e-5
RMS_EPS = 1e-6

A_Q = A_HEADS * HEAD_DIM
A_KV = A_KV_HEADS * HEAD_DIM
B_Q = B_HEADS * HEAD_DIM
B_KV = B_KV_HEADS * HEAD_DIM
I_Q = IDX_HEADS * IDX_DIM
GROUP = A_HEADS // A_KV_HEADS

LANES = 128
KEY_CHUNK = 256
PIECE = 256
ONES_ROWS = 16
O_ROWS = HEAD_DIM + 8
NEG_BIG = -1e30
INT_MIN = -(2 ** 31)
VMEM_LIMIT = 56 * 1024 * 1024

Q_ROWS = A_Q + B_Q + I_Q
K_ROWS = 3 * LANES
V_ROWS = A_KV + B_KV
W_ROWS = 8
P_ROWS = Q_ROWS + K_ROWS + V_ROWS + 2 * W_ROWS


def _silu(x):
    return x * (1.0 / (1.0 + jnp.exp(-x)))


def _layer_norm(z, g, b):
    mu = jnp.mean(z, axis=-1, keepdims=True)
    zc = z - mu
    var = jnp.mean(zc * zc, axis=-1, keepdims=True)
    return zc * lax.rsqrt(var + LN_EPS) * g + b


def _mod_kernel(c_ref, w_ref, b_ref, o_ref):
    c_act = _silu(c_ref[...]).astype(jnp.bfloat16)
    o_ref[...] = jnp.dot(c_act, w_ref[...].astype(jnp.bfloat16),
                         preferred_element_type=jnp.float32) + b_ref[...]


def _adaln_mod(c, w_ada, b_ada):
    depth, d, d6 = w_ada.shape
    bsz = c.shape[0]
    tn = 1024
    return pl.pallas_call(
        _mod_kernel,
        out_shape=jax.ShapeDtypeStruct((depth, bsz, d6), jnp.float32),
        grid=(depth, d6 // tn),
        in_specs=[
            pl.BlockSpec((bsz, d), lambda l, j: (0, 0)),
            pl.BlockSpec((None, d, tn), lambda l, j: (l, 0, j)),
            pl.BlockSpec((None, 1, tn), lambda l, j: (l, 0, j)),
        ],
        out_specs=pl.BlockSpec((None, bsz, tn), lambda l, j: (l, 0, j)),
        compiler_params=pltpu.CompilerParams(
            dimension_semantics=("arbitrary", "arbitrary"),
            vmem_limit_bytes=VMEM_LIMIT),
        name="adaln_mod",
    )(c, w_ada, b_ada.reshape(depth, 1, d6))


def _rope_head(x, r, cos, sin, scale=None):
    x1, x2 = x[r:r + HALF], x[r + HALF:r + HEAD_DIM]
    o1 = x1 * cos - x2 * sin
    o2 = x2 * cos + x1 * sin
    if scale is not None:
        o1, o2 = o1 * scale, o2 * scale
    return o1, o2


def _inproj_kernel(x_ref, sc_ref, sh_ref, pos_ref, inv_ref, w_ref,
                   qT_ref, kN_ref, vT_ref, wT_ref, *, w_scale):
    h = x_ref[...] * (1.0 + sc_ref[...]) + sh_ref[...]
    pT = lax.dot_general(w_ref[...], h.astype(jnp.bfloat16),
                         (((1,), (1,)), ((), ())),
                         preferred_element_type=jnp.float32)
    ang = inv_ref[...] * pos_ref[...].astype(jnp.float32)
    cos, sin = jnp.cos(ang), jnp.sin(ang)

    nblk = vT_ref.shape[0]
    for hd in range(Q_ROWS // HEAD_DIM):
        r = hd * HEAD_DIM
        scale = HEAD_DIM ** -0.5 if r < A_Q + B_Q else None
        o1, o2 = _rope_head(pT, r, cos, sin, scale)
        o1, o2 = o1.astype(qT_ref.dtype), o2.astype(qT_ref.dtype)
        for j in range(nblk):
            qT_ref[j, r:r + HALF, :] = o1[:, j * LANES:(j + 1) * LANES]
            qT_ref[j, r + HALF:r + HEAD_DIM, :] = o2[:, j * LANES:(j + 1) * LANES]

    k_rows = []
    for hd in range(K_ROWS // HEAD_DIM):
        k_rows.extend(_rope_head(pT, Q_ROWS + hd * HEAD_DIM, cos, sin))
    kN_ref[...] = jnp.concatenate(k_rows, axis=0).T.astype(kN_ref.dtype)

    v0 = Q_ROWS + K_ROWS
    v = pT[v0:v0 + V_ROWS].astype(vT_ref.dtype)
    w0 = v0 + V_ROWS
    w = pT[w0:w0 + W_ROWS] * w_scale
    for j in range(nblk):
        vT_ref[j] = v[:, j * LANES:(j + 1) * LANES]
        wT_ref[j] = w[:, j * LANES:(j + 1) * LANES]


def _inproj(x, sc, sh, pos3, inv, w_t, *, tm):
    bsz, s, d = x.shape
    kern = functools.partial(_inproj_kernel, w_scale=IDX_HEADS ** -0.5 * IDX_DIM ** -0.5)
    return pl.pallas_call(
        kern,
        out_shape=(
            jax.ShapeDtypeStruct((bsz, s // LANES, Q_ROWS, LANES), jnp.bfloat16),
            jax.ShapeDtypeStruct((bsz, s, K_ROWS), jnp.bfloat16),
            jax.ShapeDtypeStruct((bsz, s // LANES, V_ROWS, LANES), jnp.bfloat16),
            jax.ShapeDtypeStruct((bsz, s // LANES, W_ROWS, LANES), jnp.float32),
        ),
        grid=(bsz, s // tm),
        in_specs=[
            pl.BlockSpec((None, tm, d), lambda b, i: (b, i, 0)),
            pl.BlockSpec((None, 1, d), lambda b, i: (b, 0, 0)),
            pl.BlockSpec((None, 1, d), lambda b, i: (b, 0, 0)),
            pl.BlockSpec((None, 1, tm), lambda b, i: (b, 0, i)),
            pl.BlockSpec((HALF, 1), lambda b, i: (0, 0)),
            pl.BlockSpec((P_ROWS, d), lambda b, i: (0, 0)),
        ],
        out_specs=(
            pl.BlockSpec((None, tm // LANES, Q_ROWS, LANES), lambda b, i: (b, i, 0, 0)),
            pl.BlockSpec((None, tm, K_ROWS), lambda b, i: (b, i, 0)),
            pl.BlockSpec((None, tm // LANES, V_ROWS, LANES), lambda b, i: (b, i, 0, 0)),
            pl.BlockSpec((None, tm // LANES, W_ROWS, LANES), lambda b, i: (b, i, 0, 0)),
        ),
        compiler_params=pltpu.CompilerParams(
            dimension_semantics=("arbitrary", "arbitrary"),
            vmem_limit_bytes=VMEM_LIMIT),
        name="in_proj",
    )(x, sc, sh, pos3, inv, w_t)


def _ordered_to_f32(c):
    bits = jnp.where(c >= 0, c, c ^ jnp.int32(0x7FFFFFFF))
    return lax.bitcast_convert_type(bits, jnp.float32)


def _head_rhs(qT, row0):
    zeros = jnp.zeros((HEAD_DIM, LANES), qT.dtype)
    cols = []
    for h in range(A_HEADS):
        blk = qT[row0 + h * HEAD_DIM:row0 + (h + 1) * HEAD_DIM, :]
        g = h // GROUP
        cols.append(jnp.concatenate([blk, zeros] if g == 0 else [zeros, blk], axis=0))
    return jnp.concatenate(cols, axis=1)


def _attn_kernel(sink_ref, qT_ref, kN_ref, vT_ref, wT_ref, ga_ref, gb_ref,
                 y_ref, score_ref, thr_ref, jt_ref, m_ref, o_ref, yT_ref,
                 sa_ref, sb_ref, p_ref, rhs_ref, sq_ref, cmx_ref,
                 *, seq, topk):
    n = pl.program_id(1)
    q0 = n * BLOCK
    nk = (n + 2) // 2
    qidx = q0 + lax.broadcasted_iota(jnp.int32, (KEY_CHUNK, LANES), 1)
    row_iota = lax.broadcasted_iota(jnp.int32, (KEY_CHUNK, LANES), 0)
    nh = B_HEADS

    def identity():
        return (lax.broadcasted_iota(jnp.int32, (LANES, LANES), 0)
                == lax.broadcasted_iota(jnp.int32, (LANES, LANES), 1)).astype(jnp.bfloat16)

    def mask_rhs(row0):
        return jnp.concatenate([_head_rhs(qT_ref[...], row0),
                                jnp.concatenate([identity()] * B_HEADS, axis=1)], axis=0)

    ones_rows = jnp.ones((ONES_ROWS, KEY_CHUNK), jnp.bfloat16)

    def finish_group(r0, width, g_ref):
        blk = yT_ref[r0:r0 + width, :]
        ms = jnp.sum(blk * blk, axis=0, keepdims=True) * (1.0 / width)
        y_t = (blk * lax.rsqrt(ms + RMS_EPS) * g_ref[...]).astype(jnp.bfloat16)
        y = lax.dot_general(identity(), y_t, (((1,), (1,)), ((), ())), preferred_element_type=jnp.float32)
        y_ref[:, r0:r0 + width] = y.astype(y_ref.dtype)


    def softmax_heads(s_all, m_of):
        p_heads, maxes = [], []
        for h in range(nh):
            sl = slice(h * LANES, (h + 1) * LANES)
            s_h = s_all[:, sl]
            m_old = m_of(h, sl)
            m_new = jnp.maximum(m_old, jnp.max(s_h, axis=0, keepdims=True))
            p_heads.append(jnp.exp(s_h - m_new).astype(jnp.bfloat16))
            maxes.append((m_old, m_new))
        return p_heads, maxes

    rhs_a = mask_rhs(0)
    wblk = jnp.maximum(n - 1, 0)
    w0 = pl.multiple_of(wblk * BLOCK, BLOCK)
    delta = qidx - (w0 + row_iota)
    band = (delta >= 0) & (delta < WINDOW)
    bias = jnp.where(band, 0.0, NEG_BIG).astype(jnp.bfloat16)
    lhs = jnp.concatenate([kN_ref[pl.ds(w0, 2 * BLOCK), 0:LANES], bias], axis=1)
    s_all = jnp.dot(lhs, rhs_a, preferred_element_type=jnp.float32)

    for h in range(IDX_HEADS):
        rhs_ref[0:IDX_DIM, h * LANES:(h + 1) * LANES] = (
            qT_ref[A_Q + B_Q + h * IDX_DIM:A_Q + B_Q + (h + 1) * IDX_DIM, :])
    sa_ref[...] = jnp.dot(kN_ref[0:KEY_CHUNK, 2 * LANES:2 * LANES + IDX_DIM], rhs_ref[0:IDX_DIM, :],
                          preferred_element_type=jnp.float32)

    p_heads, ms = softmax_heads(
        s_all, lambda h, sl: jnp.full((1, LANES), sink_ref[h], jnp.float32))
    for g in range(A_KV_HEADS):
        pT = jnp.concatenate(p_heads[g * GROUP:(g + 1) * GROUP], axis=1)
        rows = slice(g * HEAD_DIM, (g + 1) * HEAD_DIM)
        vg = jnp.concatenate([vT_ref[wblk, rows, :], vT_ref[wblk + 1, rows, :]], axis=1)
        vg = jnp.concatenate([vg, ones_rows], axis=0)
        og = jnp.dot(vg, pT, preferred_element_type=jnp.float32)
        for j in range(GROUP):
            h = g * GROUP + j
            sink, m = ms[h]
            csl = slice(j * LANES, (j + 1) * LANES)
            l = og[HEAD_DIM:HEAD_DIM + 1, csl] + jnp.exp(sink - m)
            yT_ref[h * HEAD_DIM:(h + 1) * HEAD_DIM, :] = og[0:HEAD_DIM, csl] * (1.0 / l)

    npair = (nk + 1) // 2
    last = nk - 1

    def rows_of(c):
        return pl.ds(pl.multiple_of(c * KEY_CHUNK, KEY_CHUNK), KEY_CHUNK)

    n_piece = B_HEADS * LANES // PIECE
    heads_per_piece = PIECE // LANES

    def rel_piece(ki, j, s_ref):
        cs = slice(j * PIECE, (j + 1) * PIECE)
        s_ref[:, cs] = jnp.dot(ki, rhs_ref[0:IDX_DIM, cs], preferred_element_type=jnp.float32)

    def score_piece(s_ref, j, sc):
        for h in range(j * heads_per_piece, (j + 1) * heads_per_piece):
            sc = sc + jnp.maximum(s_ref[:, h * LANES:(h + 1) * LANES], 0.0) * wT_ref[h:h + 1, :]
        return sc

    def score_half(c_next, s_next, c_cur, s_cur):
        ki = kN_ref[rows_of(c_next), 2 * LANES:2 * LANES + IDX_DIM]
        sc = jnp.zeros((KEY_CHUNK, LANES), jnp.float32)
        for j in range(n_piece):
            rel_piece(ki, j, s_next)
            sc = score_piece(s_cur, j, sc)
        kidx = c_cur * KEY_CHUNK + row_iota
        score_ref[rows_of(c_cur), :] = jnp.where(kidx <= qidx, sc, -jnp.inf)

    def score_body(i, carry):
        c0 = 2 * i
        c1 = jnp.minimum(c0 + 1, last)
        score_half(c1, sb_ref, c0, sa_ref)
        score_half(jnp.minimum(c0 + 2, last), sa_ref, c1, sb_ref)
        return carry

    lax.fori_loop(0, npair, score_body, 0)

    def fold_rows(x):
        return x.reshape(KEY_CHUNK // 8, 8, LANES).sum(axis=0)

    def count(pred_fn):
        def body(c, acc):
            r0 = pl.multiple_of(c * KEY_CHUNK, KEY_CHUNK)
            blk = score_ref[pl.ds(r0, KEY_CHUNK), :]
            return acc + fold_rows(pred_fn(blk, r0).astype(jnp.int32))
        acc = lax.fori_loop(0, nk, body, jnp.zeros((8, LANES), jnp.int32))
        return acc.sum(axis=0, keepdims=True)

    def masked_min(pred_fn):
        def body(c, acc):
            r0 = pl.multiple_of(c * KEY_CHUNK, KEY_CHUNK)
            blk = score_ref[pl.ds(r0, KEY_CHUNK), :]
            v = jnp.where(pred_fn(blk, r0), blk, jnp.inf)
            return jnp.minimum(acc, v.reshape(KEY_CHUNK // 8, 8, LANES).min(axis=0))
        acc = lax.fori_loop(0, nk, body, jnp.full((8, LANES), jnp.inf, jnp.float32))
        return acc.min(axis=0, keepdims=True)

    def search_static(nkk):
        def count_ge(cf):
            acc = jnp.zeros((8, LANES), jnp.int32)
            for c in range(nkk):
                blk = score_ref[c * KEY_CHUNK:(c + 1) * KEY_CHUNK, :]
                acc = acc + fold_rows((blk >= cf).astype(jnp.int32))
            return acc.sum(axis=0, keepdims=True)

        def run():
            c0 = count_ge(0.0)
            nonneg = c0 >= topk
            st0 = (jnp.where(nonneg, jnp.int32(0), jnp.int32(INT_MIN)),
                   jnp.where(nonneg, c0, jnp.int32(nkk * KEY_CHUNK)))

            def bit_body(i, st):
                pfx, c_pfx = st
                cand = pfx + (jnp.int32(1) << (30 - i))
                cnt = count_ge(_ordered_to_f32(cand))
                take = cnt >= topk
                return jnp.where(take, cand, pfx), jnp.where(take, cnt, c_pfx)

            pfx, c_pfx = lax.fori_loop(0, 31, bit_body, st0)
            return _ordered_to_f32(pfx), c_pfx

        return run

    max_nk = (seq // BLOCK + 1) // 2
    lo, c_lo = lax.cond(
        n >= 2,
        lambda: lax.switch(nk - 2, [search_static(j) for j in range(2, max_nk + 1)]),
        lambda: (jnp.full((1, LANES), -jnp.inf, jnp.float32), jnp.zeros((1, LANES), jnp.int32)))
    thr_ref[...] = lo
    jt_ref[...] = jnp.full((1, LANES), jnp.where(n >= 2, seq, -1), jnp.int32)
    unresolved = jnp.max(jnp.where(c_lo > topk, 1, 0), axis=1, keepdims=True)[0, 0] > 0

    rhs_ref[...] = mask_rhs(A_Q)
    o_ref[...] = jnp.zeros(o_ref.shape, jnp.float32)

    @pl.when(unresolved)
    def _resolve_ties():
        def counts(bv):
            return (count(lambda blk, r0: blk > bv), count(lambda blk, r0: blk >= bv))

        b0 = masked_min(lambda blk, r0: blk >= lo)
        cgt0, cge0 = counts(b0)

        def w_cond(st):
            return jnp.max(st[1], axis=1, keepdims=True)[0, 0] >= topk

        def w_body(st):
            bv, cgt, _ = st
            nxt = masked_min(lambda blk, r0: blk > bv)
            bn = jnp.where(cgt >= topk, nxt, bv)
            cgt2, cge2 = counts(bn)
            return bn, cgt2, cge2

        bv, cgt, _ = lax.while_loop(w_cond, w_body, (b0, cgt0, cge0))
        thr_ref[...] = bv
        need = topk - cgt
        nbits = max(1, int(np.ceil(np.log2(seq))))

        def jbody(i, pj):
            step = jnp.int32(1) << (nbits - 1 - i)
            cand = pj + step - 1
            f = count(lambda blk, r0: (blk == bv) & ((r0 + row_iota) <= cand))
            return jnp.where(f >= need, pj, pj + step)

        jt_ref[...] = lax.fori_loop(0, nbits, jbody, jnp.zeros((1, LANES), jnp.int32))

    thr = thr_ref[...]
    jt = jt_ref[...]

    def qk_chunk(c, carry):
        sc = score_ref[rows_of(c), :]
        kidx = c * KEY_CHUNK + row_iota
        sel = (sc > thr) | ((sc == thr) & (kidx <= jt))
        bias = jnp.where(sel, 0.0, NEG_BIG).astype(jnp.bfloat16)
        lhs = jnp.concatenate([kN_ref[rows_of(c), LANES:2 * LANES], bias], axis=1)
        for j in range(n_piece):
            cs = slice(j * PIECE, (j + 1) * PIECE)
            s = jnp.dot(lhs, rhs_ref[:, cs], preferred_element_type=jnp.float32)
            sq_ref[rows_of(c), cs] = s
            for hh in range(heads_per_piece):
                sl = slice(j * PIECE + hh * LANES, j * PIECE + (hh + 1) * LANES)
                cmx_ref[c, :, sl] = jnp.max(s[:, hh * LANES:(hh + 1) * LANES], axis=0, keepdims=True)
        return carry

    def softmax_chunk(c, carry):
        p_ref[rows_of(c), :] = jnp.exp(sq_ref[rows_of(c), :] - m_ref[...]).astype(p_ref.dtype)
        return carry

    ones2 = jnp.ones((ONES_ROWS, 2 * KEY_CHUNK), jnp.bfloat16)

    def pv_pair(i, carry):
        prow = pl.ds(pl.multiple_of(i * 2 * KEY_CHUNK, 2 * KEY_CHUNK), 2 * KEY_CHUNK)
        for g in range(B_KV_HEADS):
            rows = slice(A_KV + g * HEAD_DIM, A_KV + (g + 1) * HEAD_DIM)
            vg = jnp.concatenate([vT_ref[4 * i + t, rows, :] for t in range(4)], axis=1)
            vg = jnp.concatenate([vg, ones2], axis=0)
            csl = slice(g * GROUP * LANES, (g + 1) * GROUP * LANES)
            pv = jnp.dot(vg, p_ref[prow, csl], preferred_element_type=jnp.float32)
            o_ref[:, csl] = o_ref[:, csl] + pv[0:O_ROWS]
        return carry

    def qk_body(i, carry):
        qk_chunk(2 * i, carry)
        return qk_chunk(jnp.minimum(2 * i + 1, last), carry)

    lax.fori_loop(0, npair, qk_body, 0)
    m_ref[...] = lax.fori_loop(0, nk, lambda c, m: jnp.maximum(m, cmx_ref[c]),
                               jnp.full(m_ref.shape, NEG_BIG, jnp.float32))
    lax.fori_loop(0, nk, softmax_chunk, 0)

    @pl.when(nk < 2 * npair)
    def _zero_pad():
        p_ref[rows_of(nk), :] = jnp.zeros((KEY_CHUNK, B_HEADS * LANES), p_ref.dtype)

    lax.fori_loop(0, npair, pv_pair, 0)

    ob = o_ref[0:HEAD_DIM, :] * (1.0 / o_ref[HEAD_DIM:HEAD_DIM + 1, :])
    for h in range(nh):
        yT_ref[A_Q + h * HEAD_DIM:A_Q + (h + 1) * HEAD_DIM, :] = ob[:, h * LANES:(h + 1) * LANES]

    finish_group(0, A_Q, ga_ref)
    finish_group(A_Q, B_Q, gb_ref)


def _attention(sinks, qT, kN, vT, wT, g_a, g_b):
    bsz, s, _ = kN.shape
    nb = s // BLOCK
    topk = min(TOPK_MAX, s // 4)
    kern = functools.partial(_attn_kernel, seq=s, topk=topk)
    return pl.pallas_call(
        kern,
        out_shape=jax.ShapeDtypeStruct((bsz, s, A_Q + B_Q), jnp.bfloat16),
        grid=(bsz, nb),
        in_specs=[
            pl.BlockSpec(memory_space=pltpu.SMEM),
            pl.BlockSpec((None, None, Q_ROWS, BLOCK), lambda b, n: (b, n, 0, 0)),
            pl.BlockSpec((None, s, K_ROWS), lambda b, n: (b, 0, 0)),
            pl.BlockSpec((None, nb, V_ROWS, LANES), lambda b, n: (b, 0, 0, 0)),
            pl.BlockSpec((None, None, W_ROWS, BLOCK), lambda b, n: (b, n, 0, 0)),
            pl.BlockSpec((A_Q, LANES), lambda b, n: (0, 0)),
            pl.BlockSpec((B_Q, LANES), lambda b, n: (0, 0)),
        ],
        out_specs=pl.BlockSpec((None, BLOCK, A_Q + B_Q), lambda b, n: (b, n, 0)),
        scratch_shapes=[
            pltpu.VMEM((s, LANES), jnp.float32),
            pltpu.VMEM((1, LANES), jnp.float32),
            pltpu.VMEM((1, LANES), jnp.int32),
            pltpu.VMEM((1, B_HEADS * LANES), jnp.float32),
            pltpu.VMEM((O_ROWS, B_HEADS * LANES), jnp.float32),
            pltpu.VMEM((A_Q + B_Q, LANES), jnp.float32),
            pltpu.VMEM((KEY_CHUNK, B_HEADS * LANES), jnp.float32),
            pltpu.VMEM((KEY_CHUNK, B_HEADS * LANES), jnp.float32),
            pltpu.VMEM((s, B_HEADS * LANES), jnp.bfloat16),
            pltpu.VMEM((2 * LANES, B_HEADS * LANES), jnp.bfloat16),
            pltpu.VMEM((s, B_HEADS * LANES), jnp.float32),
            pltpu.VMEM(((nb + 1) // 2, 1, B_HEADS * LANES), jnp.float32),
        ],
        compiler_params=pltpu.CompilerParams(
            dimension_semantics=("arbitrary", "arbitrary"),
            vmem_limit_bytes=VMEM_LIMIT),
        name="attention",
    )(sinks, qT, kN, vT, wT, g_a, g_b)


def _post_kernel(y_ref, x_ref, gt1_ref, wo_ref, g1_ref, b1_ref, sc_ref, sh_ref, gt2_ref,
                 wg_ref, wu_ref, wd_ref, g2_ref, b2_ref, o_ref, *, alpha):
    proj = jnp.dot(y_ref[...], wo_ref[...], preferred_element_type=jnp.float32)
    x1 = _layer_norm(alpha * x_ref[...] + (1.0 + gt1_ref[...]) * proj, g1_ref[...], b1_ref[...])
    h = (x1 * (1.0 + sc_ref[...]) + sh_ref[...]).astype(jnp.bfloat16)
    gate = jnp.dot(h, wg_ref[...], preferred_element_type=jnp.float32)
    up = jnp.dot(h, wu_ref[...], preferred_element_type=jnp.float32)
    act = (_silu(gate) * up).astype(jnp.bfloat16)
    y2 = jnp.dot(act, wd_ref[...], preferred_element_type=jnp.float32)
    o_ref[...] = _layer_norm(alpha * x1 + (1.0 + gt2_ref[...]) * y2, g2_ref[...], b2_ref[...])


def _post_attention(y, x, gt1, w_o, g1, b1, sc2, sh2, gt2, w_gate, w_up, w_down, g2, b2, *, alpha, tm):
    bsz, s, d = x.shape
    dy = y.shape[-1]
    dff = w_gate.shape[-1]
    resident = pl.Buffered(1)
    tile = lambda width: pl.BlockSpec((None, tm, width), lambda b_, i: (b_, i, 0))
    per_batch = pl.BlockSpec((None, 1, d), lambda b_, i: (b_, 0, 0))
    row = pl.BlockSpec((1, d), lambda b_, i: (0, 0))
    weight = lambda r, c: pl.BlockSpec((r, c), lambda b_, i: (0, 0), pipeline_mode=resident)
    return pl.pallas_call(
        functools.partial(_post_kernel, alpha=alpha),
        out_shape=jax.ShapeDtypeStruct((bsz, s, d), jnp.float32),
        grid=(bsz, s // tm),
        in_specs=[tile(dy), tile(d), per_batch, weight(dy, d), row, row,
                  per_batch, per_batch, per_batch,
                  weight(d, dff), weight(d, dff), weight(dff, d), row, row],
        out_specs=tile(d),
        compiler_params=pltpu.CompilerParams(
            dimension_semantics=("arbitrary", "arbitrary"),
            vmem_limit_bytes=VMEM_LIMIT),
        name="out_proj_ffn",
    )(y, x, gt1, w_o, g1, b1, sc2, sh2, gt2, w_gate, w_up, w_down, g2, b2)


def _regroup_w_in(w_in_l):
    d = w_in_l.shape[0]
    splits = np.cumsum([A_Q, A_KV, A_KV, B_Q, B_KV, B_KV, I_Q, IDX_DIM, IDX_HEADS])
    qa, ka, va, qb, kb, vb, qi, ki, wi = jnp.split(w_in_l, splits[:-1], axis=1)
    zk = jnp.zeros((d, K_ROWS - A_KV - B_KV - IDX_DIM), w_in_l.dtype)
    zw = jnp.zeros((d, 2 * W_ROWS - IDX_HEADS), w_in_l.dtype)
    cols = jnp.concatenate([qa, qb, qi, ka, kb, ki, zk, va, vb, wi, zw], axis=1)
    return cols.T.astype(jnp.bfloat16)


def kernel(x, c, positions, w_ada, b_ada, w_in, attn_sinks, g_a, g_b, w_o,
           ln1_g, ln1_b, w_gate, w_up, w_down, ln2_g, ln2_b):
    bsz, s, d = x.shape
    depth = w_ada.shape[0]
    alpha = (2.0 * depth) ** 0.25
    bf = jnp.bfloat16

    mod = _adaln_mod(c, w_ada, b_ada)
    inv = (ROPE_THETA ** (-jnp.arange(HALF, dtype=jnp.float32) / HALF)).reshape(HALF, 1)
    pos3 = positions.reshape(bsz, 1, s)

    for l in range(depth):
        m = mod[l].reshape(bsz, 1, 6 * d)
        sh1, sc1, gt1, sh2, sc2, gt2 = [m[:, :, i * d:(i + 1) * d] for i in range(6)]
        qT, kN, vT, wT = _inproj(x, sc1, sh1, pos3, inv, _regroup_w_in(w_in[l]), tm=512)
        y = _attention(attn_sinks[l], qT, kN, vT, wT,
                       jnp.broadcast_to(g_a[l][:, None], (A_Q, LANES)),
                       jnp.broadcast_to(g_b[l][:, None], (B_Q, LANES)))
        x = _post_attention(y, x, gt1, w_o[l].astype(bf), ln1_g[l].reshape(1, d), ln1_b[l].reshape(1, d),
                            sc2, sh2, gt2, w_gate[l].astype(bf), w_up[l].astype(bf), w_down[l].astype(bf),
                            ln2_g[l].reshape(1, d), ln2_b[l].reshape(1, d), alpha=alpha, tm=512)
    return x
```

```python
import functools

import jax
import jax.numpy as jnp
import numpy as np
from jax import lax
from jax.experimental import pallas as pl
from jax.experimental.pallas import tpu as pltpu

HEAD_DIM = 64
HALF = HEAD_DIM // 2
A_HEADS = 8
A_KV_HEADS = 2
B_HEADS = 8
B_KV_HEADS = 2
IDX_HEADS = 8
IDX_DIM = 64
WINDOW = 128
BLOCK = 128
TOPK_MAX = 256
ROPE_THETA = 10000.0
LN_EPS = 1e-5
RMS_EPS = 1e-6

A_Q = A_HEADS * HEAD_DIM
A_KV = A_KV_HEADS * HEAD_DIM
B_Q = B_HEADS * HEAD_DIM
B_KV = B_KV_HEADS * HEAD_DIM
I_Q = IDX_HEADS * IDX_DIM
GROUP = A_HEADS // A_KV_HEADS

LANES = 128
KEY_CHUNK = 256
PIECE = 256
ONES_ROWS = 16
O_ROWS = HEAD_DIM + 8
NEG_BIG = -1e30
INT_MIN = -(2 ** 31)
VMEM_LIMIT = 56 * 1024 * 1024

Q_ROWS = A_Q + B_Q + I_Q
K_ROWS = 3 * LANES
V_ROWS = A_KV + B_KV
W_ROWS = 8
P_ROWS = Q_ROWS + K_ROWS + V_ROWS + 2 * W_ROWS


def _silu(x):
    return x * (1.0 / (1.0 + jnp.exp(-x)))


def _layer_norm(z, g, b):
    mu = jnp.mean(z, axis=-1, keepdims=True)
    zc = z - mu
    var = jnp.mean(zc * zc, axis=-1, keepdims=True)
    return zc * lax.rsqrt(var + LN_EPS) * g + b


def _mod_kernel(c_ref, w_ref, b_ref, o_ref):
    c_act = _silu(c_ref[...]).astype(jnp.bfloat16)
    o_ref[...] = jnp.dot(c_act, w_ref[...].astype(jnp.bfloat16),
                         preferred_element_type=jnp.float32) + b_ref[...]


def _adaln_mod(c, w_ada, b_ada):
    depth, d, d6 = w_ada.shape
    bsz = c.shape[0]
    tn = 1024
    return pl.pallas_call(
        _mod_kernel,
        out_shape=jax.ShapeDtypeStruct((depth, bsz, d6), jnp.float32),
        grid=(depth, d6 // tn),
        in_specs=[
            pl.BlockSpec((bsz, d), lambda l, j: (0, 0)),
            pl.BlockSpec((None, d, tn), lambda l, j: (l, 0, j)),
            pl.BlockSpec((None, 1, tn), lambda l, j: (l, 0, j)),
        ],
        out_specs=pl.BlockSpec((None, bsz, tn), lambda l, j: (l, 0, j)),
        compiler_params=pltpu.CompilerParams(
            dimension_semantics=("arbitrary", "arbitrary"),
            vmem_limit_bytes=VMEM_LIMIT),
        name="adaln_mod",
    )(c, w_ada, b_ada.reshape(depth, 1, d6))


def _rope_head(x, r, cos, sin, scale=None):
    x1, x2 = x[r:r + HALF], x[r + HALF:r + HEAD_DIM]
    o1 = x1 * cos - x2 * sin
    o2 = x2 * cos + x1 * sin
    if scale is not None:
        o1, o2 = o1 * scale, o2 * scale
    return o1, o2


def _inproj_kernel(x_ref, sc_ref, sh_ref, pos_ref, inv_ref, w_ref,
                   qT_ref, kN_ref, vT_ref, wT_ref, *, w_scale):
    h = x_ref[...] * (1.0 + sc_ref[...]) + sh_ref[...]
    pT = lax.dot_general(w_ref[...], h.astype(jnp.bfloat16),
                         (((1,), (1,)), ((), ())),
                         preferred_element_type=jnp.float32)
    ang = inv_ref[...] * pos_ref[...].astype(jnp.float32)
    cos, sin = jnp.cos(ang), jnp.sin(ang)

    nblk = vT_ref.shape[0]
    for hd in range(Q_ROWS // HEAD_DIM):
        r = hd * HEAD_DIM
        scale = HEAD_DIM ** -0.5 if r < A_Q + B_Q else None
        o1, o2 = _rope_head(pT, r, cos, sin, scale)
        o1, o2 = o1.astype(qT_ref.dtype), o2.astype(qT_ref.dtype)
        for j in range(nblk):
            qT_ref[j, r:r + HALF, :] = o1[:, j * LANES:(j + 1) * LANES]
            qT_ref[j, r + HALF:r + HEAD_DIM, :] = o2[:, j * LANES:(j + 1) * LANES]

    k_rows = []
    for hd in range(K_ROWS // HEAD_DIM):
        k_rows.extend(_rope_head(pT, Q_ROWS + hd * HEAD_DIM, cos, sin))
    kN_ref[...] = jnp.concatenate(k_rows, axis=0).T.astype(kN_ref.dtype)

    v0 = Q_ROWS + K_ROWS
    v = pT[v0:v0 + V_ROWS].astype(vT_ref.dtype)
    w0 = v0 + V_ROWS
    w = pT[w0:w0 + W_ROWS] * w_scale
    for j in range(nblk):
        vT_ref[j] = v[:, j * LANES:(j + 1) * LANES]
        wT_ref[j] = w[:, j * LANES:(j + 1) * LANES]


def _inproj(x, sc, sh, pos3, inv, w_t, *, tm):
    bsz, s, d = x.shape
    kern = functools.partial(_inproj_kernel, w_scale=IDX_HEADS ** -0.5 * IDX_DIM ** -0.5)
    return pl.pallas_call(
        kern,
        out_shape=(
            jax.ShapeDtypeStruct((bsz, s // LANES, Q_ROWS, LANES), jnp.bfloat16),
            jax.ShapeDtypeStruct((bsz, s, K_ROWS), jnp.bfloat16),
            jax.ShapeDtypeStruct((bsz, s // LANES, V_ROWS, LANES), jnp.bfloat16),
            jax.ShapeDtypeStruct((bsz, s // LANES, W_ROWS, LANES), jnp.float32),
        ),
        grid=(bsz, s // tm),
        in_specs=[
            pl.BlockSpec((None, tm, d), lambda b, i: (b, i, 0)),
            pl.BlockSpec((None, 1, d), lambda b, i: (b, 0, 0)),
            pl.BlockSpec((None, 1, d), lambda b, i: (b, 0, 0)),
            pl.BlockSpec((None, 1, tm), lambda b, i: (b, 0, i)),
            pl.BlockSpec((HALF, 1), lambda b, i: (0, 0)),
            pl.BlockSpec((P_ROWS, d), lambda b, i: (0, 0)),
        ],
        out_specs=(
            pl.BlockSpec((None, tm // LANES, Q_ROWS, LANES), lambda b, i: (b, i, 0, 0)),
            pl.BlockSpec((None, tm, K_ROWS), lambda b, i: (b, i, 0)),
            pl.BlockSpec((None, tm // LANES, V_ROWS, LANES), lambda b, i: (b, i, 0, 0)),
            pl.BlockSpec((None, tm // LANES, W_ROWS, LANES), lambda b, i: (b, i, 0, 0)),
        ),
        compiler_params=pltpu.CompilerParams(
            dimension_semantics=("arbitrary", "arbitrary"),
            vmem_limit_bytes=VMEM_LIMIT),
        name="in_proj",
    )(x, sc, sh, pos3, inv, w_t)


def _ordered_to_f32(c):
    bits = jnp.where(c >= 0, c, c ^ jnp.int32(0x7FFFFFFF))
    return lax.bitcast_convert_type(bits, jnp.float32)


def _head_rhs(qT, row0):
    zeros = jnp.zeros((HEAD_DIM, LANES), qT.dtype)
    cols = []
    for h in range(A_HEADS):
        blk = qT[row0 + h * HEAD_DIM:row0 + (h + 1) * HEAD_DIM, :]
        g = h // GROUP
        cols.append(jnp.concatenate([blk, zeros] if g == 0 else [zeros, blk], axis=0))
    return jnp.concatenate(cols, axis=1)


def _attn_kernel(sink_ref, qT_ref, kN_ref, vT_ref, wT_ref, ga_ref, gb_ref,
                 y_ref, score_ref, thr_ref, jt_ref, m_ref, o_ref, yT_ref,
                 sa_ref, p_ref, rhs_ref, sq_ref,
                 *, seq, topk):
    n = pl.program_id(1)
    q0 = n * BLOCK
    nk = (n + 2) // 2
    qidx = q0 + lax.broadcasted_iota(jnp.int32, (KEY_CHUNK, LANES), 1)
    row_iota = lax.broadcasted_iota(jnp.int32, (KEY_CHUNK, LANES), 0)
    nh = B_HEADS

    def identity():
        return (lax.broadcasted_iota(jnp.int32, (LANES, LANES), 0)
                == lax.broadcasted_iota(jnp.int32, (LANES, LANES), 1)).astype(jnp.bfloat16)

    def mask_rhs(row0):
        return jnp.concatenate([_head_rhs(qT_ref[...], row0),
                                jnp.concatenate([identity()] * B_HEADS, axis=1)], axis=0)

    ones_rows = jnp.ones((ONES_ROWS, KEY_CHUNK), jnp.bfloat16)

    def finish_group(r0, width, g_ref):
        blk = yT_ref[r0:r0 + width, :]
        ms = jnp.sum(blk * blk, axis=0, keepdims=True) * (1.0 / width)
        y_t = (blk * lax.rsqrt(ms + RMS_EPS) * g_ref[...]).astype(jnp.bfloat16)
        y = lax.dot_general(identity(), y_t, (((1,), (1,)), ((), ())), preferred_element_type=jnp.float32)
        y_ref[:, r0:r0 + width] = y.astype(y_ref.dtype)


    def softmax_heads(s_all, m_of):
        p_heads, maxes = [], []
        for h in range(nh):
            sl = slice(h * LANES, (h + 1) * LANES)
            s_h = s_all[:, sl]
            m_old = m_of(h, sl)
            m_new = jnp.maximum(m_old, jnp.max(s_h, axis=0, keepdims=True))
            p_heads.append(jnp.exp(s_h - m_new).astype(jnp.bfloat16))
            maxes.append((m_old, m_new))
        return p_heads, maxes

    rhs_a = mask_rhs(0)
    wblk = jnp.maximum(n - 1, 0)
    w0 = pl.multiple_of(wblk * BLOCK, BLOCK)
    delta = qidx - (w0 + row_iota)
    band = (delta >= 0) & (delta < WINDOW)
    bias = jnp.where(band, 0.0, NEG_BIG).astype(jnp.bfloat16)
    lhs = jnp.concatenate([kN_ref[pl.ds(w0, 2 * BLOCK), 0:LANES], bias], axis=1)
    s_all = jnp.dot(lhs, rhs_a, preferred_element_type=jnp.float32)

    for h in range(IDX_HEADS):
        rhs_ref[0:IDX_DIM, h * LANES:(h + 1) * LANES] = (
            qT_ref[A_Q + B_Q + h * IDX_DIM:A_Q + B_Q + (h + 1) * IDX_DIM, :])
    sa_ref[...] = jnp.dot(kN_ref[0:KEY_CHUNK, 2 * LANES:2 * LANES + IDX_DIM], rhs_ref[0:IDX_DIM, :],
                          preferred_element_type=jnp.float32)

    p_heads, ms = softmax_heads(
        s_all, lambda h, sl: jnp.full((1, LANES), sink_ref[h], jnp.float32))
    for g in range(A_KV_HEADS):
        pT = jnp.concatenate(p_heads[g * GROUP:(g + 1) * GROUP], axis=1)
        rows = slice(g * HEAD_DIM, (g + 1) * HEAD_DIM)
        vg = jnp.concatenate([vT_ref[wblk, rows, :], vT_ref[wblk + 1, rows, :]], axis=1)
        vg = jnp.concatenate([vg, ones_rows], axis=0)
        og = jnp.dot(vg, pT, preferred_element_type=jnp.float32)
        for j in range(GROUP):
            h = g * GROUP + j
            sink, m = ms[h]
            csl = slice(j * LANES, (j + 1) * LANES)
            l = og[HEAD_DIM:HEAD_DIM + 1, csl] + jnp.exp(sink - m)
            yT_ref[h * HEAD_DIM:(h + 1) * HEAD_DIM, :] = og[0:HEAD_DIM, csl] * (1.0 / l)

    max_nk = (seq // BLOCK + 1) // 2
    n_piece = B_HEADS * LANES // PIECE
    heads_per_piece = PIECE // LANES

    def score_stream(nkk):
        def run():
            for c in range(nkk):
                rows = slice(c * KEY_CHUNK, (c + 1) * KEY_CHUNK)
                sc = jnp.zeros((KEY_CHUNK, LANES), jnp.float32)
                for j in range(n_piece):
                    cs = slice(j * PIECE, (j + 1) * PIECE)
                    if c == 0:
                        rel = sa_ref[:, cs]
                    else:
                        rel = jnp.dot(kN_ref[rows, 2 * LANES:2 * LANES + IDX_DIM], rhs_ref[0:IDX_DIM, cs],
                                      preferred_element_type=jnp.float32)
                    for hh in range(heads_per_piece):
                        h = j * heads_per_piece + hh
                        sc = sc + jnp.maximum(rel[:, hh * LANES:(hh + 1) * LANES], 0.0) * wT_ref[h:h + 1, :]
                kidx = c * KEY_CHUNK + row_iota
                score_ref[rows, :] = jnp.where(kidx <= qidx, sc, -jnp.inf)
        return run

    def fold_rows(x):
        return x.reshape(KEY_CHUNK // 8, 8, LANES).sum(axis=0)

    def count(pred_fn):
        def body(c, acc):
            r0 = pl.multiple_of(c * KEY_CHUNK, KEY_CHUNK)
            blk = score_ref[pl.ds(r0, KEY_CHUNK), :]
            return acc + fold_rows(pred_fn(blk, r0).astype(jnp.int32))
        acc = lax.fori_loop(0, nk, body, jnp.zeros((8, LANES), jnp.int32))
        return acc.sum(axis=0, keepdims=True)

    def masked_min(pred_fn):
        def body(c, acc):
            r0 = pl.multiple_of(c * KEY_CHUNK, KEY_CHUNK)
            blk = score_ref[pl.ds(r0, KEY_CHUNK), :]
            v = jnp.where(pred_fn(blk, r0), blk, jnp.inf)
            return jnp.minimum(acc, v.reshape(KEY_CHUNK // 8, 8, LANES).min(axis=0))
        acc = lax.fori_loop(0, nk, body, jnp.full((8, LANES), jnp.inf, jnp.float32))
        return acc.min(axis=0, keepdims=True)

    def search_static(nkk):
        def count_ge(cf):
            acc = jnp.zeros((8, LANES), jnp.int32)
            for c in range(nkk):
                blk = score_ref[c * KEY_CHUNK:(c + 1) * KEY_CHUNK, :]
                acc = acc + fold_rows((blk >= cf).astype(jnp.int32))
            return acc.sum(axis=0, keepdims=True)

        def run():
            c0 = count_ge(0.0)
            nonneg = c0 >= topk
            st0 = (jnp.where(nonneg, jnp.int32(0), jnp.int32(INT_MIN)),
                   jnp.where(nonneg, c0, jnp.int32(nkk * KEY_CHUNK)))

            def bit_body(i, st):
                pfx, c_pfx = st
                cand = pfx + (jnp.int32(1) << (30 - i))
                cnt = count_ge(_ordered_to_f32(cand))
                take = cnt >= topk
                return jnp.where(take, cand, pfx), jnp.where(take, cnt, c_pfx)

            pfx, c_pfx = lax.fori_loop(0, 31, bit_body, st0)
            return _ordered_to_f32(pfx), c_pfx

        return run

    def score_and_search(nkk):
        def run():
            score_stream(nkk)()
            if nkk == 1:
                return (jnp.full((1, LANES), -jnp.inf, jnp.float32), jnp.zeros((1, LANES), jnp.int32))
            return search_static(nkk)()
        return run

    lo, c_lo = lax.switch(nk - 1, [score_and_search(j) for j in range(1, max_nk + 1)])
    thr_ref[...] = lo
    jt_ref[...] = jnp.full((1, LANES), jnp.where(n >= 2, seq, -1), jnp.int32)
    unresolved = jnp.max(jnp.where(c_lo > topk, 1, 0), axis=1, keepdims=True)[0, 0] > 0

    rhs_ref[...] = mask_rhs(A_Q)

    @pl.when(unresolved)
    def _resolve_ties():
        def counts(bv):
            return (count(lambda blk, r0: blk > bv), count(lambda blk, r0: blk >= bv))

        b0 = masked_min(lambda blk, r0: blk >= lo)
        cgt0, cge0 = counts(b0)

        def w_cond(st):
            return jnp.max(st[1], axis=1, keepdims=True)[0, 0] >= topk

        def w_body(st):
            bv, cgt, _ = st
            nxt = masked_min(lambda blk, r0: blk > bv)
            bn = jnp.where(cgt >= topk, nxt, bv)
            cgt2, cge2 = counts(bn)
            return bn, cgt2, cge2

        bv, cgt, _ = lax.while_loop(w_cond, w_body, (b0, cgt0, cge0))
        thr_ref[...] = bv
        need = topk - cgt
        nbits = max(1, int(np.ceil(np.log2(seq))))

        def jbody(i, pj):
            step = jnp.int32(1) << (nbits - 1 - i)
            cand = pj + step - 1
            f = count(lambda blk, r0: (blk == bv) & ((r0 + row_iota) <= cand))
            return jnp.where(f >= need, pj, pj + step)

        jt_ref[...] = lax.fori_loop(0, nbits, jbody, jnp.zeros((1, LANES), jnp.int32))

    thr = thr_ref[...]
    jt = jt_ref[...]

    def qk_stream(nkk):
        def run():
            m_fin = [jnp.full((1, LANES), NEG_BIG, jnp.float32) for _ in range(nh)]
            for c in range(nkk):
                rows = slice(c * KEY_CHUNK, (c + 1) * KEY_CHUNK)
                sc = score_ref[rows, :]
                kidx = c * KEY_CHUNK + row_iota
                sel = (sc > thr) | ((sc == thr) & (kidx <= jt))
                bias = jnp.where(sel, 0.0, NEG_BIG).astype(jnp.bfloat16)
                lhs = jnp.concatenate([kN_ref[rows, LANES:2 * LANES], bias], axis=1)
                for j in range(n_piece):
                    cs = slice(j * PIECE, (j + 1) * PIECE)
                    s = jnp.dot(lhs, rhs_ref[:, cs], preferred_element_type=jnp.float32)
                    sq_ref[rows, cs] = s
                    for hh in range(heads_per_piece):
                        h = j * heads_per_piece + hh
                        m_fin[h] = jnp.maximum(
                            m_fin[h], jnp.max(s[:, hh * LANES:(hh + 1) * LANES], axis=0, keepdims=True))
            for h in range(nh):
                m_ref[:, h * LANES:(h + 1) * LANES] = m_fin[h]
        return run

    def softmax_chunk(c, carry):
        rows = pl.ds(pl.multiple_of(c * KEY_CHUNK, KEY_CHUNK), KEY_CHUNK)
        p_ref[rows, :] = jnp.exp(sq_ref[rows, :] - m_ref[...]).astype(p_ref.dtype)
        return carry

    def pv_stream(nkk):
        def run():
            keys = nkk * KEY_CHUNK
            ones = jnp.ones((ONES_ROWS, keys), jnp.bfloat16)
            for g in range(B_KV_HEADS):
                rows = slice(A_KV + g * HEAD_DIM, A_KV + (g + 1) * HEAD_DIM)
                csl = slice(g * GROUP * LANES, (g + 1) * GROUP * LANES)
                vg = jnp.concatenate([vT_ref[t, rows, :] for t in range(keys // LANES)], axis=1)
                vg = jnp.concatenate([vg, ones], axis=0)
                pv = jnp.dot(vg, p_ref[0:keys, csl], preferred_element_type=jnp.float32)
                o_ref[:, csl] = pv[0:O_ROWS]
        return run

    lax.switch(nk - 1, [qk_stream(j) for j in range(1, max_nk + 1)])
    lax.fori_loop(0, nk, softmax_chunk, 0)
    lax.switch(nk - 1, [pv_stream(j) for j in range(1, max_nk + 1)])

    ob = o_ref[0:HEAD_DIM, :] * (1.0 / o_ref[HEAD_DIM:HEAD_DIM + 1, :])
    for h in range(nh):
        yT_ref[A_Q + h * HEAD_DIM:A_Q + (h + 1) * HEAD_DIM, :] = ob[:, h * LANES:(h + 1) * LANES]

    finish_group(0, A_Q, ga_ref)
    finish_group(A_Q, B_Q, gb_ref)


def _attention(sinks, qT, kN, vT, wT, g_a, g_b):
    bsz, s, _ = kN.shape
    nb = s // BLOCK
    topk = min(TOPK_MAX, s // 4)
    kern = functools.partial(_attn_kernel, seq=s, topk=topk)
    return pl.pallas_call(
        kern,
        out_shape=jax.ShapeDtypeStruct((bsz, s, A_Q + B_Q), jnp.bfloat16),
        grid=(bsz, nb),
        in_specs=[
            pl.BlockSpec(memory_space=pltpu.SMEM),
            pl.BlockSpec((None, None, Q_ROWS, BLOCK), lambda b, n: (b, n, 0, 0)),
            pl.BlockSpec((None, s, K_ROWS), lambda b, n: (b, 0, 0)),
            pl.BlockSpec((None, nb, V_ROWS, LANES), lambda b, n: (b, 0, 0, 0)),
            pl.BlockSpec((None, None, W_ROWS, BLOCK), lambda b, n: (b, n, 0, 0)),
            pl.BlockSpec((A_Q, LANES), lambda b, n: (0, 0)),
            pl.BlockSpec((B_Q, LANES), lambda b, n: (0, 0)),
        ],
        out_specs=pl.BlockSpec((None, BLOCK, A_Q + B_Q), lambda b, n: (b, n, 0)),
        scratch_shapes=[
            pltpu.VMEM((s, LANES), jnp.float32),
            pltpu.VMEM((1, LANES), jnp.float32),
            pltpu.VMEM((1, LANES), jnp.int32),
            pltpu.VMEM((1, B_HEADS * LANES), jnp.float32),
            pltpu.VMEM((O_ROWS, B_HEADS * LANES), jnp.float32),
            pltpu.VMEM((A_Q + B_Q, LANES), jnp.float32),
            pltpu.VMEM((KEY_CHUNK, B_HEADS * LANES), jnp.float32),
            pltpu.VMEM((s, B_HEADS * LANES), jnp.bfloat16),
            pltpu.VMEM((2 * LANES, B_HEADS * LANES), jnp.bfloat16),
            pltpu.VMEM((s, B_HEADS * LANES), jnp.float32),
        ],
        compiler_params=pltpu.CompilerParams(
            dimension_semantics=("arbitrary", "arbitrary"),
            vmem_limit_bytes=VMEM_LIMIT),
        name="attention",
    )(sinks, qT, kN, vT, wT, g_a, g_b)


def _post_kernel(y_ref, x_ref, gt1_ref, wo_ref, g1_ref, b1_ref, sc_ref, sh_ref, gt2_ref,
                 wg_ref, wu_ref, wd_ref, g2_ref, b2_ref, o_ref, *, alpha):
    proj = jnp.dot(y_ref[...], wo_ref[...], preferred_element_type=jnp.float32)
    x1 = _layer_norm(alpha * x_ref[...] + (1.0 + gt1_ref[...]) * proj, g1_ref[...], b1_ref[...])
    h = (x1 * (1.0 + sc_ref[...]) + sh_ref[...]).astype(jnp.bfloat16)
    gate = jnp.dot(h, wg_ref[...], preferred_element_type=jnp.float32)
    up = jnp.dot(h, wu_ref[...], preferred_element_type=jnp.float32)
    act = (_silu(gate) * up).astype(jnp.bfloat16)
    y2 = jnp.dot(act, wd_ref[...], preferred_element_type=jnp.float32)
    o_ref[...] = _layer_norm(alpha * x1 + (1.0 + gt2_ref[...]) * y2, g2_ref[...], b2_ref[...])


def _post_attention(y, x, gt1, w_o, g1, b1, sc2, sh2, gt2, w_gate, w_up, w_down, g2, b2, *, alpha, tm):
    bsz, s, d = x.shape
    dy = y.shape[-1]
    dff = w_gate.shape[-1]
    resident = pl.Buffered(1)
    tile = lambda width: pl.BlockSpec((None, tm, width), lambda b_, i: (b_, i, 0))
    per_batch = pl.BlockSpec((None, 1, d), lambda b_, i: (b_, 0, 0))
    row = pl.BlockSpec((1, d), lambda b_, i: (0, 0))
    weight = lambda r, c: pl.BlockSpec((r, c), lambda b_, i: (0, 0), pipeline_mode=resident)
    return pl.pallas_call(
        functools.partial(_post_kernel, alpha=alpha),
        out_shape=jax.ShapeDtypeStruct((bsz, s, d), jnp.float32),
        grid=(bsz, s // tm),
        in_specs=[tile(dy), tile(d), per_batch, weight(dy, d), row, row,
                  per_batch, per_batch, per_batch,
                  weight(d, dff), weight(d, dff), weight(dff, d), row, row],
        out_specs=tile(d),
        compiler_params=pltpu.CompilerParams(
            dimension_semantics=("arbitrary", "arbitrary"),
            vmem_limit_bytes=VMEM_LIMIT),
        name="out_proj_ffn",
    )(y, x, gt1, w_o, g1, b1, sc2, sh2, gt2, w_gate, w_up, w_down, g2, b2)


def _regroup_w_in(w_in_l):
    d = w_in_l.shape[0]
    splits = np.cumsum([A_Q, A_KV, A_KV, B_Q, B_KV, B_KV, I_Q, IDX_DIM, IDX_HEADS])
    qa, ka, va, qb, kb, vb, qi, ki, wi = jnp.split(w_in_l, splits[:-1], axis=1)
    zk = jnp.zeros((d, K_ROWS - A_KV - B_KV - IDX_DIM), w_in_l.dtype)
    zw = jnp.zeros((d, 2 * W_ROWS - IDX_HEADS), w_in_l.dtype)
    cols = jnp.concatenate([qa, qb, qi, ka, kb, ki, zk, va, vb, wi, zw], axis=1)
    return cols.T.astype(jnp.bfloat16)


def kernel(x, c, positions, w_ada, b_ada, w_in, attn_sinks, g_a, g_b, w_o,
           ln1_g, ln1_b, w_gate, w_up, w_down, ln2_g, ln2_b):
    bsz, s, d = x.shape
    depth = w_ada.shape[0]
    alpha = (2.0 * depth) ** 0.25
    bf = jnp.bfloat16

    mod = _adaln_mod(c, w_ada, b_ada)
    inv = (ROPE_THETA ** (-jnp.arange(HALF, dtype=jnp.float32) / HALF)).reshape(HALF, 1)
    pos3 = positions.reshape(bsz, 1, s)

    for l in range(depth):
        m = mod[l].reshape(bsz, 1, 6 * d)
        sh1, sc1, gt1, sh2, sc2, gt2 = [m[:, :, i * d:(i + 1) * d] for i in range(6)]
        qT, kN, vT, wT = _inproj(x, sc1, sh1, pos3, inv, _regroup_w_in(w_in[l]), tm=512)
        y = _attention(attn_sinks[l], qT, kN, vT, wT,
                       jnp.broadcast_to(g_a[l][:, None], (A_Q, LANES)),
                       jnp.broadcast_to(g_b[l][:, None], (B_Q, LANES)))
        x = _post_attention(y, x, gt1, w_o[l].astype(bf), ln1_g[l].reshape(1, d), ln1_b[l].reshape(1, d),
                            sc2, sh2, gt2, w_gate[l].astype(bf), w_up[l].astype(bf), w_down[l].astype(bf),
                            ln2_g[l].reshape(1, d), ln2_b[l].reshape(1, d), alpha=alpha, tm=512)
    return x
```

```python
import functools

import jax
import jax.numpy as jnp
import numpy as np
from jax import lax
from jax.experimental import pallas as pl
from jax.experimental.pallas import tpu as pltpu

HEAD_DIM = 64
HALF = HEAD_DIM // 2
A_HEADS = 8
A_KV_HEADS = 2
B_HEADS = 8
B_KV_HEADS = 2
IDX_HEADS = 8
IDX_DIM = 64
WINDOW = 128
BLOCK = 128
TOPK_MAX = 256
ROPE_THETA = 10000.0
LN_EPS = 1e-5
RMS_EPS = 1e-6

A_Q = A_HEADS * HEAD_DIM
A_KV = A_KV_HEADS * HEAD_DIM
B_Q = B_HEADS * HEAD_DIM
B_KV = B_KV_HEADS * HEAD_DIM
I_Q = IDX_HEADS * IDX_DIM
GROUP = A_HEADS // A_KV_HEADS

LANES = 128
KEY_CHUNK = 256
PIECE = 256
ONES_ROWS = 16
O_ROWS = HEAD_DIM + 8
NEG_BIG = -1e30
INT_MIN = -(2 ** 31)
VMEM_LIMIT = 56 * 1024 * 1024

Q_ROWS = A_Q + B_Q + I_Q
K_ROWS = 3 * LANES
V_ROWS = A_KV + B_KV
W_ROWS = 8
P_ROWS = Q_ROWS + K_ROWS + V_ROWS + 2 * W_ROWS


def _silu(x):
    return x * (1.0 / (1.0 + jnp.exp(-x)))


def _layer_norm(z, g, b):
    mu = jnp.mean(z, axis=-1, keepdims=True)
    zc = z - mu
    var = jnp.mean(zc * zc, axis=-1, keepdims=True)
    return zc * lax.rsqrt(var + LN_EPS) * g + b


def _mod_kernel(c_ref, w_ref, b_ref, o_ref):
    c_act = _silu(c_ref[...]).astype(jnp.bfloat16)
    o_ref[...] = jnp.dot(c_act, w_ref[...].astype(jnp.bfloat16),
                         preferred_element_type=jnp.float32) + b_ref[...]


def _adaln_mod(c, w_ada, b_ada):
    depth, d, d6 = w_ada.shape
    bsz = c.shape[0]
    tn = 1024
    return pl.pallas_call(
        _mod_kernel,
        out_shape=jax.ShapeDtypeStruct((depth, bsz, d6), jnp.float32),
        grid=(depth, d6 // tn),
        in_specs=[
            pl.BlockSpec((bsz, d), lambda l, j: (0, 0)),
            pl.BlockSpec((None, d, tn), lambda l, j: (l, 0, j)),
            pl.BlockSpec((None, 1, tn), lambda l, j: (l, 0, j)),
        ],
        out_specs=pl.BlockSpec((None, bsz, tn), lambda l, j: (l, 0, j)),
        compiler_params=pltpu.CompilerParams(
            dimension_semantics=("arbitrary", "arbitrary"),
            vmem_limit_bytes=VMEM_LIMIT),
        name="adaln_mod",
    )(c, w_ada, b_ada.reshape(depth, 1, d6))


def _rope_head(x, r, cos, sin, scale=None):
    x1, x2 = x[r:r + HALF], x[r + HALF:r + HEAD_DIM]
    o1 = x1 * cos - x2 * sin
    o2 = x2 * cos + x1 * sin
    if scale is not None:
        o1, o2 = o1 * scale, o2 * scale
    return o1, o2


def _inproj_kernel(x_ref, sc_ref, sh_ref, pos_ref, inv_ref, w_ref,
                   qT_ref, kN_ref, vT_ref, wT_ref, *, w_scale):
    h = x_ref[...] * (1.0 + sc_ref[...]) + sh_ref[...]
    pT = lax.dot_general(w_ref[...], h.astype(jnp.bfloat16),
                         (((1,), (1,)), ((), ())),
                         preferred_element_type=jnp.float32)
    ang = inv_ref[...] * pos_ref[...].astype(jnp.float32)
    cos, sin = jnp.cos(ang), jnp.sin(ang)

    nblk = vT_ref.shape[0]
    for hd in range(Q_ROWS // HEAD_DIM):
        r = hd * HEAD_DIM
        scale = HEAD_DIM ** -0.5 if r < A_Q + B_Q else None
        o1, o2 = _rope_head(pT, r, cos, sin, scale)
        o1, o2 = o1.astype(qT_ref.dtype), o2.astype(qT_ref.dtype)
        for j in range(nblk):
            qT_ref[j, r:r + HALF, :] = o1[:, j * LANES:(j + 1) * LANES]
            qT_ref[j, r + HALF:r + HEAD_DIM, :] = o2[:, j * LANES:(j + 1) * LANES]

    k_rows = []
    for hd in range(K_ROWS // HEAD_DIM):
        k_rows.extend(_rope_head(pT, Q_ROWS + hd * HEAD_DIM, cos, sin))
    kN_ref[...] = jnp.concatenate(k_rows, axis=0).T.astype(kN_ref.dtype)

    v0 = Q_ROWS + K_ROWS
    v = pT[v0:v0 + V_ROWS].astype(vT_ref.dtype)
    w0 = v0 + V_ROWS
    w = pT[w0:w0 + W_ROWS] * w_scale
    for j in range(nblk):
        vT_ref[j] = v[:, j * LANES:(j + 1) * LANES]
        wT_ref[j] = w[:, j * LANES:(j + 1) * LANES]


def _inproj(x, sc, sh, pos3, inv, w_t, *, tm):
    bsz, s, d = x.shape
    kern = functools.partial(_inproj_kernel, w_scale=IDX_HEADS ** -0.5 * IDX_DIM ** -0.5)
    return pl.pallas_call(
        kern,
        out_shape=(
            jax.ShapeDtypeStruct((bsz, s // LANES, Q_ROWS, LANES), jnp.bfloat16),
            jax.ShapeDtypeStruct((bsz, s, K_ROWS), jnp.bfloat16),
            jax.ShapeDtypeStruct((bsz, s // LANES, V_ROWS, LANES), jnp.bfloat16),
            jax.ShapeDtypeStruct((bsz, s // LANES, W_ROWS, LANES), jnp.float32),
        ),
        grid=(bsz, s // tm),
        in_specs=[
            pl.BlockSpec((None, tm, d), lambda b, i: (b, i, 0)),
            pl.BlockSpec((None, 1, d), lambda b, i: (b, 0, 0)),
            pl.BlockSpec((None, 1, d), lambda b, i: (b, 0, 0)),
            pl.BlockSpec((None, 1, tm), lambda b, i: (b, 0, i)),
            pl.BlockSpec((HALF, 1), lambda b, i: (0, 0)),
            pl.BlockSpec((P_ROWS, d), lambda b, i: (0, 0)),
        ],
        out_specs=(
            pl.BlockSpec((None, tm // LANES, Q_ROWS, LANES), lambda b, i: (b, i, 0, 0)),
            pl.BlockSpec((None, tm, K_ROWS), lambda b, i: (b, i, 0)),
            pl.BlockSpec((None, tm // LANES, V_ROWS, LANES), lambda b, i: (b, i, 0, 0)),
            pl.BlockSpec((None, tm // LANES, W_ROWS, LANES), lambda b, i: (b, i, 0, 0)),
        ),
        compiler_params=pltpu.CompilerParams(
            dimension_semantics=("arbitrary", "arbitrary"),
            vmem_limit_bytes=VMEM_LIMIT),
        name="in_proj",
    )(x, sc, sh, pos3, inv, w_t)


def _ordered_to_f32(c):
    bits = jnp.where(c >= 0, c, c ^ jnp.int32(0x7FFFFFFF))
    return lax.bitcast_convert_type(bits, jnp.float32)


def _head_rhs(qT, row0):
    zeros = jnp.zeros((HEAD_DIM, LANES), qT.dtype)
    cols = []
    for h in range(A_HEADS):
        blk = qT[row0 + h * HEAD_DIM:row0 + (h + 1) * HEAD_DIM, :]
        g = h // GROUP
        cols.append(jnp.concatenate([blk, zeros] if g == 0 else [zeros, blk], axis=0))
    return jnp.concatenate(cols, axis=1)


def _attn_kernel(sink_ref, qT_ref, kN_ref, vT_ref, wT_ref, ga_ref, gb_ref,
                 y_ref, score_ref, thr_ref, jt_ref, m_ref, o_ref, yT_ref,
                 sa_ref, p_ref, rhs_ref, sq_ref,
                 *, seq, topk):
    n = pl.program_id(1)
    q0 = n * BLOCK
    nk = (n + 2) // 2
    qidx = q0 + lax.broadcasted_iota(jnp.int32, (KEY_CHUNK, LANES), 1)
    row_iota = lax.broadcasted_iota(jnp.int32, (KEY_CHUNK, LANES), 0)
    nh = B_HEADS

    def identity():
        return (lax.broadcasted_iota(jnp.int32, (LANES, LANES), 0)
                == lax.broadcasted_iota(jnp.int32, (LANES, LANES), 1)).astype(jnp.bfloat16)

    def mask_rhs(row0):
        return jnp.concatenate([_head_rhs(qT_ref[...], row0),
                                jnp.concatenate([identity()] * B_HEADS, axis=1)], axis=0)

    ones_rows = jnp.ones((ONES_ROWS, KEY_CHUNK), jnp.bfloat16)

    def finish_group(r0, width, g_ref):
        blk = yT_ref[r0:r0 + width, :]
        ms = jnp.sum(blk * blk, axis=0, keepdims=True) * (1.0 / width)
        y_t = (blk * lax.rsqrt(ms + RMS_EPS) * g_ref[...]).astype(jnp.bfloat16)
        y = lax.dot_general(identity(), y_t, (((1,), (1,)), ((), ())), preferred_element_type=jnp.float32)
        y_ref[:, r0:r0 + width] = y.astype(y_ref.dtype)


    def softmax_heads(s_all, m_of):
        p_heads, maxes = [], []
        for h in range(nh):
            sl = slice(h * LANES, (h + 1) * LANES)
            s_h = s_all[:, sl]
            m_old = m_of(h, sl)
            m_new = jnp.maximum(m_old, jnp.max(s_h, axis=0, keepdims=True))
            p_heads.append(jnp.exp(s_h - m_new).astype(jnp.bfloat16))
            maxes.append((m_old, m_new))
        return p_heads, maxes

    rhs_a = mask_rhs(0)
    wblk = jnp.maximum(n - 1, 0)
    w0 = pl.multiple_of(wblk * BLOCK, BLOCK)
    delta = qidx - (w0 + row_iota)
    band = (delta >= 0) & (delta < WINDOW)
    bias = jnp.where(band, 0.0, NEG_BIG).astype(jnp.bfloat16)
    lhs = jnp.concatenate([kN_ref[pl.ds(w0, 2 * BLOCK), 0:LANES], bias], axis=1)
    s_all = jnp.dot(lhs, rhs_a, preferred_element_type=jnp.float32)

    for h in range(IDX_HEADS):
        rhs_ref[0:IDX_DIM, h * LANES:(h + 1) * LANES] = (
            qT_ref[A_Q + B_Q + h * IDX_DIM:A_Q + B_Q + (h + 1) * IDX_DIM, :])
    sa_ref[...] = jnp.dot(kN_ref[0:KEY_CHUNK, 2 * LANES:2 * LANES + IDX_DIM], rhs_ref[0:IDX_DIM, :],
                          preferred_element_type=jnp.float32)

    p_heads, ms = softmax_heads(
        s_all, lambda h, sl: jnp.full((1, LANES), sink_ref[h], jnp.float32))
    for g in range(A_KV_HEADS):
        pT = jnp.concatenate(p_heads[g * GROUP:(g + 1) * GROUP], axis=1)
        rows = slice(g * HEAD_DIM, (g + 1) * HEAD_DIM)
        vg = jnp.concatenate([vT_ref[wblk, rows, :], vT_ref[wblk + 1, rows, :]], axis=1)
        vg = jnp.concatenate([vg, ones_rows], axis=0)
        og = jnp.dot(vg, pT, preferred_element_type=jnp.float32)
        for j in range(GROUP):
            h = g * GROUP + j
            sink, m = ms[h]
            csl = slice(j * LANES, (j + 1) * LANES)
            l = og[HEAD_DIM:HEAD_DIM + 1, csl] + jnp.exp(sink - m)
            yT_ref[h * HEAD_DIM:(h + 1) * HEAD_DIM, :] = og[0:HEAD_DIM, csl] * (1.0 / l)

    max_nk = (seq // BLOCK + 1) // 2
    n_piece = B_HEADS * LANES // PIECE
    heads_per_piece = PIECE // LANES

    def score_stream(nkk):
        def run():
            for c in range(nkk):
                rows = slice(c * KEY_CHUNK, (c + 1) * KEY_CHUNK)
                sc = jnp.zeros((KEY_CHUNK, LANES), jnp.float32)
                for j in range(n_piece):
                    cs = slice(j * PIECE, (j + 1) * PIECE)
                    if c == 0:
                        rel = sa_ref[:, cs]
                    else:
                        rel = jnp.dot(kN_ref[rows, 2 * LANES:2 * LANES + IDX_DIM], rhs_ref[0:IDX_DIM, cs],
                                      preferred_element_type=jnp.float32)
                    for hh in range(heads_per_piece):
                        h = j * heads_per_piece + hh
                        sc = sc + jnp.maximum(rel[:, hh * LANES:(hh + 1) * LANES], 0.0) * wT_ref[h:h + 1, :]
                kidx = c * KEY_CHUNK + row_iota
                score_ref[rows, :] = jnp.where(kidx <= qidx, sc, -jnp.inf)
        return run

    def fold_rows(x):
        return x.reshape(KEY_CHUNK // 8, 8, LANES).sum(axis=0)

    def count(pred_fn):
        def body(c, acc):
            r0 = pl.multiple_of(c * KEY_CHUNK, KEY_CHUNK)
            blk = score_ref[pl.ds(r0, KEY_CHUNK), :]
            return acc + fold_rows(pred_fn(blk, r0).astype(jnp.int32))
        acc = lax.fori_loop(0, nk, body, jnp.zeros((8, LANES), jnp.int32))
        return acc.sum(axis=0, keepdims=True)

    def masked_min(pred_fn):
        def body(c, acc):
            r0 = pl.multiple_of(c * KEY_CHUNK, KEY_CHUNK)
            blk = score_ref[pl.ds(r0, KEY_CHUNK), :]
            v = jnp.where(pred_fn(blk, r0), blk, jnp.inf)
            return jnp.minimum(acc, v.reshape(KEY_CHUNK // 8, 8, LANES).min(axis=0))
        acc = lax.fori_loop(0, nk, body, jnp.full((8, LANES), jnp.inf, jnp.float32))
        return acc.min(axis=0, keepdims=True)

    def search_static(nkk):
        def count_ge(cf):
            acc = jnp.zeros((8, LANES), jnp.int32)
            for c in range(nkk):
                blk = score_ref[c * KEY_CHUNK:(c + 1) * KEY_CHUNK, :]
                acc = acc + fold_rows((blk >= cf).astype(jnp.int32))
            return acc.sum(axis=0, keepdims=True)

        def run():
            c0 = count_ge(0.0)
            nonneg = c0 >= topk
            st0 = (jnp.where(nonneg, jnp.int32(0), jnp.int32(INT_MIN)),
                   jnp.where(nonneg, c0, jnp.int32(nkk * KEY_CHUNK)))

            def bit_body(i, st):
                pfx, c_pfx = st
                cand = pfx + (jnp.int32(1) << (30 - i))
                cnt = count_ge(_ordered_to_f32(cand))
                take = cnt >= topk
                return jnp.where(take, cand, pfx), jnp.where(take, cnt, c_pfx)

            pfx, c_pfx = lax.fori_loop(0, 31, bit_body, st0)
            return _ordered_to_f32(pfx), c_pfx

        return run

    def score_and_search(nkk):
        def run():
            score_stream(nkk)()
            if nkk == 1:
                return (jnp.full((1, LANES), -jnp.inf, jnp.float32), jnp.zeros((1, LANES), jnp.int32))
            return search_static(nkk)()
        return run

    lo, c_lo = lax.switch(nk - 1, [score_and_search(j) for j in range(1, max_nk + 1)])
    thr_ref[...] = lo
    jt_ref[...] = jnp.full((1, LANES), jnp.where(n >= 2, seq, -1), jnp.int32)
    unresolved = jnp.max(jnp.where(c_lo > topk, 1, 0), axis=1, keepdims=True)[0, 0] > 0

    rhs_ref[...] = mask_rhs(A_Q)

    @pl.when(unresolved)
    def _resolve_ties():
        def count_gt(bv):
            return count(lambda blk, r0: blk > bv)

        b0 = masked_min(lambda blk, r0: blk >= lo)

        def w_cond(st):
            return jnp.max(st[1], axis=1, keepdims=True)[0, 0] >= topk

        def w_body(st):
            bv, cgt = st
            nxt = masked_min(lambda blk, r0: blk > bv)
            bn = jnp.where(cgt >= topk, nxt, bv)
            return bn, count_gt(bn)

        bv, cgt = lax.while_loop(w_cond, w_body, (b0, count_gt(b0)))
        thr_ref[...] = bv
        need = (topk - cgt).astype(jnp.float32)

        tri = (lax.broadcasted_iota(jnp.int32, (KEY_CHUNK, KEY_CHUNK), 0)
               >= lax.broadcasted_iota(jnp.int32, (KEY_CHUNK, KEY_CHUNK), 1)).astype(jnp.bfloat16)

        def tie_body(c, st):
            seen, jt_f = st
            r0 = pl.multiple_of(c * KEY_CHUNK, KEY_CHUNK)
            tie = score_ref[pl.ds(r0, KEY_CHUNK), :] == bv
            upto = jnp.dot(tri, jnp.where(tie, 1.0, 0.0).astype(jnp.bfloat16),
                           preferred_element_type=jnp.float32) + seen
            kidx_f = (r0 + row_iota).astype(jnp.float32)
            kept = jnp.where(tie, jnp.where(upto <= need, kidx_f, -1.0), -1.0)
            return upto[KEY_CHUNK - 1:KEY_CHUNK, :], jnp.maximum(jt_f, jnp.max(kept, axis=0, keepdims=True))

        _, jt_f = lax.fori_loop(0, nk, tie_body, (jnp.zeros((1, LANES), jnp.float32),
                                                  jnp.full((1, LANES), -1.0, jnp.float32)))
        jt_ref[...] = jt_f.astype(jnp.int32)

    thr = thr_ref[...]
    jt = jt_ref[...]

    def qk_stream(nkk):
        def run():
            m_fin = [jnp.full((1, LANES), NEG_BIG, jnp.float32) for _ in range(nh)]
            for c in range(nkk):
                rows = slice(c * KEY_CHUNK, (c + 1) * KEY_CHUNK)
                sc = score_ref[rows, :]
                kidx = c * KEY_CHUNK + row_iota
                sel = (sc > thr) | ((sc == thr) & (kidx <= jt))
                bias = jnp.where(sel, 0.0, NEG_BIG).astype(jnp.bfloat16)
                lhs = jnp.concatenate([kN_ref[rows, LANES:2 * LANES], bias], axis=1)
                for j in range(n_piece):
                    cs = slice(j * PIECE, (j + 1) * PIECE)
                    s = jnp.dot(lhs, rhs_ref[:, cs], preferred_element_type=jnp.float32)
                    sq_ref[rows, cs] = s
                    for hh in range(heads_per_piece):
                        h = j * heads_per_piece + hh
                        m_fin[h] = jnp.maximum(
                            m_fin[h], jnp.max(s[:, hh * LANES:(hh + 1) * LANES], axis=0, keepdims=True))
            for h in range(nh):
                m_ref[:, h * LANES:(h + 1) * LANES] = m_fin[h]
        return run

    def softmax_chunk(c, carry):
        rows = pl.ds(pl.multiple_of(c * KEY_CHUNK, KEY_CHUNK), KEY_CHUNK)
        p_ref[rows, :] = jnp.exp(sq_ref[rows, :] - m_ref[...]).astype(p_ref.dtype)
        return carry

    def pv_stream(nkk):
        def run():
            keys = nkk * KEY_CHUNK
            ones = jnp.ones((ONES_ROWS, keys), jnp.bfloat16)
            for g in range(B_KV_HEADS):
                rows = slice(A_KV + g * HEAD_DIM, A_KV + (g + 1) * HEAD_DIM)
                csl = slice(g * GROUP * LANES, (g + 1) * GROUP * LANES)
                vg = jnp.concatenate([vT_ref[t, rows, :] for t in range(keys // LANES)], axis=1)
                vg = jnp.concatenate([vg, ones], axis=0)
                pv = jnp.dot(vg, p_ref[0:keys, csl], preferred_element_type=jnp.float32)
                o_ref[:, csl] = pv[0:O_ROWS]
        return run

    lax.switch(nk - 1, [qk_stream(j) for j in range(1, max_nk + 1)])
    lax.fori_loop(0, nk, softmax_chunk, 0)
    lax.switch(nk - 1, [pv_stream(j) for j in range(1, max_nk + 1)])

    ob = o_ref[0:HEAD_DIM, :] * (1.0 / o_ref[HEAD_DIM:HEAD_DIM + 1, :])
    for h in range(nh):
        yT_ref[A_Q + h * HEAD_DIM:A_Q + (h + 1) * HEAD_DIM, :] = ob[:, h * LANES:(h + 1) * LANES]

    finish_group(0, A_Q, ga_ref)
    finish_group(A_Q, B_Q, gb_ref)


def _attention(sinks, qT, kN, vT, wT, g_a, g_b):
    bsz, s, _ = kN.shape
    nb = s // BLOCK
    topk = min(TOPK_MAX, s // 4)
    kern = functools.partial(_attn_kernel, seq=s, topk=topk)
    return pl.pallas_call(
        kern,
        out_shape=jax.ShapeDtypeStruct((bsz, s, A_Q + B_Q), jnp.bfloat16),
        grid=(bsz, nb),
        in_specs=[
            pl.BlockSpec(memory_space=pltpu.SMEM),
            pl.BlockSpec((None, None, Q_ROWS, BLOCK), lambda b, n: (b, n, 0, 0)),
            pl.BlockSpec((None, s, K_ROWS), lambda b, n: (b, 0, 0)),
            pl.BlockSpec((None, nb, V_ROWS, LANES), lambda b, n: (b, 0, 0, 0)),
            pl.BlockSpec((None, None, W_ROWS, BLOCK), lambda b, n: (b, n, 0, 0)),
            pl.BlockSpec((A_Q, LANES), lambda b, n: (0, 0)),
            pl.BlockSpec((B_Q, LANES), lambda b, n: (0, 0)),
        ],
        out_specs=pl.BlockSpec((None, BLOCK, A_Q + B_Q), lambda b, n: (b, n, 0)),
        scratch_shapes=[
            pltpu.VMEM((s, LANES), jnp.float32),
            pltpu.VMEM((1, LANES), jnp.float32),
            pltpu.VMEM((1, LANES), jnp.int32),
            pltpu.VMEM((1, B_HEADS * LANES), jnp.float32),
            pltpu.VMEM((O_ROWS, B_HEADS * LANES), jnp.float32),
            pltpu.VMEM((A_Q + B_Q, LANES), jnp.float32),
            pltpu.VMEM((KEY_CHUNK, B_HEADS * LANES), jnp.float32),
            pltpu.VMEM((s, B_HEADS * LANES), jnp.bfloat16),
            pltpu.VMEM((2 * LANES, B_HEADS * LANES), jnp.bfloat16),
            pltpu.VMEM((s, B_HEADS * LANES), jnp.float32),
        ],
        compiler_params=pltpu.CompilerParams(
            dimension_semantics=("arbitrary", "arbitrary"),
            vmem_limit_bytes=VMEM_LIMIT),
        name="attention",
    )(sinks, qT, kN, vT, wT, g_a, g_b)


def _post_kernel(y_ref, x_ref, gt1_ref, wo_ref, g1_ref, b1_ref, sc_ref, sh_ref, gt2_ref,
                 wg_ref, wu_ref, wd_ref, g2_ref, b2_ref, o_ref, *, alpha):
    proj = jnp.dot(y_ref[...], wo_ref[...], preferred_element_type=jnp.float32)
    x1 = _layer_norm(alpha * x_ref[...] + (1.0 + gt1_ref[...]) * proj, g1_ref[...], b1_ref[...])
    h = (x1 * (1.0 + sc_ref[...]) + sh_ref[...]).astype(jnp.bfloat16)
    gate = jnp.dot(h, wg_ref[...], preferred_element_type=jnp.float32)
    up = jnp.dot(h, wu_ref[...], preferred_element_type=jnp.float32)
    act = (_silu(gate) * up).astype(jnp.bfloat16)
    y2 = jnp.dot(act, wd_ref[...], preferred_element_type=jnp.float32)
    o_ref[...] = _layer_norm(alpha * x1 + (1.0 + gt2_ref[...]) * y2, g2_ref[...], b2_ref[...])


def _post_attention(y, x, gt1, w_o, g1, b1, sc2, sh2, gt2, w_gate, w_up, w_down, g2, b2, *, alpha, tm):
    bsz, s, d = x.shape
    dy = y.shape[-1]
    dff = w_gate.shape[-1]
    resident = pl.Buffered(1)
    tile = lambda width: pl.BlockSpec((None, tm, width), lambda b_, i: (b_, i, 0))
    per_batch = pl.BlockSpec((None, 1, d), lambda b_, i: (b_, 0, 0))
    row = pl.BlockSpec((1, d), lambda b_, i: (0, 0))
    weight = lambda r, c: pl.BlockSpec((r, c), lambda b_, i: (0, 0), pipeline_mode=resident)
    return pl.pallas_call(
        functools.partial(_post_kernel, alpha=alpha),
        out_shape=jax.ShapeDtypeStruct((bsz, s, d), jnp.float32),
        grid=(bsz, s // tm),
        in_specs=[tile(dy), tile(d), per_batch, weight(dy, d), row, row,
                  per_batch, per_batch, per_batch,
                  weight(d, dff), weight(d, dff), weight(dff, d), row, row],
        out_specs=tile(d),
        compiler_params=pltpu.CompilerParams(
            dimension_semantics=("arbitrary", "arbitrary"),
            vmem_limit_bytes=VMEM_LIMIT),
        name="out_proj_ffn",
    )(y, x, gt1, w_o, g1, b1, sc2, sh2, gt2, w_gate, w_up, w_down, g2, b2)


def _regroup_w_in(w_in_l):
    d = w_in_l.shape[0]
    splits = np.cumsum([A_Q, A_KV, A_KV, B_Q, B_KV, B_KV, I_Q, IDX_DIM, IDX_HEADS])
    qa, ka, va, qb, kb, vb, qi, ki, wi = jnp.split(w_in_l, splits[:-1], axis=1)
    zk = jnp.zeros((d, K_ROWS - A_KV - B_KV - IDX_DIM), w_in_l.dtype)
    zw = jnp.zeros((d, 2 * W_ROWS - IDX_HEADS), w_in_l.dtype)
    cols = jnp.concatenate([qa, qb, qi, ka, kb, ki, zk, va, vb, wi, zw], axis=1)
    return cols.T.astype(jnp.bfloat16)


def kernel(x, c, positions, w_ada, b_ada, w_in, attn_sinks, g_a, g_b, w_o,
           ln1_g, ln1_b, w_gate, w_up, w_down, ln2_g, ln2_b):
    bsz, s, d = x.shape
    depth = w_ada.shape[0]
    alpha = (2.0 * depth) ** 0.25
    bf = jnp.bfloat16

    mod = _adaln_mod(c, w_ada, b_ada)
    inv = (ROPE_THETA ** (-jnp.arange(HALF, dtype=jnp.float32) / HALF)).reshape(HALF, 1)
    pos3 = positions.reshape(bsz, 1, s)

    for l in range(depth):
        m = mod[l].reshape(bsz, 1, 6 * d)
        sh1, sc1, gt1, sh2, sc2, gt2 = [m[:, :, i * d:(i + 1) * d] for i in range(6)]
        qT, kN, vT, wT = _inproj(x, sc1, sh1, pos3, inv, _regroup_w_in(w_in[l]), tm=1024)
        y = _attention(attn_sinks[l], qT, kN, vT, wT,
                       jnp.broadcast_to(g_a[l][:, None], (A_Q, LANES)),
                       jnp.broadcast_to(g_b[l][:, None], (B_Q, LANES)))
        x = _post_attention(y, x, gt1, w_o[l].astype(bf), ln1_g[l].reshape(1, d), ln1_b[l].reshape(1, d),
                            sc2, sh2, gt2, w_gate[l].astype(bf), w_up[l].astype(bf), w_down[l].astype(bf),
                            ln2_g[l].reshape(1, d), ln2_b[l].reshape(1, d), alpha=alpha, tm=512)
    return x
```

```python
import functools

import jax
import jax.numpy as jnp
import numpy as np
from jax import lax
from jax.experimental import pallas as pl
from jax.experimental.pallas import tpu as pltpu

HEAD_DIM = 64
HALF = HEAD_DIM // 2
A_HEADS = 8
A_KV_HEADS = 2
B_HEADS = 8
B_KV_HEADS = 2
IDX_HEADS = 8
IDX_DIM = 64
WINDOW = 128
BLOCK = 128
TOPK_MAX = 256
ROPE_THETA = 10000.0
LN_EPS = 1e-5
RMS_EPS = 1e-6

A_Q = A_HEADS * HEAD_DIM
A_KV = A_KV_HEADS * HEAD_DIM
B_Q = B_HEADS * HEAD_DIM
B_KV = B_KV_HEADS * HEAD_DIM
I_Q = IDX_HEADS * IDX_DIM
GROUP = A_HEADS // A_KV_HEADS

LANES = 128
KEY_CHUNK = 256
PIECE = 256
ONES_ROWS = 16
O_ROWS = HEAD_DIM + 8
NEG_BIG = -1e30
INT_MIN = -(2 ** 31)
VMEM_LIMIT = 56 * 1024 * 1024

Q_ROWS = A_Q + B_Q + I_Q
K_ROWS = 3 * LANES
V_ROWS = A_KV + B_KV
W_ROWS = 8
P_ROWS = Q_ROWS + K_ROWS + V_ROWS + 2 * W_ROWS


def _silu(x):
    return x * (1.0 / (1.0 + jnp.exp(-x)))


def _layer_norm(z, g, b):
    mu = jnp.mean(z, axis=-1, keepdims=True)
    zc = z - mu
    var = jnp.mean(zc * zc, axis=-1, keepdims=True)
    return zc * lax.rsqrt(var + LN_EPS) * g + b


def _mod_kernel(c_ref, w_ref, b_ref, o_ref):
    c_act = _silu(c_ref[...]).astype(jnp.bfloat16)
    o_ref[...] = jnp.dot(c_act, w_ref[...].astype(jnp.bfloat16),
                         preferred_element_type=jnp.float32) + b_ref[...]


def _adaln_mod(c, w_ada, b_ada):
    depth, d, d6 = w_ada.shape
    bsz = c.shape[0]
    tn = 1024
    return pl.pallas_call(
        _mod_kernel,
        out_shape=jax.ShapeDtypeStruct((depth, bsz, d6), jnp.float32),
        grid=(depth, d6 // tn),
        in_specs=[
            pl.BlockSpec((bsz, d), lambda l, j: (0, 0)),
            pl.BlockSpec((None, d, tn), lambda l, j: (l, 0, j)),
            pl.BlockSpec((None, 1, tn), lambda l, j: (l, 0, j)),
        ],
        out_specs=pl.BlockSpec((None, bsz, tn), lambda l, j: (l, 0, j)),
        compiler_params=pltpu.CompilerParams(
            dimension_semantics=("arbitrary", "arbitrary"),
            vmem_limit_bytes=VMEM_LIMIT),
        name="adaln_mod",
    )(c, w_ada, b_ada.reshape(depth, 1, d6))


def _rope_head(x, r, cos, sin, scale=None):
    x1, x2 = x[r:r + HALF], x[r + HALF:r + HEAD_DIM]
    o1 = x1 * cos - x2 * sin
    o2 = x2 * cos + x1 * sin
    if scale is not None:
        o1, o2 = o1 * scale, o2 * scale
    return o1, o2


def _inproj_kernel(x_ref, sc_ref, sh_ref, pos_ref, inv_ref, w_ref,
                   qT_ref, kN_ref, vT_ref, wT_ref, *, w_scale):
    h = x_ref[...] * (1.0 + sc_ref[...]) + sh_ref[...]
    pT = lax.dot_general(w_ref[...], h.astype(jnp.bfloat16),
                         (((1,), (1,)), ((), ())),
                         preferred_element_type=jnp.float32)
    ang = inv_ref[...] * pos_ref[...].astype(jnp.float32)
    cos, sin = jnp.cos(ang), jnp.sin(ang)

    nblk = vT_ref.shape[0]
    for hd in range(Q_ROWS // HEAD_DIM):
        r = hd * HEAD_DIM
        scale = HEAD_DIM ** -0.5 if r < A_Q + B_Q else None
        o1, o2 = _rope_head(pT, r, cos, sin, scale)
        o1, o2 = o1.astype(qT_ref.dtype), o2.astype(qT_ref.dtype)
        for j in range(nblk):
            qT_ref[j, r:r + HALF, :] = o1[:, j * LANES:(j + 1) * LANES]
            qT_ref[j, r + HALF:r + HEAD_DIM, :] = o2[:, j * LANES:(j + 1) * LANES]

    k_rows = []
    for hd in range(K_ROWS // HEAD_DIM):
        k_rows.extend(_rope_head(pT, Q_ROWS + hd * HEAD_DIM, cos, sin))
    kN_ref[...] = jnp.concatenate(k_rows, axis=0).T.astype(kN_ref.dtype)

    v0 = Q_ROWS + K_ROWS
    v = pT[v0:v0 + V_ROWS].astype(vT_ref.dtype)
    w0 = v0 + V_ROWS
    w = pT[w0:w0 + W_ROWS] * w_scale
    for j in range(nblk):
        vT_ref[j] = v[:, j * LANES:(j + 1) * LANES]
        wT_ref[j] = w[:, j * LANES:(j + 1) * LANES]


def _inproj(x, sc, sh, pos3, inv, w_t, *, tm):
    bsz, s, d = x.shape
    kern = functools.partial(_inproj_kernel, w_scale=IDX_HEADS ** -0.5 * IDX_DIM ** -0.5)
    return pl.pallas_call(
        kern,
        out_shape=(
            jax.ShapeDtypeStruct((bsz, s // LANES, Q_ROWS, LANES), jnp.bfloat16),
            jax.ShapeDtypeStruct((bsz, s, K_ROWS), jnp.bfloat16),
            jax.ShapeDtypeStruct((bsz, s // LANES, V_ROWS, LANES), jnp.bfloat16),
            jax.ShapeDtypeStruct((bsz, s // LANES, W_ROWS, LANES), jnp.float32),
        ),
        grid=(bsz, s // tm),
        in_specs=[
            pl.BlockSpec((None, tm, d), lambda b, i: (b, i, 0)),
            pl.BlockSpec((None, 1, d), lambda b, i: (b, 0, 0)),
            pl.BlockSpec((None, 1, d), lambda b, i: (b, 0, 0)),
            pl.BlockSpec((None, 1, tm), lambda b, i: (b, 0, i)),
            pl.BlockSpec((HALF, 1), lambda b, i: (0, 0)),
            pl.BlockSpec((P_ROWS, d), lambda b, i: (0, 0)),
        ],
        out_specs=(
            pl.BlockSpec((None, tm // LANES, Q_ROWS, LANES), lambda b, i: (b, i, 0, 0)),
            pl.BlockSpec((None, tm, K_ROWS), lambda b, i: (b, i, 0)),
            pl.BlockSpec((None, tm // LANES, V_ROWS, LANES), lambda b, i: (b, i, 0, 0)),
            pl.BlockSpec((None, tm // LANES, W_ROWS, LANES), lambda b, i: (b, i, 0, 0)),
        ),
        compiler_params=pltpu.CompilerParams(
            dimension_semantics=("arbitrary", "arbitrary"),
            vmem_limit_bytes=VMEM_LIMIT),
        name="in_proj",
    )(x, sc, sh, pos3, inv, w_t)


def _ordered_to_f32(c):
    bits = jnp.where(c >= 0, c, c ^ jnp.int32(0x7FFFFFFF))
    return lax.bitcast_convert_type(bits, jnp.float32)


def _head_rhs(qT, row0):
    zeros = jnp.zeros((HEAD_DIM, LANES), qT.dtype)
    cols = []
    for h in range(A_HEADS):
        blk = qT[row0 + h * HEAD_DIM:row0 + (h + 1) * HEAD_DIM, :]
        g = h // GROUP
        cols.append(jnp.concatenate([blk, zeros] if g == 0 else [zeros, blk], axis=0))
    return jnp.concatenate(cols, axis=1)


def _attn_kernel(sink_ref, qT_ref, kN_ref, vT_ref, wT_ref, ga_ref, gb_ref,
                 y_ref, score_ref, thr_ref, jt_ref, m_ref, o_ref, yT_ref,
                 sa_ref, p_ref, rhs_ref, sq_ref,
                 *, seq, topk):
    n = pl.program_id(1)
    q0 = n * BLOCK
    nk = (n + 2) // 2
    qidx = q0 + lax.broadcasted_iota(jnp.int32, (KEY_CHUNK, LANES), 1)
    row_iota = lax.broadcasted_iota(jnp.int32, (KEY_CHUNK, LANES), 0)
    nh = B_HEADS

    def identity():
        return (lax.broadcasted_iota(jnp.int32, (LANES, LANES), 0)
                == lax.broadcasted_iota(jnp.int32, (LANES, LANES), 1)).astype(jnp.bfloat16)

    def mask_rhs(row0):
        return jnp.concatenate([_head_rhs(qT_ref[...], row0),
                                jnp.concatenate([identity()] * B_HEADS, axis=1)], axis=0)

    ones_rows = jnp.ones((ONES_ROWS, KEY_CHUNK), jnp.bfloat16)

    def finish_group(r0, width, g_ref):
        blk = yT_ref[r0:r0 + width, :]
        ms = jnp.sum(blk * blk, axis=0, keepdims=True) * (1.0 / width)
        y_t = (blk * lax.rsqrt(ms + RMS_EPS) * g_ref[...]).astype(jnp.bfloat16)
        y = lax.dot_general(identity(), y_t, (((1,), (1,)), ((), ())), preferred_element_type=jnp.float32)
        y_ref[:, r0:r0 + width] = y.astype(y_ref.dtype)


    def softmax_heads(s_all, m_of):
        p_heads, maxes = [], []
        for h in range(nh):
            sl = slice(h * LANES, (h + 1) * LANES)
            s_h = s_all[:, sl]
            m_old = m_of(h, sl)
            m_new = jnp.maximum(m_old, jnp.max(s_h, axis=0, keepdims=True))
            p_heads.append(jnp.exp(s_h - m_new).astype(jnp.bfloat16))
            maxes.append((m_old, m_new))
        return p_heads, maxes

    rhs_a = mask_rhs(0)
    wblk = jnp.maximum(n - 1, 0)
    w0 = pl.multiple_of(wblk * BLOCK, BLOCK)
    delta = qidx - (w0 + row_iota)
    band = (delta >= 0) & (delta < WINDOW)
    bias = jnp.where(band, 0.0, NEG_BIG).astype(jnp.bfloat16)
    lhs = jnp.concatenate([kN_ref[pl.ds(w0, 2 * BLOCK), 0:LANES], bias], axis=1)
    s_all = jnp.dot(lhs, rhs_a, preferred_element_type=jnp.float32)

    for h in range(IDX_HEADS):
        rhs_ref[0:IDX_DIM, h * LANES:(h + 1) * LANES] = (
            qT_ref[A_Q + B_Q + h * IDX_DIM:A_Q + B_Q + (h + 1) * IDX_DIM, :])
    sa_ref[...] = jnp.dot(kN_ref[0:KEY_CHUNK, 2 * LANES:2 * LANES + IDX_DIM], rhs_ref[0:IDX_DIM, :],
                          preferred_element_type=jnp.float32)

    p_heads, ms = softmax_heads(
        s_all, lambda h, sl: jnp.full((1, LANES), sink_ref[h], jnp.float32))
    for g in range(A_KV_HEADS):
        pT = jnp.concatenate(p_heads[g * GROUP:(g + 1) * GROUP], axis=1)
        rows = slice(g * HEAD_DIM, (g + 1) * HEAD_DIM)
        vg = jnp.concatenate([vT_ref[wblk, rows, :], vT_ref[wblk + 1, rows, :]], axis=1)
        vg = jnp.concatenate([vg, ones_rows], axis=0)
        og = jnp.dot(vg, pT, preferred_element_type=jnp.float32)
        for j in range(GROUP):
            h = g * GROUP + j
            sink, m = ms[h]
            csl = slice(j * LANES, (j + 1) * LANES)
            l = og[HEAD_DIM:HEAD_DIM + 1, csl] + jnp.exp(sink - m)
            yT_ref[h * HEAD_DIM:(h + 1) * HEAD_DIM, :] = og[0:HEAD_DIM, csl] * (1.0 / l)

    max_nk = (seq // BLOCK + 1) // 2
    n_piece = B_HEADS * LANES // PIECE
    heads_per_piece = PIECE // LANES

    def score_stream(nkk):
        def run():
            for c in range(nkk):
                rows = slice(c * KEY_CHUNK, (c + 1) * KEY_CHUNK)
                sc = jnp.zeros((KEY_CHUNK, LANES), jnp.float32)
                for j in range(n_piece):
                    cs = slice(j * PIECE, (j + 1) * PIECE)
                    if c == 0:
                        rel = sa_ref[:, cs]
                    else:
                        rel = jnp.dot(kN_ref[rows, 2 * LANES:2 * LANES + IDX_DIM], rhs_ref[0:IDX_DIM, cs],
                                      preferred_element_type=jnp.float32)
                    for hh in range(heads_per_piece):
                        h = j * heads_per_piece + hh
                        sc = sc + jnp.maximum(rel[:, hh * LANES:(hh + 1) * LANES], 0.0) * wT_ref[h:h + 1, :]
                kidx = c * KEY_CHUNK + row_iota
                score_ref[rows, :] = jnp.where(kidx <= qidx, sc, -jnp.inf)
        return run

    def fold_rows(x):
        return x.reshape(KEY_CHUNK // 8, 8, LANES).sum(axis=0)

    def count(pred_fn):
        def body(c, acc):
            r0 = pl.multiple_of(c * KEY_CHUNK, KEY_CHUNK)
            blk = score_ref[pl.ds(r0, KEY_CHUNK), :]
            return acc + fold_rows(pred_fn(blk, r0).astype(jnp.int32))
        acc = lax.fori_loop(0, nk, body, jnp.zeros((8, LANES), jnp.int32))
        return acc.sum(axis=0, keepdims=True)

    def masked_min(pred_fn):
        def body(c, acc):
            r0 = pl.multiple_of(c * KEY_CHUNK, KEY_CHUNK)
            blk = score_ref[pl.ds(r0, KEY_CHUNK), :]
            v = jnp.where(pred_fn(blk, r0), blk, jnp.inf)
            return jnp.minimum(acc, v.reshape(KEY_CHUNK // 8, 8, LANES).min(axis=0))
        acc = lax.fori_loop(0, nk, body, jnp.full((8, LANES), jnp.inf, jnp.float32))
        return acc.min(axis=0, keepdims=True)

    def search_static(nkk):
        def count_ge(cf):
            acc = jnp.zeros((8, LANES), jnp.int32)
            for c in range(nkk):
                blk = score_ref[c * KEY_CHUNK:(c + 1) * KEY_CHUNK, :]
                acc = acc + fold_rows((blk >= cf).astype(jnp.int32))
            return acc.sum(axis=0, keepdims=True)

        def run():
            c0 = count_ge(0.0)
            nonneg = c0 >= topk
            pfx0 = jnp.where(nonneg, jnp.int32(0), jnp.int32(INT_MIN))
            st0 = (pfx0, jnp.where(nonneg, c0, jnp.int32(nkk * KEY_CHUNK)),
                   _ordered_to_f32(pfx0 + jnp.int32(1 << 30)))

            def bit_body(i, st):
                pfx, c_pfx, cf = st
                bit = jnp.int32(1) << (30 - i)
                cand = pfx + bit
                f_take = _ordered_to_f32(cand + (bit >> 1))
                f_keep = _ordered_to_f32(pfx + (bit >> 1))
                cnt = count_ge(cf)
                take = cnt >= topk
                return (jnp.where(take, cand, pfx), jnp.where(take, cnt, c_pfx),
                        jnp.where(take, f_take, f_keep))

            pfx, c_pfx, _ = lax.fori_loop(0, 31, bit_body, st0)
            return _ordered_to_f32(pfx), c_pfx

        return run

    def score_and_search(nkk):
        def run():
            score_stream(nkk)()
            if nkk == 1:
                return (jnp.full((1, LANES), -jnp.inf, jnp.float32), jnp.zeros((1, LANES), jnp.int32))
            return search_static(nkk)()
        return run

    lo, c_lo = lax.switch(nk - 1, [score_and_search(j) for j in range(1, max_nk + 1)])
    thr_ref[...] = lo
    jt_ref[...] = jnp.full((1, LANES), jnp.where(n >= 2, seq, -1), jnp.int32)
    unresolved = jnp.max(jnp.where(c_lo > topk, 1, 0), axis=1, keepdims=True)[0, 0] > 0

    rhs_ref[...] = mask_rhs(A_Q)

    @pl.when(unresolved)
    def _resolve_ties():
        def count_gt(bv):
            return count(lambda blk, r0: blk > bv)

        b0 = masked_min(lambda blk, r0: blk >= lo)

        def w_cond(st):
            return jnp.max(st[1], axis=1, keepdims=True)[0, 0] >= topk

        def w_body(st):
            bv, cgt = st
            nxt = masked_min(lambda blk, r0: blk > bv)
            bn = jnp.where(cgt >= topk, nxt, bv)
            return bn, count_gt(bn)

        bv, cgt = lax.while_loop(w_cond, w_body, (b0, count_gt(b0)))
        thr_ref[...] = bv
        need = (topk - cgt).astype(jnp.float32)

        tri = (lax.broadcasted_iota(jnp.int32, (KEY_CHUNK, KEY_CHUNK), 0)
               >= lax.broadcasted_iota(jnp.int32, (KEY_CHUNK, KEY_CHUNK), 1)).astype(jnp.bfloat16)

        def tie_body(c, st):
            seen, jt_f = st
            r0 = pl.multiple_of(c * KEY_CHUNK, KEY_CHUNK)
            tie = score_ref[pl.ds(r0, KEY_CHUNK), :] == bv
            upto = jnp.dot(tri, jnp.where(tie, 1.0, 0.0).astype(jnp.bfloat16),
                           preferred_element_type=jnp.float32) + seen
            kidx_f = (r0 + row_iota).astype(jnp.float32)
            kept = jnp.where(tie, jnp.where(upto <= need, kidx_f, -1.0), -1.0)
            return upto[KEY_CHUNK - 1:KEY_CHUNK, :], jnp.maximum(jt_f, jnp.max(kept, axis=0, keepdims=True))

        _, jt_f = lax.fori_loop(0, nk, tie_body, (jnp.zeros((1, LANES), jnp.float32),
                                                  jnp.full((1, LANES), -1.0, jnp.float32)))
        jt_ref[...] = jt_f.astype(jnp.int32)

    thr = thr_ref[...]
    jt = jt_ref[...]

    def qk_stream(nkk):
        def run():
            m_fin = [jnp.full((1, LANES), NEG_BIG, jnp.float32) for _ in range(nh)]
            for c in range(nkk):
                rows = slice(c * KEY_CHUNK, (c + 1) * KEY_CHUNK)
                sc = score_ref[rows, :]
                kidx = c * KEY_CHUNK + row_iota
                sel = (sc > thr) | ((sc == thr) & (kidx <= jt))
                bias = jnp.where(sel, 0.0, NEG_BIG).astype(jnp.bfloat16)
                lhs = jnp.concatenate([kN_ref[rows, LANES:2 * LANES], bias], axis=1)
                for j in range(n_piece):
                    cs = slice(j * PIECE, (j + 1) * PIECE)
                    s = jnp.dot(lhs, rhs_ref[:, cs], preferred_element_type=jnp.float32)
                    sq_ref[rows, cs] = s
                    for hh in range(heads_per_piece):
                        h = j * heads_per_piece + hh
                        m_fin[h] = jnp.maximum(
                            m_fin[h], jnp.max(s[:, hh * LANES:(hh + 1) * LANES], axis=0, keepdims=True))
            for h in range(nh):
                m_ref[:, h * LANES:(h + 1) * LANES] = m_fin[h]
        return run

    def softmax_chunk(c, carry):
        rows = pl.ds(pl.multiple_of(c * KEY_CHUNK, KEY_CHUNK), KEY_CHUNK)
        p_ref[rows, :] = jnp.exp(sq_ref[rows, :] - m_ref[...]).astype(p_ref.dtype)
        return carry

    def pv_stream(nkk):
        def run():
            keys = nkk * KEY_CHUNK
            ones = jnp.ones((ONES_ROWS, keys), jnp.bfloat16)
            for g in range(B_KV_HEADS):
                rows = slice(A_KV + g * HEAD_DIM, A_KV + (g + 1) * HEAD_DIM)
                csl = slice(g * GROUP * LANES, (g + 1) * GROUP * LANES)
                vg = jnp.concatenate([vT_ref[t, rows, :] for t in range(keys // LANES)], axis=1)
                vg = jnp.concatenate([vg, ones], axis=0)
                pv = jnp.dot(vg, p_ref[0:keys, csl], preferred_element_type=jnp.float32)
                o_ref[:, csl] = pv[0:O_ROWS]
        return run

    lax.switch(nk - 1, [qk_stream(j) for j in range(1, max_nk + 1)])
    lax.fori_loop(0, nk, softmax_chunk, 0)
    lax.switch(nk - 1, [pv_stream(j) for j in range(1, max_nk + 1)])

    ob = o_ref[0:HEAD_DIM, :] * (1.0 / o_ref[HEAD_DIM:HEAD_DIM + 1, :])
    for h in range(nh):
        yT_ref[A_Q + h * HEAD_DIM:A_Q + (h + 1) * HEAD_DIM, :] = ob[:, h * LANES:(h + 1) * LANES]

    finish_group(0, A_Q, ga_ref)
    finish_group(A_Q, B_Q, gb_ref)


def _attention(sinks, qT, kN, vT, wT, g_a, g_b):
    bsz, s, _ = kN.shape
    nb = s // BLOCK
    topk = min(TOPK_MAX, s // 4)
    kern = functools.partial(_attn_kernel, seq=s, topk=topk)
    return pl.pallas_call(
        kern,
        out_shape=jax.ShapeDtypeStruct((bsz, s, A_Q + B_Q), jnp.bfloat16),
        grid=(bsz, nb),
        in_specs=[
            pl.BlockSpec(memory_space=pltpu.SMEM),
            pl.BlockSpec((None, None, Q_ROWS, BLOCK), lambda b, n: (b, n, 0, 0)),
            pl.BlockSpec((None, s, K_ROWS), lambda b, n: (b, 0, 0)),
            pl.BlockSpec((None, nb, V_ROWS, LANES), lambda b, n: (b, 0, 0, 0)),
            pl.BlockSpec((None, None, W_ROWS, BLOCK), lambda b, n: (b, n, 0, 0)),
            pl.BlockSpec((A_Q, LANES), lambda b, n: (0, 0)),
            pl.BlockSpec((B_Q, LANES), lambda b, n: (0, 0)),
        ],
        out_specs=pl.BlockSpec((None, BLOCK, A_Q + B_Q), lambda b, n: (b, n, 0)),
        scratch_shapes=[
            pltpu.VMEM((s, LANES), jnp.float32),
            pltpu.VMEM((1, LANES), jnp.float32),
            pltpu.VMEM((1, LANES), jnp.int32),
            pltpu.VMEM((1, B_HEADS * LANES), jnp.float32),
            pltpu.VMEM((O_ROWS, B_HEADS * LANES), jnp.float32),
            pltpu.VMEM((A_Q + B_Q, LANES), jnp.float32),
            pltpu.VMEM((KEY_CHUNK, B_HEADS * LANES), jnp.float32),
            pltpu.VMEM((s, B_HEADS * LANES), jnp.bfloat16),
            pltpu.VMEM((2 * LANES, B_HEADS * LANES), jnp.bfloat16),
            pltpu.VMEM((s, B_HEADS * LANES), jnp.float32),
        ],
        compiler_params=pltpu.CompilerParams(
            dimension_semantics=("arbitrary", "arbitrary"),
            vmem_limit_bytes=VMEM_LIMIT),
        name="attention",
    )(sinks, qT, kN, vT, wT, g_a, g_b)


def _post_kernel(y_ref, x_ref, gt1_ref, wo_ref, g1_ref, b1_ref, sc_ref, sh_ref, gt2_ref,
                 wg_ref, wu_ref, wd_ref, g2_ref, b2_ref, o_ref, *, alpha):
    proj = jnp.dot(y_ref[...], wo_ref[...], preferred_element_type=jnp.float32)
    x1 = _layer_norm(alpha * x_ref[...] + (1.0 + gt1_ref[...]) * proj, g1_ref[...], b1_ref[...])
    h = (x1 * (1.0 + sc_ref[...]) + sh_ref[...]).astype(jnp.bfloat16)
    gate = jnp.dot(h, wg_ref[...], preferred_element_type=jnp.float32)
    up = jnp.dot(h, wu_ref[...], preferred_element_type=jnp.float32)
    act = (_silu(gate) * up).astype(jnp.bfloat16)
    y2 = jnp.dot(act, wd_ref[...], preferred_element_type=jnp.float32)
    o_ref[...] = _layer_norm(alpha * x1 + (1.0 + gt2_ref[...]) * y2, g2_ref[...], b2_ref[...])


def _post_attention(y, x, gt1, w_o, g1, b1, sc2, sh2, gt2, w_gate, w_up, w_down, g2, b2, *, alpha, tm):
    bsz, s, d = x.shape
    dy = y.shape[-1]
    dff = w_gate.shape[-1]
    resident = pl.Buffered(1)
    tile = lambda width: pl.BlockSpec((None, tm, width), lambda b_, i: (b_, i, 0))
    per_batch = pl.BlockSpec((None, 1, d), lambda b_, i: (b_, 0, 0))
    row = pl.BlockSpec((1, d), lambda b_, i: (0, 0))
    weight = lambda r, c: pl.BlockSpec((r, c), lambda b_, i: (0, 0), pipeline_mode=resident)
    return pl.pallas_call(
        functools.partial(_post_kernel, alpha=alpha),
        out_shape=jax.ShapeDtypeStruct((bsz, s, d), jnp.float32),
        grid=(bsz, s // tm),
        in_specs=[tile(dy), tile(d), per_batch, weight(dy, d), row, row,
                  per_batch, per_batch, per_batch,
                  weight(d, dff), weight(d, dff), weight(dff, d), row, row],
        out_specs=tile(d),
        compiler_params=pltpu.CompilerParams(
            dimension_semantics=("arbitrary", "arbitrary"),
            vmem_limit_bytes=VMEM_LIMIT),
        name="out_proj_ffn",
    )(y, x, gt1, w_o, g1, b1, sc2, sh2, gt2, w_gate, w_up, w_down, g2, b2)


def _regroup_w_in(w_in_l):
    d = w_in_l.shape[0]
    splits = np.cumsum([A_Q, A_KV, A_KV, B_Q, B_KV, B_KV, I_Q, IDX_DIM, IDX_HEADS])
    qa, ka, va, qb, kb, vb, qi, ki, wi = jnp.split(w_in_l, splits[:-1], axis=1)
    zk = jnp.zeros((d, K_ROWS - A_KV - B_KV - IDX_DIM), w_in_l.dtype)
    zw = jnp.zeros((d, 2 * W_ROWS - IDX_HEADS), w_in_l.dtype)
    cols = jnp.concatenate([qa, qb, qi, ka, kb, ki, zk, va, vb, wi, zw], axis=1)
    return cols.T.astype(jnp.bfloat16)


def kernel(x, c, positions, w_ada, b_ada, w_in, attn_sinks, g_a, g_b, w_o,
           ln1_g, ln1_b, w_gate, w_up, w_down, ln2_g, ln2_b):
    bsz, s, d = x.shape
    depth = w_ada.shape[0]
    alpha = (2.0 * depth) ** 0.25
    bf = jnp.bfloat16

    mod = _adaln_mod(c, w_ada, b_ada)
    inv = (ROPE_THETA ** (-jnp.arange(HALF, dtype=jnp.float32) / HALF)).reshape(HALF, 1)
    pos3 = positions.reshape(bsz, 1, s)

    for l in range(depth):
        m = mod[l].reshape(bsz, 1, 6 * d)
        sh1, sc1, gt1, sh2, sc2, gt2 = [m[:, :, i * d:(i + 1) * d] for i in range(6)]
        qT, kN, vT, wT = _inproj(x, sc1, sh1, pos3, inv, _regroup_w_in(w_in[l]), tm=1024)
        y = _attention(attn_sinks[l], qT, kN, vT, wT,
                       jnp.broadcast_to(g_a[l][:, None], (A_Q, LANES)),
                       jnp.broadcast_to(g_b[l][:, None], (B_Q, LANES)))
        x = _post_attention(y, x, gt1, w_o[l].astype(bf), ln1_g[l].reshape(1, d), ln1_b[l].reshape(1, d),
                            sc2, sh2, gt2, w_gate[l].astype(bf), w_up[l].astype(bf), w_down[l].astype(bf),
                            ln2_g[l].reshape(1, d), ln2_b[l].reshape(1, d), alpha=alpha, tm=512)
    return x
```

```python
import functools

import jax
import jax.numpy as jnp
import numpy as np
from jax import lax
from jax.experimental import pallas as pl
from jax.experimental.pallas import tpu as pltpu

HEAD_DIM = 64
HALF = HEAD_DIM // 2
A_HEADS = 8
A_KV_HEADS = 2
B_HEADS = 8
B_KV_HEADS = 2
IDX_HEADS = 8
IDX_DIM = 64
WINDOW = 128
BLOCK = 128
TOPK_MAX = 256
ROPE_THETA = 10000.0
LN_EPS = 1e-5
RMS_EPS = 1e-6

A_Q = A_HEADS * HEAD_DIM
A_KV = A_KV_HEADS * HEAD_DIM
B_Q = B_HEADS * HEAD_DIM
B_KV = B_KV_HEADS * HEAD_DIM
I_Q = IDX_HEADS * IDX_DIM
GROUP = A_HEADS // A_KV_HEADS

LANES = 128
KEY_CHUNK = 256
PIECE = 256
ONES_ROWS = 16
O_ROWS = HEAD_DIM + 8
NEG_BIG = -1e30
INT_MIN = -(2 ** 31)
VMEM_LIMIT = 56 * 1024 * 1024

Q_ROWS = A_Q + B_Q + I_Q
K_ROWS = 3 * LANES
V_ROWS = A_KV + B_KV
W_ROWS = 8
P_ROWS = Q_ROWS + K_ROWS + V_ROWS + 2 * W_ROWS


def _silu(x):
    return x * (1.0 / (1.0 + jnp.exp(-x)))


def _layer_norm(z, g, b):
    mu = jnp.mean(z, axis=-1, keepdims=True)
    zc = z - mu
    var = jnp.mean(zc * zc, axis=-1, keepdims=True)
    return zc * lax.rsqrt(var + LN_EPS) * g + b


def _mod_kernel(c_ref, w_ref, b_ref, o_ref):
    c_act = _silu(c_ref[...]).astype(jnp.bfloat16)
    o_ref[...] = jnp.dot(c_act, w_ref[...].astype(jnp.bfloat16),
                         preferred_element_type=jnp.float32) + b_ref[...]


def _adaln_mod(c, w_ada, b_ada):
    depth, d, d6 = w_ada.shape
    bsz = c.shape[0]
    tn = 1024
    return pl.pallas_call(
        _mod_kernel,
        out_shape=jax.ShapeDtypeStruct((depth, bsz, d6), jnp.float32),
        grid=(depth, d6 // tn),
        in_specs=[
            pl.BlockSpec((bsz, d), lambda l, j: (0, 0)),
            pl.BlockSpec((None, d, tn), lambda l, j: (l, 0, j)),
            pl.BlockSpec((None, 1, tn), lambda l, j: (l, 0, j)),
        ],
        out_specs=pl.BlockSpec((None, bsz, tn), lambda l, j: (l, 0, j)),
        compiler_params=pltpu.CompilerParams(
            dimension_semantics=("arbitrary", "arbitrary"),
            vmem_limit_bytes=VMEM_LIMIT),
        name="adaln_mod",
    )(c, w_ada, b_ada.reshape(depth, 1, d6))


def _rope_head(x, r, cos, sin, scale=None):
    x1, x2 = x[r:r + HALF], x[r + HALF:r + HEAD_DIM]
    o1 = x1 * cos - x2 * sin
    o2 = x2 * cos + x1 * sin
    if scale is not None:
        o1, o2 = o1 * scale, o2 * scale
    return o1, o2


def _inproj_kernel(x_ref, sc_ref, sh_ref, pos_ref, inv_ref, w_ref,
                   qT_ref, kN_ref, vT_ref, wT_ref, *, w_scale):
    h = x_ref[...] * (1.0 + sc_ref[...]) + sh_ref[...]
    pT = lax.dot_general(w_ref[...], h.astype(jnp.bfloat16),
                         (((1,), (1,)), ((), ())),
                         preferred_element_type=jnp.float32)
    ang = inv_ref[...] * pos_ref[...].astype(jnp.float32)
    cos, sin = jnp.cos(ang), jnp.sin(ang)

    nblk = vT_ref.shape[0]
    for hd in range(Q_ROWS // HEAD_DIM):
        r = hd * HEAD_DIM
        scale = HEAD_DIM ** -0.5 if r < A_Q + B_Q else None
        o1, o2 = _rope_head(pT, r, cos, sin, scale)
        o1, o2 = o1.astype(qT_ref.dtype), o2.astype(qT_ref.dtype)
        for j in range(nblk):
            qT_ref[j, r:r + HALF, :] = o1[:, j * LANES:(j + 1) * LANES]
            qT_ref[j, r + HALF:r + HEAD_DIM, :] = o2[:, j * LANES:(j + 1) * LANES]

    k_rows = []
    for hd in range(K_ROWS // HEAD_DIM):
        k_rows.extend(_rope_head(pT, Q_ROWS + hd * HEAD_DIM, cos, sin))
    kN_ref[...] = jnp.concatenate(k_rows, axis=0).T.astype(kN_ref.dtype)

    v0 = Q_ROWS + K_ROWS
    v = pT[v0:v0 + V_ROWS].astype(vT_ref.dtype)
    w0 = v0 + V_ROWS
    w = pT[w0:w0 + W_ROWS] * w_scale
    for j in range(nblk):
        vT_ref[j] = v[:, j * LANES:(j + 1) * LANES]
        wT_ref[j] = w[:, j * LANES:(j + 1) * LANES]


def _inproj(x, sc, sh, pos3, inv, w_t, *, tm):
    bsz, s, d = x.shape
    kern = functools.partial(_inproj_kernel, w_scale=IDX_HEADS ** -0.5 * IDX_DIM ** -0.5)
    return pl.pallas_call(
        kern,
        out_shape=(
            jax.ShapeDtypeStruct((bsz, s // LANES, Q_ROWS, LANES), jnp.bfloat16),
            jax.ShapeDtypeStruct((bsz, s, K_ROWS), jnp.bfloat16),
            jax.ShapeDtypeStruct((bsz, s // LANES, V_ROWS, LANES), jnp.bfloat16),
            jax.ShapeDtypeStruct((bsz, s // LANES, W_ROWS, LANES), jnp.float32),
        ),
        grid=(bsz, s // tm),
        in_specs=[
            pl.BlockSpec((None, tm, d), lambda b, i: (b, i, 0)),
            pl.BlockSpec((None, 1, d), lambda b, i: (b, 0, 0)),
            pl.BlockSpec((None, 1, d), lambda b, i: (b, 0, 0)),
            pl.BlockSpec((None, 1, tm), lambda b, i: (b, 0, i)),
            pl.BlockSpec((HALF, 1), lambda b, i: (0, 0)),
            pl.BlockSpec((P_ROWS, d), lambda b, i: (0, 0)),
        ],
        out_specs=(
            pl.BlockSpec((None, tm // LANES, Q_ROWS, LANES), lambda b, i: (b, i, 0, 0)),
            pl.BlockSpec((None, tm, K_ROWS), lambda b, i: (b, i, 0)),
            pl.BlockSpec((None, tm // LANES, V_ROWS, LANES), lambda b, i: (b, i, 0, 0)),
            pl.BlockSpec((None, tm // LANES, W_ROWS, LANES), lambda b, i: (b, i, 0, 0)),
        ),
        compiler_params=pltpu.CompilerParams(
            dimension_semantics=("arbitrary", "arbitrary"),
            vmem_limit_bytes=VMEM_LIMIT),
        name="in_proj",
    )(x, sc, sh, pos3, inv, w_t)


def _ordered_to_f32(c):
    bits = jnp.where(c >= 0, c, c ^ jnp.int32(0x7FFFFFFF))
    return lax.bitcast_convert_type(bits, jnp.float32)


def _head_rhs(qT, row0):
    zeros = jnp.zeros((HEAD_DIM, LANES), qT.dtype)
    cols = []
    for h in range(A_HEADS):
        blk = qT[row0 + h * HEAD_DIM:row0 + (h + 1) * HEAD_DIM, :]
        g = h // GROUP
        cols.append(jnp.concatenate([blk, zeros] if g == 0 else [zeros, blk], axis=0))
    return jnp.concatenate(cols, axis=1)


def _attn_kernel(sink_ref, qT_ref, kN_ref, vT_ref, wT_ref, ga_ref, gb_ref,
                 y_ref, score_ref, thr_ref, jt_ref, m_ref, o_ref, yT_ref,
                 sa_ref, p_ref, rhs_ref, sq_ref,
                 *, seq, topk):
    n = pl.program_id(1)
    q0 = n * BLOCK
    nk = (n + 2) // 2
    qidx = q0 + lax.broadcasted_iota(jnp.int32, (KEY_CHUNK, LANES), 1)
    row_iota = lax.broadcasted_iota(jnp.int32, (KEY_CHUNK, LANES), 0)
    nh = B_HEADS

    def identity():
        return (lax.broadcasted_iota(jnp.int32, (LANES, LANES), 0)
                == lax.broadcasted_iota(jnp.int32, (LANES, LANES), 1)).astype(jnp.bfloat16)

    def mask_rhs(row0):
        return jnp.concatenate([_head_rhs(qT_ref[...], row0),
                                jnp.concatenate([identity()] * B_HEADS, axis=1)], axis=0)

    ones_rows = jnp.ones((ONES_ROWS, KEY_CHUNK), jnp.bfloat16)

    def finish_group(r0, width, g_ref):
        blk = yT_ref[r0:r0 + width, :]
        ms = jnp.sum(blk * blk, axis=0, keepdims=True) * (1.0 / width)
        y_ref[r0:r0 + width, :] = (blk * lax.rsqrt(ms + RMS_EPS) * g_ref[...]).astype(y_ref.dtype)


    def softmax_heads(s_all, m_of):
        p_heads, maxes = [], []
        for h in range(nh):
            sl = slice(h * LANES, (h + 1) * LANES)
            s_h = s_all[:, sl]
            m_old = m_of(h, sl)
            m_new = jnp.maximum(m_old, jnp.max(s_h, axis=0, keepdims=True))
            p_heads.append(jnp.exp(s_h - m_new).astype(jnp.bfloat16))
            maxes.append((m_old, m_new))
        return p_heads, maxes

    rhs_a = mask_rhs(0)
    wblk = jnp.maximum(n - 1, 0)
    w0 = pl.multiple_of(wblk * BLOCK, BLOCK)
    delta = qidx - (w0 + row_iota)
    band = (delta >= 0) & (delta < WINDOW)
    bias = jnp.where(band, 0.0, NEG_BIG).astype(jnp.bfloat16)
    lhs = jnp.concatenate([kN_ref[pl.ds(w0, 2 * BLOCK), 0:LANES], bias], axis=1)
    s_all = jnp.dot(lhs, rhs_a, preferred_element_type=jnp.float32)

    for h in range(IDX_HEADS):
        rhs_ref[0:IDX_DIM, h * LANES:(h + 1) * LANES] = (
            qT_ref[A_Q + B_Q + h * IDX_DIM:A_Q + B_Q + (h + 1) * IDX_DIM, :])
    sa_ref[...] = jnp.dot(kN_ref[0:KEY_CHUNK, 2 * LANES:2 * LANES + IDX_DIM], rhs_ref[0:IDX_DIM, :],
                          preferred_element_type=jnp.float32)

    p_heads, ms = softmax_heads(
        s_all, lambda h, sl: jnp.full((1, LANES), sink_ref[h], jnp.float32))
    for g in range(A_KV_HEADS):
        pT = jnp.concatenate(p_heads[g * GROUP:(g + 1) * GROUP], axis=1)
        rows = slice(g * HEAD_DIM, (g + 1) * HEAD_DIM)
        vg = jnp.concatenate([vT_ref[wblk, rows, :], vT_ref[wblk + 1, rows, :]], axis=1)
        vg = jnp.concatenate([vg, ones_rows], axis=0)
        og = jnp.dot(vg, pT, preferred_element_type=jnp.float32)
        for j in range(GROUP):
            h = g * GROUP + j
            sink, m = ms[h]
            csl = slice(j * LANES, (j + 1) * LANES)
            l = og[HEAD_DIM:HEAD_DIM + 1, csl] + jnp.exp(sink - m)
            yT_ref[h * HEAD_DIM:(h + 1) * HEAD_DIM, :] = og[0:HEAD_DIM, csl] * (1.0 / l)

    max_nk = (seq // BLOCK + 1) // 2
    n_piece = B_HEADS * LANES // PIECE
    heads_per_piece = PIECE // LANES

    def score_stream(nkk):
        def run():
            for c in range(nkk):
                rows = slice(c * KEY_CHUNK, (c + 1) * KEY_CHUNK)
                sc = jnp.zeros((KEY_CHUNK, LANES), jnp.float32)
                for j in range(n_piece):
                    cs = slice(j * PIECE, (j + 1) * PIECE)
                    if c == 0:
                        rel = sa_ref[:, cs]
                    else:
                        rel = jnp.dot(kN_ref[rows, 2 * LANES:2 * LANES + IDX_DIM], rhs_ref[0:IDX_DIM, cs],
                                      preferred_element_type=jnp.float32)
                    for hh in range(heads_per_piece):
                        h = j * heads_per_piece + hh
                        sc = sc + jnp.maximum(rel[:, hh * LANES:(hh + 1) * LANES], 0.0) * wT_ref[h:h + 1, :]
                kidx = c * KEY_CHUNK + row_iota
                score_ref[rows, :] = jnp.where(kidx <= qidx, sc, -jnp.inf)
        return run

    def fold_rows(x):
        return x.reshape(KEY_CHUNK // 8, 8, LANES).sum(axis=0)

    def count(pred_fn):
        def body(c, acc):
            r0 = pl.multiple_of(c * KEY_CHUNK, KEY_CHUNK)
            blk = score_ref[pl.ds(r0, KEY_CHUNK), :]
            return acc + fold_rows(pred_fn(blk, r0).astype(jnp.int32))
        acc = lax.fori_loop(0, nk, body, jnp.zeros((8, LANES), jnp.int32))
        return acc.sum(axis=0, keepdims=True)

    def masked_min(pred_fn):
        def body(c, acc):
            r0 = pl.multiple_of(c * KEY_CHUNK, KEY_CHUNK)
            blk = score_ref[pl.ds(r0, KEY_CHUNK), :]
            v = jnp.where(pred_fn(blk, r0), blk, jnp.inf)
            return jnp.minimum(acc, v.reshape(KEY_CHUNK // 8, 8, LANES).min(axis=0))
        acc = lax.fori_loop(0, nk, body, jnp.full((8, LANES), jnp.inf, jnp.float32))
        return acc.min(axis=0, keepdims=True)

    def search_static(nkk):
        def count_ge(cf):
            acc = jnp.zeros((8, LANES), jnp.int32)
            for c in range(nkk):
                blk = score_ref[c * KEY_CHUNK:(c + 1) * KEY_CHUNK, :]
                acc = acc + fold_rows((blk >= cf).astype(jnp.int32))
            return acc.sum(axis=0, keepdims=True)

        def run():
            c0 = count_ge(0.0)
            nonneg = c0 >= topk
            pfx0 = jnp.where(nonneg, jnp.int32(0), jnp.int32(INT_MIN))
            st0 = (pfx0, jnp.where(nonneg, c0, jnp.int32(nkk * KEY_CHUNK)),
                   _ordered_to_f32(pfx0 + jnp.int32(1 << 30)))

            def bit_body(i, st):
                pfx, c_pfx, cf = st
                bit = jnp.int32(1) << (30 - i)
                cand = pfx + bit
                f_take = _ordered_to_f32(cand + (bit >> 1))
                f_keep = _ordered_to_f32(pfx + (bit >> 1))
                cnt = count_ge(cf)
                take = cnt >= topk
                return (jnp.where(take, cand, pfx), jnp.where(take, cnt, c_pfx),
                        jnp.where(take, f_take, f_keep))

            pfx, c_pfx, _ = lax.fori_loop(0, 31, bit_body, st0)
            return _ordered_to_f32(pfx), c_pfx

        return run

    def score_and_search(nkk):
        def run():
            score_stream(nkk)()
            if nkk == 1:
                return (jnp.full((1, LANES), -jnp.inf, jnp.float32), jnp.zeros((1, LANES), jnp.int32))
            return search_static(nkk)()
        return run

    lo, c_lo = lax.switch(nk - 1, [score_and_search(j) for j in range(1, max_nk + 1)])
    thr_ref[...] = lo
    jt_ref[...] = jnp.full((1, LANES), jnp.where(n >= 2, seq, -1), jnp.int32)
    unresolved = jnp.max(jnp.where(c_lo > topk, 1, 0), axis=1, keepdims=True)[0, 0] > 0

    rhs_ref[...] = mask_rhs(A_Q)

    @pl.when(unresolved)
    def _resolve_ties():
        def count_gt(bv):
            return count(lambda blk, r0: blk > bv)

        b0 = masked_min(lambda blk, r0: blk >= lo)

        def w_cond(st):
            return jnp.max(st[1], axis=1, keepdims=True)[0, 0] >= topk

        def w_body(st):
            bv, cgt = st
            nxt = masked_min(lambda blk, r0: blk > bv)
            bn = jnp.where(cgt >= topk, nxt, bv)
            return bn, count_gt(bn)

        bv, cgt = lax.while_loop(w_cond, w_body, (b0, count_gt(b0)))
        thr_ref[...] = bv
        need = (topk - cgt).astype(jnp.float32)

        tri = (lax.broadcasted_iota(jnp.int32, (KEY_CHUNK, KEY_CHUNK), 0)
               >= lax.broadcasted_iota(jnp.int32, (KEY_CHUNK, KEY_CHUNK), 1)).astype(jnp.bfloat16)

        def tie_body(c, st):
            seen, jt_f = st
            r0 = pl.multiple_of(c * KEY_CHUNK, KEY_CHUNK)
            tie = score_ref[pl.ds(r0, KEY_CHUNK), :] == bv
            upto = jnp.dot(tri, jnp.where(tie, 1.0, 0.0).astype(jnp.bfloat16),
                           preferred_element_type=jnp.float32) + seen
            kidx_f = (r0 + row_iota).astype(jnp.float32)
            kept = jnp.where(tie, jnp.where(upto <= need, kidx_f, -1.0), -1.0)
            return upto[KEY_CHUNK - 1:KEY_CHUNK, :], jnp.maximum(jt_f, jnp.max(kept, axis=0, keepdims=True))

        _, jt_f = lax.fori_loop(0, nk, tie_body, (jnp.zeros((1, LANES), jnp.float32),
                                                  jnp.full((1, LANES), -1.0, jnp.float32)))
        jt_ref[...] = jt_f.astype(jnp.int32)

    thr = thr_ref[...]
    jt = jt_ref[...]

    def qk_stream(nkk):
        def run():
            m_fin = [jnp.full((1, LANES), NEG_BIG, jnp.float32) for _ in range(nh)]
            for c in range(nkk):
                rows = slice(c * KEY_CHUNK, (c + 1) * KEY_CHUNK)
                sc = score_ref[rows, :]
                kidx = c * KEY_CHUNK + row_iota
                sel = (sc > thr) | ((sc == thr) & (kidx <= jt))
                bias = jnp.where(sel, 0.0, NEG_BIG).astype(jnp.bfloat16)
                lhs = jnp.concatenate([kN_ref[rows, LANES:2 * LANES], bias], axis=1)
                for j in range(n_piece):
                    cs = slice(j * PIECE, (j + 1) * PIECE)
                    s = jnp.dot(lhs, rhs_ref[:, cs], preferred_element_type=jnp.float32)
                    sq_ref[rows, cs] = s
                    for hh in range(heads_per_piece):
                        h = j * heads_per_piece + hh
                        m_fin[h] = jnp.maximum(
                            m_fin[h], jnp.max(s[:, hh * LANES:(hh + 1) * LANES], axis=0, keepdims=True))
            for h in range(nh):
                m_ref[:, h * LANES:(h + 1) * LANES] = m_fin[h]
        return run

    def softmax_chunk(c, carry):
        rows = pl.ds(pl.multiple_of(c * KEY_CHUNK, KEY_CHUNK), KEY_CHUNK)
        p_ref[rows, :] = jnp.exp(sq_ref[rows, :] - m_ref[...]).astype(p_ref.dtype)
        return carry

    def pv_stream(nkk):
        def run():
            keys = nkk * KEY_CHUNK
            ones = jnp.ones((ONES_ROWS, keys), jnp.bfloat16)
            for g in range(B_KV_HEADS):
                rows = slice(A_KV + g * HEAD_DIM, A_KV + (g + 1) * HEAD_DIM)
                csl = slice(g * GROUP * LANES, (g + 1) * GROUP * LANES)
                vg = jnp.concatenate([vT_ref[t, rows, :] for t in range(keys // LANES)], axis=1)
                vg = jnp.concatenate([vg, ones], axis=0)
                pv = jnp.dot(vg, p_ref[0:keys, csl], preferred_element_type=jnp.float32)
                o_ref[:, csl] = pv[0:O_ROWS]
        return run

    lax.switch(nk - 1, [qk_stream(j) for j in range(1, max_nk + 1)])
    lax.fori_loop(0, nk, softmax_chunk, 0)
    lax.switch(nk - 1, [pv_stream(j) for j in range(1, max_nk + 1)])

    ob = o_ref[0:HEAD_DIM, :] * (1.0 / o_ref[HEAD_DIM:HEAD_DIM + 1, :])
    for h in range(nh):
        yT_ref[A_Q + h * HEAD_DIM:A_Q + (h + 1) * HEAD_DIM, :] = ob[:, h * LANES:(h + 1) * LANES]

    finish_group(0, A_Q, ga_ref)
    finish_group(A_Q, B_Q, gb_ref)


def _attention(sinks, qT, kN, vT, wT, g_a, g_b):
    bsz, s, _ = kN.shape
    nb = s // BLOCK
    topk = min(TOPK_MAX, s // 4)
    kern = functools.partial(_attn_kernel, seq=s, topk=topk)
    return pl.pallas_call(
        kern,
        out_shape=jax.ShapeDtypeStruct((bsz, A_Q + B_Q, s), jnp.bfloat16),
        grid=(bsz, nb),
        in_specs=[
            pl.BlockSpec(memory_space=pltpu.SMEM),
            pl.BlockSpec((None, None, Q_ROWS, BLOCK), lambda b, n: (b, n, 0, 0)),
            pl.BlockSpec((None, s, K_ROWS), lambda b, n: (b, 0, 0)),
            pl.BlockSpec((None, nb, V_ROWS, LANES), lambda b, n: (b, 0, 0, 0)),
            pl.BlockSpec((None, None, W_ROWS, BLOCK), lambda b, n: (b, n, 0, 0)),
            pl.BlockSpec((A_Q, LANES), lambda b, n: (0, 0)),
            pl.BlockSpec((B_Q, LANES), lambda b, n: (0, 0)),
        ],
        out_specs=pl.BlockSpec((None, A_Q + B_Q, BLOCK), lambda b, n: (b, 0, n)),
        scratch_shapes=[
            pltpu.VMEM((s, LANES), jnp.float32),
            pltpu.VMEM((1, LANES), jnp.float32),
            pltpu.VMEM((1, LANES), jnp.int32),
            pltpu.VMEM((1, B_HEADS * LANES), jnp.float32),
            pltpu.VMEM((O_ROWS, B_HEADS * LANES), jnp.float32),
            pltpu.VMEM((A_Q + B_Q, LANES), jnp.float32),
            pltpu.VMEM((KEY_CHUNK, B_HEADS * LANES), jnp.float32),
            pltpu.VMEM((s, B_HEADS * LANES), jnp.bfloat16),
            pltpu.VMEM((2 * LANES, B_HEADS * LANES), jnp.bfloat16),
            pltpu.VMEM((s, B_HEADS * LANES), jnp.float32),
        ],
        compiler_params=pltpu.CompilerParams(
            dimension_semantics=("arbitrary", "arbitrary"),
            vmem_limit_bytes=VMEM_LIMIT),
        name="attention",
    )(sinks, qT, kN, vT, wT, g_a, g_b)


def _post_kernel(y_ref, x_ref, gt1_ref, wo_ref, g1_ref, b1_ref, sc_ref, sh_ref, gt2_ref,
                 wg_ref, wu_ref, wd_ref, g2_ref, b2_ref, o_ref, *, alpha):
    proj = lax.dot_general(y_ref[...], wo_ref[...], (((0,), (0,)), ((), ())),
                           preferred_element_type=jnp.float32)
    x1 = _layer_norm(alpha * x_ref[...] + (1.0 + gt1_ref[...]) * proj, g1_ref[...], b1_ref[...])
    h = (x1 * (1.0 + sc_ref[...]) + sh_ref[...]).astype(jnp.bfloat16)
    gate = jnp.dot(h, wg_ref[...], preferred_element_type=jnp.float32)
    up = jnp.dot(h, wu_ref[...], preferred_element_type=jnp.float32)
    act = (_silu(gate) * up).astype(jnp.bfloat16)
    y2 = jnp.dot(act, wd_ref[...], preferred_element_type=jnp.float32)
    o_ref[...] = _layer_norm(alpha * x1 + (1.0 + gt2_ref[...]) * y2, g2_ref[...], b2_ref[...])


def _post_attention(y, x, gt1, w_o, g1, b1, sc2, sh2, gt2, w_gate, w_up, w_down, g2, b2, *, alpha, tm):
    bsz, s, d = x.shape
    dy = y.shape[1]
    dff = w_gate.shape[-1]
    resident = pl.Buffered(1)
    tile = lambda width: pl.BlockSpec((None, tm, width), lambda b_, i: (b_, i, 0))
    per_batch = pl.BlockSpec((None, 1, d), lambda b_, i: (b_, 0, 0))
    row = pl.BlockSpec((1, d), lambda b_, i: (0, 0))
    weight = lambda r, c: pl.BlockSpec((r, c), lambda b_, i: (0, 0), pipeline_mode=resident)
    return pl.pallas_call(
        functools.partial(_post_kernel, alpha=alpha),
        out_shape=jax.ShapeDtypeStruct((bsz, s, d), jnp.float32),
        grid=(bsz, s // tm),
        in_specs=[pl.BlockSpec((None, dy, tm), lambda b_, i: (b_, 0, i)), tile(d), per_batch, weight(dy, d), row, row,
                  per_batch, per_batch, per_batch,
                  weight(d, dff), weight(d, dff), weight(dff, d), row, row],
        out_specs=tile(d),
        compiler_params=pltpu.CompilerParams(
            dimension_semantics=("arbitrary", "arbitrary"),
            vmem_limit_bytes=VMEM_LIMIT),
        name="out_proj_ffn",
    )(y, x, gt1, w_o, g1, b1, sc2, sh2, gt2, w_gate, w_up, w_down, g2, b2)


def _regroup_w_in(w_in_l):
    d = w_in_l.shape[0]
    splits = np.cumsum([A_Q, A_KV, A_KV, B_Q, B_KV, B_KV, I_Q, IDX_DIM, IDX_HEADS])
    qa, ka, va, qb, kb, vb, qi, ki, wi = jnp.split(w_in_l, splits[:-1], axis=1)
    zk = jnp.zeros((d, K_ROWS - A_KV - B_KV - IDX_DIM), w_in_l.dtype)
    zw = jnp.zeros((d, 2 * W_ROWS - IDX_HEADS), w_in_l.dtype)
    cols = jnp.concatenate([qa, qb, qi, ka, kb, ki, zk, va, vb, wi, zw], axis=1)
    return cols.T.astype(jnp.bfloat16)


def kernel(x, c, positions, w_ada, b_ada, w_in, attn_sinks, g_a, g_b, w_o,
           ln1_g, ln1_b, w_gate, w_up, w_down, ln2_g, ln2_b):
    bsz, s, d = x.shape
    depth = w_ada.shape[0]
    alpha = (2.0 * depth) ** 0.25
    bf = jnp.bfloat16

    mod = _adaln_mod(c, w_ada, b_ada)
    inv = (ROPE_THETA ** (-jnp.arange(HALF, dtype=jnp.float32) / HALF)).reshape(HALF, 1)
    pos3 = positions.reshape(bsz, 1, s)

    for l in range(depth):
        m = mod[l].reshape(bsz, 1, 6 * d)
        sh1, sc1, gt1, sh2, sc2, gt2 = [m[:, :, i * d:(i + 1) * d] for i in range(6)]
        qT, kN, vT, wT = _inproj(x, sc1, sh1, pos3, inv, _regroup_w_in(w_in[l]), tm=1024)
        y = _attention(attn_sinks[l], qT, kN, vT, wT,
                       jnp.broadcast_to(g_a[l][:, None], (A_Q, LANES)),
                       jnp.broadcast_to(g_b[l][:, None], (B_Q, LANES)))
        x = _post_attention(y, x, gt1, w_o[l].astype(bf), ln1_g[l].reshape(1, d), ln1_b[l].reshape(1, d),
                            sc2, sh2, gt2, w_gate[l].astype(bf), w_up[l].astype(bf), w_down[l].astype(bf),
                            ln2_g[l].reshape(1, d), ln2_b[l].reshape(1, d), alpha=alpha, tm=512)
    return x
```

```python
import functools

import jax
import jax.numpy as jnp
import numpy as np
from jax import lax
from jax.experimental import pallas as pl
from jax.experimental.pallas import tpu as pltpu

HEAD_DIM = 64
HALF = HEAD_DIM // 2
A_HEADS = 8
A_KV_HEADS = 2
B_HEADS = 8
B_KV_HEADS = 2
IDX_HEADS = 8
IDX_DIM = 64
WINDOW = 128
BLOCK = 128
TOPK_MAX = 256
ROPE_THETA = 10000.0
LN_EPS = 1e-5
RMS_EPS = 1e-6

A_Q = A_HEADS * HEAD_DIM
A_KV = A_KV_HEADS * HEAD_DIM
B_Q = B_HEADS * HEAD_DIM
B_KV = B_KV_HEADS * HEAD_DIM
I_Q = IDX_HEADS * IDX_DIM
GROUP = A_HEADS // A_KV_HEADS

LANES = 128
KEY_CHUNK = 256
PIECE = 256
ONES_ROWS = 16
O_ROWS = HEAD_DIM + 8
NEG_BIG = -1e30
INT_MIN = -(2 ** 31)
VMEM_LIMIT = 56 * 1024 * 1024

Q_ROWS = A_Q + B_Q + I_Q
K_ROWS = 3 * LANES
V_ROWS = A_KV + B_KV
W_ROWS = 8
P_ROWS = Q_ROWS + K_ROWS + V_ROWS + 2 * W_ROWS


def _silu(x):
    return x * (1.0 / (1.0 + jnp.exp(-x)))


def _layer_norm(z, g, b):
    mu = jnp.mean(z, axis=-1, keepdims=True)
    zc = z - mu
    var = jnp.mean(zc * zc, axis=-1, keepdims=True)
    return zc * lax.rsqrt(var + LN_EPS) * g + b


def _mod_kernel(c_ref, w_ref, b_ref, o_ref):
    c_act = _silu(c_ref[...]).astype(jnp.bfloat16)
    o_ref[...] = jnp.dot(c_act, w_ref[...].astype(jnp.bfloat16),
                         preferred_element_type=jnp.float32) + b_ref[...]


def _adaln_mod(c, w_ada, b_ada):
    depth, d, d6 = w_ada.shape
    bsz = c.shape[0]
    tn = 1024
    return pl.pallas_call(
        _mod_kernel,
        out_shape=jax.ShapeDtypeStruct((depth, bsz, d6), jnp.float32),
        grid=(depth, d6 // tn),
        in_specs=[
            pl.BlockSpec((bsz, d), lambda l, j: (0, 0)),
            pl.BlockSpec((None, d, tn), lambda l, j: (l, 0, j)),
            pl.BlockSpec((None, 1, tn), lambda l, j: (l, 0, j)),
        ],
        out_specs=pl.BlockSpec((None, bsz, tn), lambda l, j: (l, 0, j)),
        compiler_params=pltpu.CompilerParams(
            dimension_semantics=("arbitrary", "arbitrary"),
            vmem_limit_bytes=VMEM_LIMIT),
        name="adaln_mod",
    )(c, w_ada, b_ada.reshape(depth, 1, d6))


def _rope_head(x, r, cos, sin, scale=None):
    x1, x2 = x[r:r + HALF], x[r + HALF:r + HEAD_DIM]
    o1 = x1 * cos - x2 * sin
    o2 = x2 * cos + x1 * sin
    if scale is not None:
        o1, o2 = o1 * scale, o2 * scale
    return o1, o2


def _inproj_kernel(x_ref, sc_ref, sh_ref, pos_ref, inv_ref, w_ref,
                   qT_ref, kN_ref, vT_ref, wT_ref, *, w_scale):
    h = x_ref[...] * (1.0 + sc_ref[...]) + sh_ref[...]
    pT = lax.dot_general(w_ref[...], h.astype(jnp.bfloat16),
                         (((1,), (1,)), ((), ())),
                         preferred_element_type=jnp.float32)
    ang = inv_ref[...] * pos_ref[...].astype(jnp.float32)
    cos, sin = jnp.cos(ang), jnp.sin(ang)

    nblk = vT_ref.shape[0]
    for hd in range(Q_ROWS // HEAD_DIM):
        r = hd * HEAD_DIM
        scale = HEAD_DIM ** -0.5 if r < A_Q + B_Q else None
        o1, o2 = _rope_head(pT, r, cos, sin, scale)
        o1, o2 = o1.astype(qT_ref.dtype), o2.astype(qT_ref.dtype)
        for j in range(nblk):
            qT_ref[j, r:r + HALF, :] = o1[:, j * LANES:(j + 1) * LANES]
            qT_ref[j, r + HALF:r + HEAD_DIM, :] = o2[:, j * LANES:(j + 1) * LANES]

    k_rows = []
    for hd in range(K_ROWS // HEAD_DIM):
        k_rows.extend(_rope_head(pT, Q_ROWS + hd * HEAD_DIM, cos, sin))
    kN_ref[...] = jnp.concatenate(k_rows, axis=0).T.astype(kN_ref.dtype)

    v0 = Q_ROWS + K_ROWS
    v = pT[v0:v0 + V_ROWS].astype(vT_ref.dtype)
    w0 = v0 + V_ROWS
    w = pT[w0:w0 + W_ROWS] * w_scale
    for j in range(nblk):
        vT_ref[j] = v[:, j * LANES:(j + 1) * LANES]
        wT_ref[j] = w[:, j * LANES:(j + 1) * LANES]


def _inproj(x, sc, sh, pos3, inv, w_t, *, tm):
    bsz, s, d = x.shape
    kern = functools.partial(_inproj_kernel, w_scale=IDX_HEADS ** -0.5 * IDX_DIM ** -0.5)
    return pl.pallas_call(
        kern,
        out_shape=(
            jax.ShapeDtypeStruct((bsz, s // LANES, Q_ROWS, LANES), jnp.bfloat16),
            jax.ShapeDtypeStruct((bsz, s, K_ROWS), jnp.bfloat16),
            jax.ShapeDtypeStruct((bsz, s // LANES, V_ROWS, LANES), jnp.bfloat16),
            jax.ShapeDtypeStruct((bsz, s // LANES, W_ROWS, LANES), jnp.float32),
        ),
        grid=(bsz, s // tm),
        in_specs=[
            pl.BlockSpec((None, tm, d), lambda b, i: (b, i, 0)),
            pl.BlockSpec((None, 1, d), lambda b, i: (b, 0, 0)),
            pl.BlockSpec((None, 1, d), lambda b, i: (b, 0, 0)),
            pl.BlockSpec((None, 1, tm), lambda b, i: (b, 0, i)),
            pl.BlockSpec((HALF, 1), lambda b, i: (0, 0)),
            pl.BlockSpec((P_ROWS, d), lambda b, i: (0, 0)),
        ],
        out_specs=(
            pl.BlockSpec((None, tm // LANES, Q_ROWS, LANES), lambda b, i: (b, i, 0, 0)),
            pl.BlockSpec((None, tm, K_ROWS), lambda b, i: (b, i, 0)),
            pl.BlockSpec((None, tm // LANES, V_ROWS, LANES), lambda b, i: (b, i, 0, 0)),
            pl.BlockSpec((None, tm // LANES, W_ROWS, LANES), lambda b, i: (b, i, 0, 0)),
        ),
        compiler_params=pltpu.CompilerParams(
            dimension_semantics=("arbitrary", "arbitrary"),
            vmem_limit_bytes=VMEM_LIMIT),
        name="in_proj",
    )(x, sc, sh, pos3, inv, w_t)


def _ordered_to_f32(c):
    bits = jnp.where(c >= 0, c, c ^ jnp.int32(0x7FFFFFFF))
    return lax.bitcast_convert_type(bits, jnp.float32)


def _head_rhs(qT, row0):
    zeros = jnp.zeros((HEAD_DIM, LANES), qT.dtype)
    cols = []
    for h in range(A_HEADS):
        blk = qT[row0 + h * HEAD_DIM:row0 + (h + 1) * HEAD_DIM, :]
        g = h // GROUP
        cols.append(jnp.concatenate([blk, zeros] if g == 0 else [zeros, blk], axis=0))
    return jnp.concatenate(cols, axis=1)


def _attn_kernel(sink_ref, qT_ref, kN_ref, vT_ref, wT_ref, ga_ref, gb_ref,
                 y_ref, score_ref, thr_ref, jt_ref, m_ref, o_ref, yT_ref,
                 sa_ref, p_ref, rhs_ref, sq_ref,
                 *, seq, topk):
    n = pl.program_id(1)
    q0 = n * BLOCK
    nk = (n + 2) // 2
    qidx = q0 + lax.broadcasted_iota(jnp.int32, (KEY_CHUNK, LANES), 1)
    row_iota = lax.broadcasted_iota(jnp.int32, (KEY_CHUNK, LANES), 0)
    nh = B_HEADS

    def identity():
        return (lax.broadcasted_iota(jnp.int32, (LANES, LANES), 0)
                == lax.broadcasted_iota(jnp.int32, (LANES, LANES), 1)).astype(jnp.bfloat16)

    def mask_rhs(row0):
        return jnp.concatenate([_head_rhs(qT_ref[...], row0),
                                jnp.concatenate([identity()] * B_HEADS, axis=1)], axis=0)

    ones_rows = jnp.ones((ONES_ROWS, KEY_CHUNK), jnp.bfloat16)

    def finish_group(r0, width, g_ref):
        blk = yT_ref[r0:r0 + width, :]
        ms = jnp.sum(blk * blk, axis=0, keepdims=True) * (1.0 / width)
        y_ref[r0:r0 + width, :] = (blk * lax.rsqrt(ms + RMS_EPS) * g_ref[...]).astype(y_ref.dtype)

    def softmax_heads(s_all, m_of):
        p_heads, maxes = [], []
        for h in range(nh):
            sl = slice(h * LANES, (h + 1) * LANES)
            s_h = s_all[:, sl]
            m_old = m_of(h, sl)
            m_new = jnp.maximum(m_old, jnp.max(s_h, axis=0, keepdims=True))
            p_heads.append(jnp.exp(s_h - m_new).astype(jnp.bfloat16))
            maxes.append((m_old, m_new))
        return p_heads, maxes

    rhs_a = mask_rhs(0)
    wblk = jnp.maximum(n - 1, 0)
    w0 = pl.multiple_of(wblk * BLOCK, BLOCK)
    delta = qidx - (w0 + row_iota)
    band = (delta >= 0) & (delta < WINDOW)
    bias = jnp.where(band, 0.0, NEG_BIG).astype(jnp.bfloat16)
    lhs = jnp.concatenate([kN_ref[pl.ds(w0, 2 * BLOCK), 0:LANES], bias], axis=1)
    s_all = jnp.dot(lhs, rhs_a, preferred_element_type=jnp.float32)

    for h in range(IDX_HEADS):
        rhs_ref[0:IDX_DIM, h * LANES:(h + 1) * LANES] = (
            qT_ref[A_Q + B_Q + h * IDX_DIM:A_Q + B_Q + (h + 1) * IDX_DIM, :])
    sa_ref[...] = jnp.dot(kN_ref[0:KEY_CHUNK, 2 * LANES:2 * LANES + IDX_DIM], rhs_ref[0:IDX_DIM, :],
                          preferred_element_type=jnp.float32)

    p_heads, ms = softmax_heads(
        s_all, lambda h, sl: jnp.full((1, LANES), sink_ref[h], jnp.float32))
    for g in range(A_KV_HEADS):
        pT = jnp.concatenate(p_heads[g * GROUP:(g + 1) * GROUP], axis=1)
        rows = slice(g * HEAD_DIM, (g + 1) * HEAD_DIM)
        vg = jnp.concatenate([vT_ref[wblk, rows, :], vT_ref[wblk + 1, rows, :]], axis=1)
        vg = jnp.concatenate([vg, ones_rows], axis=0)
        og = jnp.dot(vg, pT, preferred_element_type=jnp.float32)
        for j in range(GROUP):
            h = g * GROUP + j
            sink, m = ms[h]
            csl = slice(j * LANES, (j + 1) * LANES)
            l = og[HEAD_DIM:HEAD_DIM + 1, csl] + jnp.exp(sink - m)
            yT_ref[h * HEAD_DIM:(h + 1) * HEAD_DIM, :] = og[0:HEAD_DIM, csl] * (1.0 / l)

    max_nk = (seq // BLOCK + 1) // 2
    n_piece = B_HEADS * LANES // PIECE
    heads_per_piece = PIECE // LANES

    def score_stream(nkk):
        def run():
            for c in range(nkk):
                rows = slice(c * KEY_CHUNK, (c + 1) * KEY_CHUNK)
                sc = jnp.zeros((KEY_CHUNK, LANES), jnp.float32)
                for j in range(n_piece):
                    cs = slice(j * PIECE, (j + 1) * PIECE)
                    if c == 0:
                        rel = sa_ref[:, cs]
                    else:
                        rel = jnp.dot(kN_ref[rows, 2 * LANES:2 * LANES + IDX_DIM], rhs_ref[0:IDX_DIM, cs],
                                      preferred_element_type=jnp.float32)
                    for hh in range(heads_per_piece):
                        h = j * heads_per_piece + hh
                        sc = sc + jnp.maximum(rel[:, hh * LANES:(hh + 1) * LANES], 0.0) * wT_ref[h:h + 1, :]
                kidx = c * KEY_CHUNK + row_iota
                score_ref[rows, :] = jnp.where(kidx <= qidx, sc, -jnp.inf)
        return run

    def fold_rows(x):
        return x.reshape(KEY_CHUNK // 8, 8, LANES).sum(axis=0)

    def count(pred_fn):
        def body(c, acc):
            r0 = pl.multiple_of(c * KEY_CHUNK, KEY_CHUNK)
            blk = score_ref[pl.ds(r0, KEY_CHUNK), :]
            return acc + fold_rows(pred_fn(blk, r0).astype(jnp.int32))
        acc = lax.fori_loop(0, nk, body, jnp.zeros((8, LANES), jnp.int32))
        return acc.sum(axis=0, keepdims=True)

    def masked_min(pred_fn):
        def body(c, acc):
            r0 = pl.multiple_of(c * KEY_CHUNK, KEY_CHUNK)
            blk = score_ref[pl.ds(r0, KEY_CHUNK), :]
            v = jnp.where(pred_fn(blk, r0), blk, jnp.inf)
            return jnp.minimum(acc, v.reshape(KEY_CHUNK // 8, 8, LANES).min(axis=0))
        acc = lax.fori_loop(0, nk, body, jnp.full((8, LANES), jnp.inf, jnp.float32))
        return acc.min(axis=0, keepdims=True)

    def search_static(nkk):
        def count_ge(cf):
            acc = jnp.zeros((8, LANES), jnp.int32)
            for c in range(nkk):
                blk = score_ref[c * KEY_CHUNK:(c + 1) * KEY_CHUNK, :]
                acc = acc + fold_rows((blk >= cf).astype(jnp.int32))
            return acc.sum(axis=0, keepdims=True)

        def run():
            c0 = count_ge(0.0)
            nonneg = c0 >= topk
            pfx0 = jnp.where(nonneg, jnp.int32(0), jnp.int32(INT_MIN))
            st0 = (pfx0, jnp.where(nonneg, c0, jnp.int32(nkk * KEY_CHUNK)),
                   _ordered_to_f32(pfx0 + jnp.int32(1 << 30)))

            def bit_body(i, st):
                pfx, c_pfx, cf = st
                bit = jnp.int32(1) << (30 - i)
                cand = pfx + bit
                f_take = _ordered_to_f32(cand + (bit >> 1))
                f_keep = _ordered_to_f32(pfx + (bit >> 1))
                cnt = count_ge(cf)
                take = cnt >= topk
                return (jnp.where(take, cand, pfx), jnp.where(take, cnt, c_pfx),
                        jnp.where(take, f_take, f_keep))

            pfx, c_pfx, _ = lax.fori_loop(0, 31, bit_body, st0)
            return _ordered_to_f32(pfx), c_pfx

        return run

    def score_and_search(nkk):
        def run():
            score_stream(nkk)()
            if nkk == 1:
                return (jnp.full((1, LANES), -jnp.inf, jnp.float32), jnp.zeros((1, LANES), jnp.int32))
            return search_static(nkk)()
        return run

    lo, c_lo = lax.switch(nk - 1, [score_and_search(j) for j in range(1, max_nk + 1)])
    thr_ref[...] = lo
    jt_ref[...] = jnp.full((1, LANES), jnp.where(n >= 2, seq, -1), jnp.int32)
    unresolved = jnp.max(jnp.where(c_lo > topk, 1, 0), axis=1, keepdims=True)[0, 0] > 0

    rhs_ref[...] = mask_rhs(A_Q)

    @pl.when(unresolved)
    def _resolve_ties():
        def count_gt(bv):
            return count(lambda blk, r0: blk > bv)

        b0 = masked_min(lambda blk, r0: blk >= lo)

        def w_cond(st):
            return jnp.max(st[1], axis=1, keepdims=True)[0, 0] >= topk

        def w_body(st):
            bv, cgt = st
            nxt = masked_min(lambda blk, r0: blk > bv)
            bn = jnp.where(cgt >= topk, nxt, bv)
            return bn, count_gt(bn)

        bv, cgt = lax.while_loop(w_cond, w_body, (b0, count_gt(b0)))
        thr_ref[...] = bv
        need = (topk - cgt).astype(jnp.float32)

        tri = (lax.broadcasted_iota(jnp.int32, (KEY_CHUNK, KEY_CHUNK), 0)
               >= lax.broadcasted_iota(jnp.int32, (KEY_CHUNK, KEY_CHUNK), 1)).astype(jnp.bfloat16)

        def tie_body(c, st):
            seen, jt_f = st
            r0 = pl.multiple_of(c * KEY_CHUNK, KEY_CHUNK)
            tie = score_ref[pl.ds(r0, KEY_CHUNK), :] == bv
            upto = jnp.dot(tri, jnp.where(tie, 1.0, 0.0).astype(jnp.bfloat16),
                           preferred_element_type=jnp.float32) + seen
            kidx_f = (r0 + row_iota).astype(jnp.float32)
            kept = jnp.where(tie, jnp.where(upto <= need, kidx_f, -1.0), -1.0)
            return upto[KEY_CHUNK - 1:KEY_CHUNK, :], jnp.maximum(jt_f, jnp.max(kept, axis=0, keepdims=True))

        _, jt_f = lax.fori_loop(0, nk, tie_body, (jnp.zeros((1, LANES), jnp.float32),
                                                  jnp.full((1, LANES), -1.0, jnp.float32)))
        jt_ref[...] = jt_f.astype(jnp.int32)

    thr = thr_ref[...]
    jt = jt_ref[...]

    def qk_stream(nkk):
        def run():
            m_fin = [jnp.full((1, LANES), NEG_BIG, jnp.float32) for _ in range(nh)]
            for c in range(nkk):
                rows = slice(c * KEY_CHUNK, (c + 1) * KEY_CHUNK)
                sc = score_ref[rows, :]
                kidx = c * KEY_CHUNK + row_iota
                sel = (sc > thr) | ((sc == thr) & (kidx <= jt))
                bias = jnp.where(sel, 0.0, NEG_BIG).astype(jnp.bfloat16)
                lhs = jnp.concatenate([kN_ref[rows, LANES:2 * LANES], bias], axis=1)
                for j in range(n_piece):
                    cs = slice(j * PIECE, (j + 1) * PIECE)
                    s = jnp.dot(lhs, rhs_ref[:, cs], preferred_element_type=jnp.float32)
                    sq_ref[rows, cs] = s
                    for hh in range(heads_per_piece):
                        h = j * heads_per_piece + hh
                        m_fin[h] = jnp.maximum(
                            m_fin[h], jnp.max(s[:, hh * LANES:(hh + 1) * LANES], axis=0, keepdims=True))
            for h in range(nh):
                m_ref[:, h * LANES:(h + 1) * LANES] = m_fin[h]
        return run

    def softmax_chunk(c, carry):
        rows = pl.ds(pl.multiple_of(c * KEY_CHUNK, KEY_CHUNK), KEY_CHUNK)
        p_ref[rows, :] = jnp.exp(sq_ref[rows, :] - m_ref[...]).astype(p_ref.dtype)
        return carry

    def pv_stream(nkk):
        def run():
            keys = nkk * KEY_CHUNK
            ones = jnp.ones((ONES_ROWS, keys), jnp.bfloat16)
            for g in range(B_KV_HEADS):
                rows = slice(A_KV + g * HEAD_DIM, A_KV + (g + 1) * HEAD_DIM)
                csl = slice(g * GROUP * LANES, (g + 1) * GROUP * LANES)
                vg = jnp.concatenate([vT_ref[t, rows, :] for t in range(keys // LANES)], axis=1)
                vg = jnp.concatenate([vg, ones], axis=0)
                pv = jnp.dot(vg, p_ref[0:keys, csl], preferred_element_type=jnp.float32)
                o_ref[:, csl] = pv[0:O_ROWS]
        return run

    lax.switch(nk - 1, [qk_stream(j) for j in range(1, max_nk + 1)])
    lax.fori_loop(0, nk, softmax_chunk, 0)
    lax.switch(nk - 1, [pv_stream(j) for j in range(1, max_nk + 1)])

    ob = o_ref[0:HEAD_DIM, :] * (1.0 / o_ref[HEAD_DIM:HEAD_DIM + 1, :])
    for h in range(nh):
        yT_ref[A_Q + h * HEAD_DIM:A_Q + (h + 1) * HEAD_DIM, :] = ob[:, h * LANES:(h + 1) * LANES]

    finish_group(0, A_Q, ga_ref)
    finish_group(A_Q, B_Q, gb_ref)


def _attention(sinks, qT, kN, vT, wT, g_a, g_b):
    bsz, s, _ = kN.shape
    nb = s // BLOCK
    topk = min(TOPK_MAX, s // 4)
    kern = functools.partial(_attn_kernel, seq=s, topk=topk)
    return pl.pallas_call(
        kern,
        out_shape=jax.ShapeDtypeStruct((bsz, A_Q + B_Q, s), jnp.bfloat16),
        grid=(bsz, nb),
        in_specs=[
            pl.BlockSpec(memory_space=pltpu.SMEM),
            pl.BlockSpec((None, None, Q_ROWS, BLOCK), lambda b, n: (b, n, 0, 0)),
            pl.BlockSpec((None, s, K_ROWS), lambda b, n: (b, 0, 0)),
            pl.BlockSpec((None, nb, V_ROWS, LANES), lambda b, n: (b, 0, 0, 0)),
            pl.BlockSpec((None, None, W_ROWS, BLOCK), lambda b, n: (b, n, 0, 0)),
            pl.BlockSpec((A_Q, LANES), lambda b, n: (0, 0)),
            pl.BlockSpec((B_Q, LANES), lambda b, n: (0, 0)),
        ],
        out_specs=pl.BlockSpec((None, A_Q + B_Q, BLOCK), lambda b, n: (b, 0, n)),
        scratch_shapes=[
            pltpu.VMEM((s, LANES), jnp.float32),
            pltpu.VMEM((1, LANES), jnp.float32),
            pltpu.VMEM((1, LANES), jnp.int32),
            pltpu.VMEM((1, B_HEADS * LANES), jnp.float32),
            pltpu.VMEM((O_ROWS, B_HEADS * LANES), jnp.float32),
            pltpu.VMEM((A_Q + B_Q, LANES), jnp.float32),
            pltpu.VMEM((KEY_CHUNK, B_HEADS * LANES), jnp.float32),
            pltpu.VMEM((s, B_HEADS * LANES), jnp.bfloat16),
            pltpu.VMEM((2 * LANES, B_HEADS * LANES), jnp.bfloat16),
            pltpu.VMEM((s, B_HEADS * LANES), jnp.float32),
        ],
        compiler_params=pltpu.CompilerParams(
            dimension_semantics=("arbitrary", "arbitrary"),
            vmem_limit_bytes=VMEM_LIMIT),
        name="attention",
    )(sinks, qT, kN, vT, wT, g_a, g_b)


def _post_kernel(y_ref, x_ref, gt1_ref, wo_ref, g1_ref, b1_ref, sc_ref, sh_ref, gt2_ref,
                 wg_ref, wu_ref, wd_ref, g2_ref, b2_ref, o_ref, *, alpha):
    proj = lax.dot_general(y_ref[...], wo_ref[...], (((0,), (0,)), ((), ())),
                           preferred_element_type=jnp.float32)
    x1 = _layer_norm(alpha * x_ref[...] + (1.0 + gt1_ref[...]) * proj, g1_ref[...], b1_ref[...])
    h = (x1 * (1.0 + sc_ref[...]) + sh_ref[...]).astype(jnp.bfloat16)
    gate = jnp.dot(h, wg_ref[...], preferred_element_type=jnp.float32)
    up = jnp.dot(h, wu_ref[...], preferred_element_type=jnp.float32)
    act = (_silu(gate) * up).astype(jnp.bfloat16)
    y2 = jnp.dot(act, wd_ref[...], preferred_element_type=jnp.float32)
    o_ref[...] = _layer_norm(alpha * x1 + (1.0 + gt2_ref[...]) * y2, g2_ref[...], b2_ref[...])


def _post_attention(y, x, gt1, w_o, g1, b1, sc2, sh2, gt2, w_gate, w_up, w_down, g2, b2, *, alpha, tm):
    bsz, s, d = x.shape
    dy = y.shape[1]
    dff = w_gate.shape[-1]
    resident = pl.Buffered(1)
    tile = lambda width: pl.BlockSpec((None, tm, width), lambda b_, i: (b_, i, 0))
    per_batch = pl.BlockSpec((None, 1, d), lambda b_, i: (b_, 0, 0))
    row = pl.BlockSpec((1, d), lambda b_, i: (0, 0))
    weight = lambda r, c: pl.BlockSpec((r, c), lambda b_, i: (0, 0), pipeline_mode=resident)
    return pl.pallas_call(
        functools.partial(_post_kernel, alpha=alpha),
        out_shape=jax.ShapeDtypeStruct((bsz, s, d), jnp.float32),
        grid=(bsz, s // tm),
        in_specs=[pl.BlockSpec((None, dy, tm), lambda b_, i: (b_, 0, i)), tile(d), per_batch, weight(dy, d), row, row,
                  per_batch, per_batch, per_batch,
                  weight(d, dff), weight(d, dff), weight(dff, d), row, row],
        out_specs=tile(d),
        compiler_params=pltpu.CompilerParams(
            dimension_semantics=("arbitrary", "arbitrary"),
            vmem_limit_bytes=VMEM_LIMIT),
        name="out_proj_ffn",
    )(y, x, gt1, w_o, g1, b1, sc2, sh2, gt2, w_gate, w_up, w_down, g2, b2)


def _regroup_w_in(w_in_l):
    d = w_in_l.shape[0]
    splits = np.cumsum([A_Q, A_KV, A_KV, B_Q, B_KV, B_KV, I_Q, IDX_DIM, IDX_HEADS])
    qa, ka, va, qb, kb, vb, qi, ki, wi = jnp.split(w_in_l, splits[:-1], axis=1)
    zk = jnp.zeros((d, K_ROWS - A_KV - B_KV - IDX_DIM), w_in_l.dtype)
    zw = jnp.zeros((d, 2 * W_ROWS - IDX_HEADS), w_in_l.dtype)
    cols = jnp.concatenate([qa, qb, qi, ka, kb, ki, zk, va, vb, wi, zw], axis=1)
    return cols.T.astype(jnp.bfloat16)


def kernel(x, c, positions, w_ada, b_ada, w_in, attn_sinks, g_a, g_b, w_o,
           ln1_g, ln1_b, w_gate, w_up, w_down, ln2_g, ln2_b):
    bsz, s, d = x.shape
    depth = w_ada.shape[0]
    alpha = (2.0 * depth) ** 0.25
    bf = jnp.bfloat16

    mod = _adaln_mod(c, w_ada, b_ada)
    inv = (ROPE_THETA ** (-jnp.arange(HALF, dtype=jnp.float32) / HALF)).reshape(HALF, 1)
    pos3 = positions.reshape(bsz, 1, s)

    for l in range(depth):
        m = mod[l].reshape(bsz, 1, 6 * d)
        sh1, sc1, gt1, sh2, sc2, gt2 = [m[:, :, i * d:(i + 1) * d] for i in range(6)]
        qT, kN, vT, wT = _inproj(x, sc1, sh1, pos3, inv, _regroup_w_in(w_in[l]), tm=1024)
        y = _attention(attn_sinks[l], qT, kN, vT, wT,
                       jnp.broadcast_to(g_a[l][:, None], (A_Q, LANES)),
                       jnp.broadcast_to(g_b[l][:, None], (B_Q, LANES)))
        x = _post_attention(y, x, gt1, w_o[l].astype(bf), ln1_g[l].reshape(1, d), ln1_b[l].reshape(1, d),
                            sc2, sh2, gt2, w_gate[l].astype(bf), w_up[l].astype(bf), w_down[l].astype(bf),
                            ln2_g[l].reshape(1, d), ln2_b[l].reshape(1, d), alpha=alpha, tm=512)
    return x
```

```python
import functools

import jax
import jax.numpy as jnp
import numpy as np
from jax import lax
from jax.experimental import pallas as pl
from jax.experimental.pallas import tpu as pltpu

HEAD_DIM = 64
HALF = HEAD_DIM // 2
A_HEADS = 8
A_KV_HEADS = 2
B_HEADS = 8
B_KV_HEADS = 2
IDX_HEADS = 8
IDX_DIM = 64
WINDOW = 128
BLOCK = 128
TOPK_MAX = 256
ROPE_THETA = 10000.0
LN_EPS = 1e-5
RMS_EPS = 1e-6

A_Q = A_HEADS * HEAD_DIM
A_KV = A_KV_HEADS * HEAD_DIM
B_Q = B_HEADS * HEAD_DIM
B_KV = B_KV_HEADS * HEAD_DIM
I_Q = IDX_HEADS * IDX_DIM
GROUP = A_HEADS // A_KV_HEADS

LANES = 128
KEY_CHUNK = 256
PIECE = 256
ONES_ROWS = 16
O_ROWS = HEAD_DIM + 8
NEG_BIG = -1e30
INT_MIN = -(2 ** 31)
VMEM_LIMIT = 56 * 1024 * 1024

Q_ROWS = A_Q + B_Q + I_Q
K_ROWS = 3 * LANES
V_ROWS = A_KV + B_KV
W_ROWS = 8
P_ROWS = Q_ROWS + K_ROWS + V_ROWS + 2 * W_ROWS


def _silu(x):
    return x * (1.0 / (1.0 + jnp.exp(-x)))


def _layer_norm(z, g, b):
    mu = jnp.mean(z, axis=-1, keepdims=True)
    zc = z - mu
    var = jnp.mean(zc * zc, axis=-1, keepdims=True)
    return zc * lax.rsqrt(var + LN_EPS) * g + b


def _mod_kernel(c_ref, w_ref, b_ref, o_ref):
    c_act = _silu(c_ref[...]).astype(jnp.bfloat16)
    o_ref[...] = jnp.dot(c_act, w_ref[...].astype(jnp.bfloat16),
                         preferred_element_type=jnp.float32) + b_ref[...]


def _adaln_mod(c, w_ada, b_ada):
    depth, d, d6 = w_ada.shape
    bsz = c.shape[0]
    tn = 1024
    return pl.pallas_call(
        _mod_kernel,
        out_shape=jax.ShapeDtypeStruct((depth, bsz, d6), jnp.float32),
        grid=(depth, d6 // tn),
        in_specs=[
            pl.BlockSpec((bsz, d), lambda l, j: (0, 0)),
            pl.BlockSpec((None, d, tn), lambda l, j: (l, 0, j)),
            pl.BlockSpec((None, 1, tn), lambda l, j: (l, 0, j)),
        ],
        out_specs=pl.BlockSpec((None, bsz, tn), lambda l, j: (l, 0, j)),
        compiler_params=pltpu.CompilerParams(
            dimension_semantics=("arbitrary", "arbitrary"),
            vmem_limit_bytes=VMEM_LIMIT),
        name="adaln_mod",
    )(c, w_ada, b_ada.reshape(depth, 1, d6))


def _rope_head(x, r, cos, sin, scale=None):
    x1, x2 = x[r:r + HALF], x[r + HALF:r + HEAD_DIM]
    o1 = x1 * cos - x2 * sin
    o2 = x2 * cos + x1 * sin
    if scale is not None:
        o1, o2 = o1 * scale, o2 * scale
    return o1, o2


def _inproj_kernel(x_ref, sc_ref, sh_ref, pos_ref, inv_ref, w_ref,
                   qT_ref, kN_ref, vT_ref, wT_ref, *, w_scale):
    h = x_ref[...] * (1.0 + sc_ref[...]) + sh_ref[...]
    pT = lax.dot_general(w_ref[...], h.astype(jnp.bfloat16),
                         (((1,), (1,)), ((), ())),
                         preferred_element_type=jnp.float32)
    ang = inv_ref[...] * pos_ref[...].astype(jnp.float32)
    cos, sin = jnp.cos(ang), jnp.sin(ang)

    nblk = vT_ref.shape[0]
    for hd in range(Q_ROWS // HEAD_DIM):
        r = hd * HEAD_DIM
        scale = HEAD_DIM ** -0.5 if r < A_Q + B_Q else None
        o1, o2 = _rope_head(pT, r, cos, sin, scale)
        o1, o2 = o1.astype(qT_ref.dtype), o2.astype(qT_ref.dtype)
        for j in range(nblk):
            qT_ref[j, r:r + HALF, :] = o1[:, j * LANES:(j + 1) * LANES]
            qT_ref[j, r + HALF:r + HEAD_DIM, :] = o2[:, j * LANES:(j + 1) * LANES]

    k_rows = []
    for hd in range(K_ROWS // HEAD_DIM):
        k_rows.extend(_rope_head(pT, Q_ROWS + hd * HEAD_DIM, cos, sin))
    kN_ref[...] = jnp.concatenate(k_rows, axis=0).T.astype(kN_ref.dtype)

    v0 = Q_ROWS + K_ROWS
    v = pT[v0:v0 + V_ROWS].astype(vT_ref.dtype)
    w0 = v0 + V_ROWS
    w = pT[w0:w0 + W_ROWS] * w_scale
    for j in range(nblk):
        vT_ref[j] = v[:, j * LANES:(j + 1) * LANES]
        wT_ref[j] = w[:, j * LANES:(j + 1) * LANES]


def _inproj(x, sc, sh, pos3, inv, w_t, *, tm):
    bsz, s, d = x.shape
    kern = functools.partial(_inproj_kernel, w_scale=IDX_HEADS ** -0.5 * IDX_DIM ** -0.5)
    return pl.pallas_call(
        kern,
        out_shape=(
            jax.ShapeDtypeStruct((bsz, s // LANES, Q_ROWS, LANES), jnp.bfloat16),
            jax.ShapeDtypeStruct((bsz, s, K_ROWS), jnp.bfloat16),
            jax.ShapeDtypeStruct((bsz, s // LANES, V_ROWS, LANES), jnp.bfloat16),
            jax.ShapeDtypeStruct((bsz, s // LANES, W_ROWS, LANES), jnp.float32),
        ),
        grid=(bsz, s // tm),
        in_specs=[
            pl.BlockSpec((None, tm, d), lambda b, i: (b, i, 0)),
            pl.BlockSpec((None, 1, d), lambda b, i: (b, 0, 0)),
            pl.BlockSpec((None, 1, d), lambda b, i: (b, 0, 0)),
            pl.BlockSpec((None, 1, tm), lambda b, i: (b, 0, i)),
            pl.BlockSpec((HALF, 1), lambda b, i: (0, 0)),
            pl.BlockSpec((P_ROWS, d), lambda b, i: (0, 0)),
        ],
        out_specs=(
            pl.BlockSpec((None, tm // LANES, Q_ROWS, LANES), lambda b, i: (b, i, 0, 0)),
            pl.BlockSpec((None, tm, K_ROWS), lambda b, i: (b, i, 0)),
            pl.BlockSpec((None, tm // LANES, V_ROWS, LANES), lambda b, i: (b, i, 0, 0)),
            pl.BlockSpec((None, tm // LANES, W_ROWS, LANES), lambda b, i: (b, i, 0, 0)),
        ),
        compiler_params=pltpu.CompilerParams(
            dimension_semantics=("arbitrary", "arbitrary"),
            vmem_limit_bytes=VMEM_LIMIT),
        name="in_proj",
    )(x, sc, sh, pos3, inv, w_t)


def _ordered_to_f32(c):
    bits = jnp.where(c >= 0, c, c ^ jnp.int32(0x7FFFFFFF))
    return lax.bitcast_convert_type(bits, jnp.float32)


def _head_rhs(qT, row0):
    zeros = jnp.zeros((HEAD_DIM, LANES), qT.dtype)
    cols = []
    for h in range(A_HEADS):
        blk = qT[row0 + h * HEAD_DIM:row0 + (h + 1) * HEAD_DIM, :]
        g = h // GROUP
        cols.append(jnp.concatenate([blk, zeros] if g == 0 else [zeros, blk], axis=0))
    return jnp.concatenate(cols, axis=1)


def _attn_kernel(sink_ref, qT_ref, kN_ref, vT_ref, wT_ref, ga_ref, gb_ref,
                 y_ref, score_ref, thr_ref, jt_ref, m_ref, o_ref, yT_ref,
                 sa_ref, p_ref, rhs_ref, sq_ref,
                 *, seq, topk):
    n = pl.program_id(1)
    q0 = n * BLOCK
    nk = (n + 2) // 2
    qidx = q0 + lax.broadcasted_iota(jnp.int32, (KEY_CHUNK, LANES), 1)
    row_iota = lax.broadcasted_iota(jnp.int32, (KEY_CHUNK, LANES), 0)
    nh = B_HEADS

    def identity():
        return (lax.broadcasted_iota(jnp.int32, (LANES, LANES), 0)
                == lax.broadcasted_iota(jnp.int32, (LANES, LANES), 1)).astype(jnp.bfloat16)

    def mask_rhs(row0):
        return jnp.concatenate([_head_rhs(qT_ref[...], row0),
                                jnp.concatenate([identity()] * B_HEADS, axis=1)], axis=0)

    ones_rows = jnp.ones((ONES_ROWS, KEY_CHUNK), jnp.bfloat16)

    def finish_group(r0, width, g_ref):
        blk = yT_ref[r0:r0 + width, :]
        ms = jnp.sum(blk * blk, axis=0, keepdims=True) * (1.0 / width)
        y_ref[r0:r0 + width, :] = (blk * lax.rsqrt(ms + RMS_EPS) * g_ref[...]).astype(y_ref.dtype)

    def softmax_heads(s_all, m_of):
        p_heads, maxes = [], []
        for h in range(nh):
            sl = slice(h * LANES, (h + 1) * LANES)
            s_h = s_all[:, sl]
            m_old = m_of(h, sl)
            m_new = jnp.maximum(m_old, jnp.max(s_h, axis=0, keepdims=True))
            p_heads.append(jnp.exp(s_h - m_new).astype(jnp.bfloat16))
            maxes.append((m_old, m_new))
        return p_heads, maxes

    rhs_a = mask_rhs(0)
    wblk = jnp.maximum(n - 1, 0)
    w0 = pl.multiple_of(wblk * BLOCK, BLOCK)
    delta = qidx - (w0 + row_iota)
    band = (delta >= 0) & (delta < WINDOW)
    bias = jnp.where(band, 0.0, NEG_BIG).astype(jnp.bfloat16)
    lhs = jnp.concatenate([kN_ref[pl.ds(w0, 2 * BLOCK), 0:LANES], bias], axis=1)
    s_all = jnp.dot(lhs, rhs_a, preferred_element_type=jnp.float32)

    for h in range(IDX_HEADS):
        rhs_ref[0:IDX_DIM, h * LANES:(h + 1) * LANES] = (
            qT_ref[A_Q + B_Q + h * IDX_DIM:A_Q + B_Q + (h + 1) * IDX_DIM, :])
    sa_ref[...] = jnp.dot(kN_ref[0:KEY_CHUNK, 2 * LANES:2 * LANES + IDX_DIM], rhs_ref[0:IDX_DIM, :],
                          preferred_element_type=jnp.float32)

    p_heads, ms = softmax_heads(
        s_all, lambda h, sl: jnp.full((1, LANES), sink_ref[h], jnp.float32))
    for g in range(A_KV_HEADS):
        pT = jnp.concatenate(p_heads[g * GROUP:(g + 1) * GROUP], axis=1)
        rows = slice(g * HEAD_DIM, (g + 1) * HEAD_DIM)
        vg = jnp.concatenate([vT_ref[wblk, rows, :], vT_ref[wblk + 1, rows, :]], axis=1)
        vg = jnp.concatenate([vg, ones_rows], axis=0)
        og = jnp.dot(vg, pT, preferred_element_type=jnp.float32)
        for j in range(GROUP):
            h = g * GROUP + j
            sink, m = ms[h]
            csl = slice(j * LANES, (j + 1) * LANES)
            l = og[HEAD_DIM:HEAD_DIM + 1, csl] + jnp.exp(sink - m)
            yT_ref[h * HEAD_DIM:(h + 1) * HEAD_DIM, :] = og[0:HEAD_DIM, csl] * (1.0 / l)

    max_nk = (seq // BLOCK + 1) // 2
    n_piece = B_HEADS * LANES // PIECE
    heads_per_piece = PIECE // LANES

    def score_stream(nkk):
        def run():
            for c in range(nkk):
                rows = slice(c * KEY_CHUNK, (c + 1) * KEY_CHUNK)
                sc = jnp.zeros((KEY_CHUNK, LANES), jnp.float32)
                for j in range(n_piece):
                    cs = slice(j * PIECE, (j + 1) * PIECE)
                    if c == 0:
                        rel = sa_ref[:, cs]
                    else:
                        rel = jnp.dot(kN_ref[rows, 2 * LANES:2 * LANES + IDX_DIM], rhs_ref[0:IDX_DIM, cs],
                                      preferred_element_type=jnp.float32)
                    for hh in range(heads_per_piece):
                        h = j * heads_per_piece + hh
                        sc = sc + jnp.maximum(rel[:, hh * LANES:(hh + 1) * LANES], 0.0) * wT_ref[h:h + 1, :]
                kidx = c * KEY_CHUNK + row_iota
                score_ref[rows, :] = jnp.where(kidx <= qidx, sc, -jnp.inf)
        return run

    def fold_rows(x):
        return x.reshape(KEY_CHUNK // 8, 8, LANES).sum(axis=0)

    def count(pred_fn):
        def body(c, acc):
            r0 = pl.multiple_of(c * KEY_CHUNK, KEY_CHUNK)
            blk = score_ref[pl.ds(r0, KEY_CHUNK), :]
            return acc + fold_rows(pred_fn(blk, r0).astype(jnp.int32))
        acc = lax.fori_loop(0, nk, body, jnp.zeros((8, LANES), jnp.int32))
        return acc.sum(axis=0, keepdims=True)

    def masked_min(pred_fn):
        def body(c, acc):
            r0 = pl.multiple_of(c * KEY_CHUNK, KEY_CHUNK)
            blk = score_ref[pl.ds(r0, KEY_CHUNK), :]
            v = jnp.where(pred_fn(blk, r0), blk, jnp.inf)
            return jnp.minimum(acc, v.reshape(KEY_CHUNK // 8, 8, LANES).min(axis=0))
        acc = lax.fori_loop(0, nk, body, jnp.full((8, LANES), jnp.inf, jnp.float32))
        return acc.min(axis=0, keepdims=True)

    def search_static(nkk):
        def count_ge(cf):
            acc = jnp.zeros((8, LANES), jnp.int32)
            for c in range(nkk):
                blk = score_ref[c * KEY_CHUNK:(c + 1) * KEY_CHUNK, :]
                acc = acc + fold_rows((blk >= cf).astype(jnp.int32))
            return acc.sum(axis=0, keepdims=True)

        def run():
            c0 = count_ge(0.0)
            nonneg = c0 >= topk
            pfx0 = jnp.where(nonneg, jnp.int32(0), jnp.int32(INT_MIN))
            st0 = (pfx0, jnp.where(nonneg, c0, jnp.int32(nkk * KEY_CHUNK)),
                   _ordered_to_f32(pfx0 + jnp.int32(1 << 30)))

            def bit_body(i, st):
                pfx, c_pfx, cf = st
                bit = jnp.int32(1) << (30 - i)
                cand = pfx + bit
                f_take = _ordered_to_f32(cand + (bit >> 1))
                f_keep = _ordered_to_f32(pfx + (bit >> 1))
                cnt = count_ge(cf)
                take = cnt >= topk
                return (jnp.where(take, cand, pfx), jnp.where(take, cnt, c_pfx),
                        jnp.where(take, f_take, f_keep))

            pfx, c_pfx, _ = lax.fori_loop(0, 31, bit_body, st0)
            return _ordered_to_f32(pfx), c_pfx

        return run

    def score_and_search(nkk):
        def run():
            score_stream(nkk)()
            if nkk == 1:
                return (jnp.full((1, LANES), -jnp.inf, jnp.float32), jnp.zeros((1, LANES), jnp.int32))
            return search_static(nkk)()
        return run

    lo, c_lo = lax.switch(nk - 1, [score_and_search(j) for j in range(1, max_nk + 1)])
    thr_ref[...] = lo
    jt_ref[...] = jnp.full((1, LANES), jnp.where(n >= 2, seq, -1), jnp.int32)
    unresolved = jnp.max(jnp.where(c_lo > topk, 1, 0), axis=1, keepdims=True)[0, 0] > 0

    rhs_ref[...] = mask_rhs(A_Q)

    @pl.when(unresolved)
    def _resolve_ties():
        def count_gt(bv):
            return count(lambda blk, r0: blk > bv)

        b0 = masked_min(lambda blk, r0: blk >= lo)

        def w_cond(st):
            return jnp.max(st[1], axis=1, keepdims=True)[0, 0] >= topk

        def w_body(st):
            bv, cgt = st
            nxt = masked_min(lambda blk, r0: blk > bv)
            bn = jnp.where(cgt >= topk, nxt, bv)
            return bn, count_gt(bn)

        bv, cgt = lax.while_loop(w_cond, w_body, (b0, count_gt(b0)))
        thr_ref[...] = bv
        need = (topk - cgt).astype(jnp.float32)

        tri = (lax.broadcasted_iota(jnp.int32, (KEY_CHUNK, KEY_CHUNK), 0)
               >= lax.broadcasted_iota(jnp.int32, (KEY_CHUNK, KEY_CHUNK), 1)).astype(jnp.bfloat16)

        def tie_body(c, st):
            seen, jt_f = st
            r0 = pl.multiple_of(c * KEY_CHUNK, KEY_CHUNK)
            tie = score_ref[pl.ds(r0, KEY_CHUNK), :] == bv
            upto = jnp.dot(tri, jnp.where(tie, 1.0, 0.0).astype(jnp.bfloat16),
                           preferred_element_type=jnp.float32) + seen
            kidx_f = (r0 + row_iota).astype(jnp.float32)
            kept = jnp.where(tie, jnp.where(upto <= need, kidx_f, -1.0), -1.0)
            return upto[KEY_CHUNK - 1:KEY_CHUNK, :], jnp.maximum(jt_f, jnp.max(kept, axis=0, keepdims=True))

        _, jt_f = lax.fori_loop(0, nk, tie_body, (jnp.zeros((1, LANES), jnp.float32),
                                                  jnp.full((1, LANES), -1.0, jnp.float32)))
        jt_ref[...] = jt_f.astype(jnp.int32)

    thr = thr_ref[...]
    jt = jt_ref[...]

    def qk_stream(nkk):
        def run():
            m_fin = [jnp.full((1, LANES), NEG_BIG, jnp.float32) for _ in range(nh)]
            for c in range(nkk):
                rows = slice(c * KEY_CHUNK, (c + 1) * KEY_CHUNK)
                sc = score_ref[rows, :]
                kidx = c * KEY_CHUNK + row_iota
                sel = (sc > thr) | ((sc == thr) & (kidx <= jt))
                bias = jnp.where(sel, 0.0, NEG_BIG).astype(jnp.bfloat16)
                lhs = jnp.concatenate([kN_ref[rows, LANES:2 * LANES], bias], axis=1)
                for j in range(n_piece):
                    cs = slice(j * PIECE, (j + 1) * PIECE)
                    s = jnp.dot(lhs, rhs_ref[:, cs], preferred_element_type=jnp.float32)
                    sq_ref[rows, cs] = s
                    for hh in range(heads_per_piece):
                        h = j * heads_per_piece + hh
                        m_fin[h] = jnp.maximum(
                            m_fin[h], jnp.max(s[:, hh * LANES:(hh + 1) * LANES], axis=0, keepdims=True))
            for h in range(nh):
                m_ref[:, h * LANES:(h + 1) * LANES] = m_fin[h]
        return run

    def softmax_chunk(c, carry):
        rows = pl.ds(pl.multiple_of(c * KEY_CHUNK, KEY_CHUNK), KEY_CHUNK)
        p_ref[rows, :] = jnp.exp(sq_ref[rows, :] - m_ref[...]).astype(p_ref.dtype)
        return carry

    def pv_stream(nkk):
        def run():
            keys = nkk * KEY_CHUNK
            ones = jnp.ones((ONES_ROWS, keys), jnp.bfloat16)
            for g in range(B_KV_HEADS):
                rows = slice(A_KV + g * HEAD_DIM, A_KV + (g + 1) * HEAD_DIM)
                csl = slice(g * GROUP * LANES, (g + 1) * GROUP * LANES)
                vg = jnp.concatenate([vT_ref[t, rows, :] for t in range(keys // LANES)], axis=1)
                vg = jnp.concatenate([vg, ones], axis=0)
                pv = jnp.dot(vg, p_ref[0:keys, csl], preferred_element_type=jnp.float32)
                o_ref[:, csl] = pv[0:O_ROWS]
        return run

    lax.switch(nk - 1, [qk_stream(j) for j in range(1, max_nk + 1)])
    lax.fori_loop(0, nk, softmax_chunk, 0)
    lax.switch(nk - 1, [pv_stream(j) for j in range(1, max_nk + 1)])

    ob = o_ref[0:HEAD_DIM, :] * (1.0 / o_ref[HEAD_DIM:HEAD_DIM + 1, :])
    for h in range(nh):
        yT_ref[A_Q + h * HEAD_DIM:A_Q + (h + 1) * HEAD_DIM, :] = ob[:, h * LANES:(h + 1) * LANES]

    finish_group(0, A_Q, ga_ref)
    finish_group(A_Q, B_Q, gb_ref)


def _attention(sinks, qT, kN, vT, wT, g_a, g_b):
    bsz, s, _ = kN.shape
    nb = s // BLOCK
    topk = min(TOPK_MAX, s // 4)
    kern = functools.partial(_attn_kernel, seq=s, topk=topk)
    return pl.pallas_call(
        kern,
        out_shape=jax.ShapeDtypeStruct((bsz, A_Q + B_Q, s), jnp.bfloat16),
        grid=(bsz, nb),
        in_specs=[
            pl.BlockSpec(memory_space=pltpu.SMEM),
            pl.BlockSpec((None, None, Q_ROWS, BLOCK), lambda b, n: (b, n, 0, 0)),
            pl.BlockSpec((None, s, K_ROWS), lambda b, n: (b, 0, 0)),
            pl.BlockSpec((None, nb, V_ROWS, LANES), lambda b, n: (b, 0, 0, 0)),
            pl.BlockSpec((None, None, W_ROWS, BLOCK), lambda b, n: (b, n, 0, 0)),
            pl.BlockSpec((A_Q, LANES), lambda b, n: (0, 0)),
            pl.BlockSpec((B_Q, LANES), lambda b, n: (0, 0)),
        ],
        out_specs=pl.BlockSpec((None, A_Q + B_Q, BLOCK), lambda b, n: (b, 0, n)),
        scratch_shapes=[
            pltpu.VMEM((s, LANES), jnp.float32),
            pltpu.VMEM((1, LANES), jnp.float32),
            pltpu.VMEM((1, LANES), jnp.int32),
            pltpu.VMEM((1, B_HEADS * LANES), jnp.float32),
            pltpu.VMEM((O_ROWS, B_HEADS * LANES), jnp.float32),
            pltpu.VMEM((A_Q + B_Q, LANES), jnp.float32),
            pltpu.VMEM((KEY_CHUNK, B_HEADS * LANES), jnp.float32),
            pltpu.VMEM((s, B_HEADS * LANES), jnp.bfloat16),
            pltpu.VMEM((2 * LANES, B_HEADS * LANES), jnp.bfloat16),
            pltpu.VMEM((s, B_HEADS * LANES), jnp.float32),
        ],
        compiler_params=pltpu.CompilerParams(
            dimension_semantics=("arbitrary", "arbitrary"),
            vmem_limit_bytes=VMEM_LIMIT),
        name="attention",
    )(sinks, qT, kN, vT, wT, g_a, g_b)


def _post_kernel(y_ref, x_ref, gt1_ref, wo_ref, g1_ref, b1_ref, sc_ref, sh_ref, gt2_ref,
                 wg_ref, wu_ref, wd_ref, g2_ref, b2_ref, o_ref, *, alpha):
    tm = x_ref.shape[0]
    halves = [slice(i * tm // 2, (i + 1) * tm // 2) for i in range(2)]
    proj = [lax.dot_general(y_ref[:, r], wo_ref[...], (((0,), (0,)), ((), ())),
                            preferred_element_type=jnp.float32) for r in halves]
    x1, gate, up = [], [], []
    for r, pr in zip(halves, proj):
        x1.append(_layer_norm(alpha * x_ref[r, :] + (1.0 + gt1_ref[...]) * pr, g1_ref[...], b1_ref[...]))
        h = (x1[-1] * (1.0 + sc_ref[...]) + sh_ref[...]).astype(jnp.bfloat16)
        gate.append(jnp.dot(h, wg_ref[...], preferred_element_type=jnp.float32))
        up.append(jnp.dot(h, wu_ref[...], preferred_element_type=jnp.float32))
    y2 = [jnp.dot((_silu(g) * u).astype(jnp.bfloat16), wd_ref[...], preferred_element_type=jnp.float32)
          for g, u in zip(gate, up)]
    for r, xh, yh in zip(halves, x1, y2):
        o_ref[r, :] = _layer_norm(alpha * xh + (1.0 + gt2_ref[...]) * yh, g2_ref[...], b2_ref[...])


def _post_attention(y, x, gt1, w_o, g1, b1, sc2, sh2, gt2, w_gate, w_up, w_down, g2, b2, *, alpha, tm):
    bsz, s, d = x.shape
    dy = y.shape[1]
    dff = w_gate.shape[-1]
    resident = pl.Buffered(1)
    tile = lambda width: pl.BlockSpec((None, tm, width), lambda b_, i: (b_, i, 0))
    per_batch = pl.BlockSpec((None, 1, d), lambda b_, i: (b_, 0, 0))
    row = pl.BlockSpec((1, d), lambda b_, i: (0, 0))
    weight = lambda r, c: pl.BlockSpec((r, c), lambda b_, i: (0, 0), pipeline_mode=resident)
    return pl.pallas_call(
        functools.partial(_post_kernel, alpha=alpha),
        out_shape=jax.ShapeDtypeStruct((bsz, s, d), jnp.float32),
        grid=(bsz, s // tm),
        in_specs=[pl.BlockSpec((None, dy, tm), lambda b_, i: (b_, 0, i)), tile(d), per_batch, weight(dy, d), row, row,
                  per_batch, per_batch, per_batch,
                  weight(d, dff), weight(d, dff), weight(dff, d), row, row],
        out_specs=tile(d),
        compiler_params=pltpu.CompilerParams(
            dimension_semantics=("arbitrary", "arbitrary"),
            vmem_limit_bytes=VMEM_LIMIT),
        name="out_proj_ffn",
    )(y, x, gt1, w_o, g1, b1, sc2, sh2, gt2, w_gate, w_up, w_down, g2, b2)


def _regroup_w_in(w_in_l):
    d = w_in_l.shape[0]
    splits = np.cumsum([A_Q, A_KV, A_KV, B_Q, B_KV, B_KV, I_Q, IDX_DIM, IDX_HEADS])
    qa, ka, va, qb, kb, vb, qi, ki, wi = jnp.split(w_in_l, splits[:-1], axis=1)
    zk = jnp.zeros((d, K_ROWS - A_KV - B_KV - IDX_DIM), w_in_l.dtype)
    zw = jnp.zeros((d, 2 * W_ROWS - IDX_HEADS), w_in_l.dtype)
    cols = jnp.concatenate([qa, qb, qi, ka, kb, ki, zk, va, vb, wi, zw], axis=1)
    return cols.T.astype(jnp.bfloat16)


def kernel(x, c, positions, w_ada, b_ada, w_in, attn_sinks, g_a, g_b, w_o,
           ln1_g, ln1_b, w_gate, w_up, w_down, ln2_g, ln2_b):
    bsz, s, d = x.shape
    depth = w_ada.shape[0]
    alpha = (2.0 * depth) ** 0.25
    bf = jnp.bfloat16

    mod = _adaln_mod(c, w_ada, b_ada)
    inv = (ROPE_THETA ** (-jnp.arange(HALF, dtype=jnp.float32) / HALF)).reshape(HALF, 1)
    pos3 = positions.reshape(bsz, 1, s)

    for l in range(depth):
        m = mod[l].reshape(bsz, 1, 6 * d)
        sh1, sc1, gt1, sh2, sc2, gt2 = [m[:, :, i * d:(i + 1) * d] for i in range(6)]
        qT, kN, vT, wT = _inproj(x, sc1, sh1, pos3, inv, _regroup_w_in(w_in[l]), tm=1024)
        y = _attention(attn_sinks[l], qT, kN, vT, wT,
                       jnp.broadcast_to(g_a[l][:, None], (A_Q, LANES)),
                       jnp.broadcast_to(g_b[l][:, None], (B_Q, LANES)))
        x = _post_attention(y, x, gt1, w_o[l].astype(bf), ln1_g[l].reshape(1, d), ln1_b[l].reshape(1, d),
                            sc2, sh2, gt2, w_gate[l].astype(bf), w_up[l].astype(bf), w_down[l].astype(bf),
                            ln2_g[l].reshape(1, d), ln2_b[l].reshape(1, d), alpha=alpha, tm=512)
    return x
```

```python
import functools

import jax
import jax.numpy as jnp
import numpy as np
from jax import lax
from jax.experimental import pallas as pl
from jax.experimental.pallas import tpu as pltpu

HEAD_DIM = 64
HALF = HEAD_DIM // 2
A_HEADS = 8
A_KV_HEADS = 2
B_HEADS = 8
B_KV_HEADS = 2
IDX_HEADS = 8
IDX_DIM = 64
WINDOW = 128
BLOCK = 128
TOPK_MAX = 256
ROPE_THETA = 10000.0
LN_EPS = 1e-5
RMS_EPS = 1e-6

A_Q = A_HEADS * HEAD_DIM
A_KV = A_KV_HEADS * HEAD_DIM
B_Q = B_HEADS * HEAD_DIM
B_KV = B_KV_HEADS * HEAD_DIM
I_Q = IDX_HEADS * IDX_DIM
GROUP = A_HEADS // A_KV_HEADS

LANES = 128
KEY_CHUNK = 256
PIECE = 256
ONES_ROWS = 16
O_ROWS = HEAD_DIM + 8
NEG_BIG = -1e30
INT_MIN = -(2 ** 31)
VMEM_LIMIT = 56 * 1024 * 1024

Q_ROWS = A_Q + B_Q + I_Q
K_ROWS = 3 * LANES
V_ROWS = A_KV + B_KV
W_ROWS = 8
P_ROWS = Q_ROWS + K_ROWS + V_ROWS + 2 * W_ROWS


def _silu(x):
    return x * (1.0 / (1.0 + jnp.exp(-x)))


def _layer_norm(z, g, b):
    mu = jnp.mean(z, axis=-1, keepdims=True)
    zc = z - mu
    var = jnp.mean(zc * zc, axis=-1, keepdims=True)
    return zc * lax.rsqrt(var + LN_EPS) * g + b


def _mod_kernel(c_ref, w_ref, b_ref, o_ref):
    c_act = _silu(c_ref[...]).astype(jnp.bfloat16)
    o_ref[...] = jnp.dot(c_act, w_ref[...].astype(jnp.bfloat16),
                         preferred_element_type=jnp.float32) + b_ref[...]


def _adaln_mod(c, w_ada, b_ada):
    depth, d, d6 = w_ada.shape
    bsz = c.shape[0]
    tn = 1024
    return pl.pallas_call(
        _mod_kernel,
        out_shape=jax.ShapeDtypeStruct((depth, bsz, d6), jnp.float32),
        grid=(depth, d6 // tn),
        in_specs=[
            pl.BlockSpec((bsz, d), lambda l, j: (0, 0)),
            pl.BlockSpec((None, d, tn), lambda l, j: (l, 0, j)),
            pl.BlockSpec((None, 1, tn), lambda l, j: (l, 0, j)),
        ],
        out_specs=pl.BlockSpec((None, bsz, tn), lambda l, j: (l, 0, j)),
        compiler_params=pltpu.CompilerParams(
            dimension_semantics=("arbitrary", "arbitrary"),
            vmem_limit_bytes=VMEM_LIMIT),
        name="adaln_mod",
    )(c, w_ada, b_ada.reshape(depth, 1, d6))


def _rope_head(x, r, cos, sin, scale=None):
    x1, x2 = x[r:r + HALF], x[r + HALF:r + HEAD_DIM]
    o1 = x1 * cos - x2 * sin
    o2 = x2 * cos + x1 * sin
    if scale is not None:
        o1, o2 = o1 * scale, o2 * scale
    return o1, o2


def _inproj_kernel(x_ref, sc_ref, sh_ref, pos_ref, inv_ref, w_ref,
                   qT_ref, kN_ref, vT_ref, wT_ref, *, w_scale):
    tm = x_ref.shape[0]
    halves = [slice(i * tm // 2, (i + 1) * tm // 2) for i in range(2)]
    pTs = []
    for r in halves:
        h = x_ref[r, :] * (1.0 + sc_ref[...]) + sh_ref[...]
        pTs.append(lax.dot_general(w_ref[...], h.astype(jnp.bfloat16), (((1,), (1,)), ((), ())),
                                   preferred_element_type=jnp.float32))
    for i, (r, pT) in enumerate(zip(halves, pTs)):
        _inproj_epilogue(pT, pos_ref[:, r], inv_ref, qT_ref, kN_ref, vT_ref, wT_ref, r,
                         i * (tm // 2 // LANES), w_scale)


def _inproj_epilogue(pT, pos, inv_ref, qT_ref, kN_ref, vT_ref, wT_ref, tok, blk0, w_scale):
    ang = inv_ref[...] * pos.astype(jnp.float32)
    cos, sin = jnp.cos(ang), jnp.sin(ang)

    nblk = pT.shape[1] // LANES
    for hd in range(Q_ROWS // HEAD_DIM):
        r = hd * HEAD_DIM
        scale = HEAD_DIM ** -0.5 if r < A_Q + B_Q else None
        o1, o2 = _rope_head(pT, r, cos, sin, scale)
        o1, o2 = o1.astype(qT_ref.dtype), o2.astype(qT_ref.dtype)
        for j in range(nblk):
            qT_ref[blk0 + j, r:r + HALF, :] = o1[:, j * LANES:(j + 1) * LANES]
            qT_ref[blk0 + j, r + HALF:r + HEAD_DIM, :] = o2[:, j * LANES:(j + 1) * LANES]

    k_rows = []
    for hd in range(K_ROWS // HEAD_DIM):
        k_rows.extend(_rope_head(pT, Q_ROWS + hd * HEAD_DIM, cos, sin))
    kN_ref[tok, :] = jnp.concatenate(k_rows, axis=0).T.astype(kN_ref.dtype)

    v0 = Q_ROWS + K_ROWS
    v = pT[v0:v0 + V_ROWS].astype(vT_ref.dtype)
    w0 = v0 + V_ROWS
    w = pT[w0:w0 + W_ROWS] * w_scale
    for j in range(nblk):
        vT_ref[blk0 + j] = v[:, j * LANES:(j + 1) * LANES]
        wT_ref[blk0 + j] = w[:, j * LANES:(j + 1) * LANES]


def _inproj(x, sc, sh, pos3, inv, w_t, *, tm):
    bsz, s, d = x.shape
    kern = functools.partial(_inproj_kernel, w_scale=IDX_HEADS ** -0.5 * IDX_DIM ** -0.5)
    return pl.pallas_call(
        kern,
        out_shape=(
            jax.ShapeDtypeStruct((bsz, s // LANES, Q_ROWS, LANES), jnp.bfloat16),
            jax.ShapeDtypeStruct((bsz, s, K_ROWS), jnp.bfloat16),
            jax.ShapeDtypeStruct((bsz, s // LANES, V_ROWS, LANES), jnp.bfloat16),
            jax.ShapeDtypeStruct((bsz, s // LANES, W_ROWS, LANES), jnp.float32),
        ),
        grid=(bsz, s // tm),
        in_specs=[
            pl.BlockSpec((None, tm, d), lambda b, i: (b, i, 0)),
            pl.BlockSpec((None, 1, d), lambda b, i: (b, 0, 0)),
            pl.BlockSpec((None, 1, d), lambda b, i: (b, 0, 0)),
            pl.BlockSpec((None, 1, tm), lambda b, i: (b, 0, i)),
            pl.BlockSpec((HALF, 1), lambda b, i: (0, 0)),
            pl.BlockSpec((P_ROWS, d), lambda b, i: (0, 0)),
        ],
        out_specs=(
            pl.BlockSpec((None, tm // LANES, Q_ROWS, LANES), lambda b, i: (b, i, 0, 0)),
            pl.BlockSpec((None, tm, K_ROWS), lambda b, i: (b, i, 0)),
            pl.BlockSpec((None, tm // LANES, V_ROWS, LANES), lambda b, i: (b, i, 0, 0)),
            pl.BlockSpec((None, tm // LANES, W_ROWS, LANES), lambda b, i: (b, i, 0, 0)),
        ),
        compiler_params=pltpu.CompilerParams(
            dimension_semantics=("arbitrary", "arbitrary"),
            vmem_limit_bytes=VMEM_LIMIT),
        name="in_proj",
    )(x, sc, sh, pos3, inv, w_t)


def _ordered_to_f32(c):
    bits = jnp.where(c >= 0, c, c ^ jnp.int32(0x7FFFFFFF))
    return lax.bitcast_convert_type(bits, jnp.float32)


def _head_rhs(qT, row0):
    zeros = jnp.zeros((HEAD_DIM, LANES), qT.dtype)
    cols = []
    for h in range(A_HEADS):
        blk = qT[row0 + h * HEAD_DIM:row0 + (h + 1) * HEAD_DIM, :]
        g = h // GROUP
        cols.append(jnp.concatenate([blk, zeros] if g == 0 else [zeros, blk], axis=0))
    return jnp.concatenate(cols, axis=1)


def _attn_kernel(sink_ref, qT_ref, kN_ref, vT_ref, wT_ref, ga_ref, gb_ref,
                 y_ref, score_ref, thr_ref, jt_ref, m_ref, o_ref, yT_ref,
                 sa_ref, p_ref, rhs_ref, sq_ref,
                 *, seq, topk):
    n = pl.program_id(1)
    q0 = n * BLOCK
    nk = (n + 2) // 2
    qidx = q0 + lax.broadcasted_iota(jnp.int32, (KEY_CHUNK, LANES), 1)
    row_iota = lax.broadcasted_iota(jnp.int32, (KEY_CHUNK, LANES), 0)
    nh = B_HEADS

    def identity():
        return (lax.broadcasted_iota(jnp.int32, (LANES, LANES), 0)
                == lax.broadcasted_iota(jnp.int32, (LANES, LANES), 1)).astype(jnp.bfloat16)

    def mask_rhs(row0):
        return jnp.concatenate([_head_rhs(qT_ref[...], row0),
                                jnp.concatenate([identity()] * B_HEADS, axis=1)], axis=0)

    ones_rows = jnp.ones((ONES_ROWS, KEY_CHUNK), jnp.bfloat16)

    def finish_group(r0, width, g_ref):
        blk = yT_ref[r0:r0 + width, :]
        ms = jnp.sum(blk * blk, axis=0, keepdims=True) * (1.0 / width)
        y_ref[r0:r0 + width, :] = (blk * lax.rsqrt(ms + RMS_EPS) * g_ref[...]).astype(y_ref.dtype)

    def softmax_heads(s_all, m_of):
        p_heads, maxes = [], []
        for h in range(nh):
            sl = slice(h * LANES, (h + 1) * LANES)
            s_h = s_all[:, sl]
            m_old = m_of(h, sl)
            m_new = jnp.maximum(m_old, jnp.max(s_h, axis=0, keepdims=True))
            p_heads.append(jnp.exp(s_h - m_new).astype(jnp.bfloat16))
            maxes.append((m_old, m_new))
        return p_heads, maxes

    rhs_a = mask_rhs(0)
    wblk = jnp.maximum(n - 1, 0)
    w0 = pl.multiple_of(wblk * BLOCK, BLOCK)
    delta = qidx - (w0 + row_iota)
    band = (delta >= 0) & (delta < WINDOW)
    bias = jnp.where(band, 0.0, NEG_BIG).astype(jnp.bfloat16)
    lhs = jnp.concatenate([kN_ref[pl.ds(w0, 2 * BLOCK), 0:LANES], bias], axis=1)
    s_all = jnp.dot(lhs, rhs_a, preferred_element_type=jnp.float32)

    for h in range(IDX_HEADS):
        rhs_ref[0:IDX_DIM, h * LANES:(h + 1) * LANES] = (
            qT_ref[A_Q + B_Q + h * IDX_DIM:A_Q + B_Q + (h + 1) * IDX_DIM, :])
    sa_ref[...] = jnp.dot(kN_ref[0:KEY_CHUNK, 2 * LANES:2 * LANES + IDX_DIM], rhs_ref[0:IDX_DIM, :],
                          preferred_element_type=jnp.float32)

    p_heads, ms = softmax_heads(
        s_all, lambda h, sl: jnp.full((1, LANES), sink_ref[h], jnp.float32))
    for g in range(A_KV_HEADS):
        pT = jnp.concatenate(p_heads[g * GROUP:(g + 1) * GROUP], axis=1)
        rows = slice(g * HEAD_DIM, (g + 1) * HEAD_DIM)
        vg = jnp.concatenate([vT_ref[wblk, rows, :], vT_ref[wblk + 1, rows, :]], axis=1)
        vg = jnp.concatenate([vg, ones_rows], axis=0)
        og = jnp.dot(vg, pT, preferred_element_type=jnp.float32)
        for j in range(GROUP):
            h = g * GROUP + j
            sink, m = ms[h]
            csl = slice(j * LANES, (j + 1) * LANES)
            l = og[HEAD_DIM:HEAD_DIM + 1, csl] + jnp.exp(sink - m)
            yT_ref[h * HEAD_DIM:(h + 1) * HEAD_DIM, :] = og[0:HEAD_DIM, csl] * (1.0 / l)

    max_nk = (seq // BLOCK + 1) // 2
    n_piece = B_HEADS * LANES // PIECE
    heads_per_piece = PIECE // LANES

    def score_stream(nkk):
        def run():
            for c in range(nkk):
                rows = slice(c * KEY_CHUNK, (c + 1) * KEY_CHUNK)
                sc = jnp.zeros((KEY_CHUNK, LANES), jnp.float32)
                for j in range(n_piece):
                    cs = slice(j * PIECE, (j + 1) * PIECE)
                    if c == 0:
                        rel = sa_ref[:, cs]
                    else:
                        rel = jnp.dot(kN_ref[rows, 2 * LANES:2 * LANES + IDX_DIM], rhs_ref[0:IDX_DIM, cs],
                                      preferred_element_type=jnp.float32)
                    for hh in range(heads_per_piece):
                        h = j * heads_per_piece + hh
                        sc = sc + jnp.maximum(rel[:, hh * LANES:(hh + 1) * LANES], 0.0) * wT_ref[h:h + 1, :]
                kidx = c * KEY_CHUNK + row_iota
                score_ref[rows, :] = jnp.where(kidx <= qidx, sc, -jnp.inf)
        return run

    def fold_rows(x):
        return x.reshape(KEY_CHUNK // 8, 8, LANES).sum(axis=0)

    def count(pred_fn):
        def body(c, acc):
            r0 = pl.multiple_of(c * KEY_CHUNK, KEY_CHUNK)
            blk = score_ref[pl.ds(r0, KEY_CHUNK), :]
            return acc + fold_rows(pred_fn(blk, r0).astype(jnp.int32))
        acc = lax.fori_loop(0, nk, body, jnp.zeros((8, LANES), jnp.int32))
        return acc.sum(axis=0, keepdims=True)

    def masked_min(pred_fn):
        def body(c, acc):
            r0 = pl.multiple_of(c * KEY_CHUNK, KEY_CHUNK)
            blk = score_ref[pl.ds(r0, KEY_CHUNK), :]
            v = jnp.where(pred_fn(blk, r0), blk, jnp.inf)
            return jnp.minimum(acc, v.reshape(KEY_CHUNK // 8, 8, LANES).min(axis=0))
        acc = lax.fori_loop(0, nk, body, jnp.full((8, LANES), jnp.inf, jnp.float32))
        return acc.min(axis=0, keepdims=True)

    def search_static(nkk):
        def count_ge(cf):
            acc = jnp.zeros((8, LANES), jnp.int32)
            for c in range(nkk):
                blk = score_ref[c * KEY_CHUNK:(c + 1) * KEY_CHUNK, :]
                acc = acc + fold_rows((blk >= cf).astype(jnp.int32))
            return acc.sum(axis=0, keepdims=True)

        def run():
            c0 = count_ge(0.0)
            nonneg = c0 >= topk
            pfx0 = jnp.where(nonneg, jnp.int32(0), jnp.int32(INT_MIN))
            st0 = (pfx0, jnp.where(nonneg, c0, jnp.int32(nkk * KEY_CHUNK)),
                   _ordered_to_f32(pfx0 + jnp.int32(1 << 30)))

            def bit_body(i, st):
                pfx, c_pfx, cf = st
                bit = jnp.int32(1) << (30 - i)
                cand = pfx + bit
                f_take = _ordered_to_f32(cand + (bit >> 1))
                f_keep = _ordered_to_f32(pfx + (bit >> 1))
                cnt = count_ge(cf)
                take = cnt >= topk
                return (jnp.where(take, cand, pfx), jnp.where(take, cnt, c_pfx),
                        jnp.where(take, f_take, f_keep))

            pfx, c_pfx, _ = lax.fori_loop(0, 31, bit_body, st0)
            return _ordered_to_f32(pfx), c_pfx

        return run

    def score_and_search(nkk):
        def run():
            score_stream(nkk)()
            if nkk == 1:
                return (jnp.full((1, LANES), -jnp.inf, jnp.float32), jnp.zeros((1, LANES), jnp.int32))
            return search_static(nkk)()
        return run

    lo, c_lo = lax.switch(nk - 1, [score_and_search(j) for j in range(1, max_nk + 1)])
    thr_ref[...] = lo
    jt_ref[...] = jnp.full((1, LANES), jnp.where(n >= 2, seq, -1), jnp.int32)
    unresolved = jnp.max(jnp.where(c_lo > topk, 1, 0), axis=1, keepdims=True)[0, 0] > 0

    rhs_ref[...] = mask_rhs(A_Q)

    @pl.when(unresolved)
    def _resolve_ties():
        def count_gt(bv):
            return count(lambda blk, r0: blk > bv)

        b0 = masked_min(lambda blk, r0: blk >= lo)

        def w_cond(st):
            return jnp.max(st[1], axis=1, keepdims=True)[0, 0] >= topk

        def w_body(st):
            bv, cgt = st
            nxt = masked_min(lambda blk, r0: blk > bv)
            bn = jnp.where(cgt >= topk, nxt, bv)
            return bn, count_gt(bn)

        bv, cgt = lax.while_loop(w_cond, w_body, (b0, count_gt(b0)))
        thr_ref[...] = bv
        need = (topk - cgt).astype(jnp.float32)

        tri = (lax.broadcasted_iota(jnp.int32, (KEY_CHUNK, KEY_CHUNK), 0)
               >= lax.broadcasted_iota(jnp.int32, (KEY_CHUNK, KEY_CHUNK), 1)).astype(jnp.bfloat16)

        def tie_body(c, st):
            seen, jt_f = st
            r0 = pl.multiple_of(c * KEY_CHUNK, KEY_CHUNK)
            tie = score_ref[pl.ds(r0, KEY_CHUNK), :] == bv
            upto = jnp.dot(tri, jnp.where(tie, 1.0, 0.0).astype(jnp.bfloat16),
                           preferred_element_type=jnp.float32) + seen
            kidx_f = (r0 + row_iota).astype(jnp.float32)
            kept = jnp.where(tie, jnp.where(upto <= need, kidx_f, -1.0), -1.0)
            return upto[KEY_CHUNK - 1:KEY_CHUNK, :], jnp.maximum(jt_f, jnp.max(kept, axis=0, keepdims=True))

        _, jt_f = lax.fori_loop(0, nk, tie_body, (jnp.zeros((1, LANES), jnp.float32),
                                                  jnp.full((1, LANES), -1.0, jnp.float32)))
        jt_ref[...] = jt_f.astype(jnp.int32)

    thr = thr_ref[...]
    jt = jt_ref[...]

    def qk_stream(nkk):
        def run():
            m_fin = [jnp.full((1, LANES), NEG_BIG, jnp.float32) for _ in range(nh)]
            for c in range(nkk):
                rows = slice(c * KEY_CHUNK, (c + 1) * KEY_CHUNK)
                sc = score_ref[rows, :]
                kidx = c * KEY_CHUNK + row_iota
                sel = (sc > thr) | ((sc == thr) & (kidx <= jt))
                bias = jnp.where(sel, 0.0, NEG_BIG).astype(jnp.bfloat16)
                lhs = jnp.concatenate([kN_ref[rows, LANES:2 * LANES], bias], axis=1)
                for j in range(n_piece):
                    cs = slice(j * PIECE, (j + 1) * PIECE)
                    s = jnp.dot(lhs, rhs_ref[:, cs], preferred_element_type=jnp.float32)
                    sq_ref[rows, cs] = s
                    for hh in range(heads_per_piece):
                        h = j * heads_per_piece + hh
                        m_fin[h] = jnp.maximum(
                            m_fin[h], jnp.max(s[:, hh * LANES:(hh + 1) * LANES], axis=0, keepdims=True))
            for h in range(nh):
                m_ref[:, h * LANES:(h + 1) * LANES] = m_fin[h]
        return run

    def softmax_chunk(c, carry):
        rows = pl.ds(pl.multiple_of(c * KEY_CHUNK, KEY_CHUNK), KEY_CHUNK)
        p_ref[rows, :] = jnp.exp(sq_ref[rows, :] - m_ref[...]).astype(p_ref.dtype)
        return carry

    def pv_stream(nkk):
        def run():
            keys = nkk * KEY_CHUNK
            ones = jnp.ones((ONES_ROWS, keys), jnp.bfloat16)
            for g in range(B_KV_HEADS):
                rows = slice(A_KV + g * HEAD_DIM, A_KV + (g + 1) * HEAD_DIM)
                csl = slice(g * GROUP * LANES, (g + 1) * GROUP * LANES)
                vg = jnp.concatenate([vT_ref[t, rows, :] for t in range(keys // LANES)], axis=1)
                vg = jnp.concatenate([vg, ones], axis=0)
                pv = jnp.dot(vg, p_ref[0:keys, csl], preferred_element_type=jnp.float32)
                o_ref[:, csl] = pv[0:O_ROWS]
        return run

    lax.switch(nk - 1, [qk_stream(j) for j in range(1, max_nk + 1)])
    lax.fori_loop(0, nk, softmax_chunk, 0)
    lax.switch(nk - 1, [pv_stream(j) for j in range(1, max_nk + 1)])

    ob = o_ref[0:HEAD_DIM, :] * (1.0 / o_ref[HEAD_DIM:HEAD_DIM + 1, :])
    for h in range(nh):
        yT_ref[A_Q + h * HEAD_DIM:A_Q + (h + 1) * HEAD_DIM, :] = ob[:, h * LANES:(h + 1) * LANES]

    finish_group(0, A_Q, ga_ref)
    finish_group(A_Q, B_Q, gb_ref)


def _attention(sinks, qT, kN, vT, wT, g_a, g_b):
    bsz, s, _ = kN.shape
    nb = s // BLOCK
    topk = min(TOPK_MAX, s // 4)
    kern = functools.partial(_attn_kernel, seq=s, topk=topk)
    return pl.pallas_call(
        kern,
        out_shape=jax.ShapeDtypeStruct((bsz, A_Q + B_Q, s), jnp.bfloat16),
        grid=(bsz, nb),
        in_specs=[
            pl.BlockSpec(memory_space=pltpu.SMEM),
            pl.BlockSpec((None, None, Q_ROWS, BLOCK), lambda b, n: (b, n, 0, 0)),
            pl.BlockSpec((None, s, K_ROWS), lambda b, n: (b, 0, 0)),
            pl.BlockSpec((None, nb, V_ROWS, LANES), lambda b, n: (b, 0, 0, 0)),
            pl.BlockSpec((None, None, W_ROWS, BLOCK), lambda b, n: (b, n, 0, 0)),
            pl.BlockSpec((A_Q, LANES), lambda b, n: (0, 0)),
            pl.BlockSpec((B_Q, LANES), lambda b, n: (0, 0)),
        ],
        out_specs=pl.BlockSpec((None, A_Q + B_Q, BLOCK), lambda b, n: (b, 0, n)),
        scratch_shapes=[
            pltpu.VMEM((s, LANES), jnp.float32),
            pltpu.VMEM((1, LANES), jnp.float32),
            pltpu.VMEM((1, LANES), jnp.int32),
            pltpu.VMEM((1, B_HEADS * LANES), jnp.float32),
            pltpu.VMEM((O_ROWS, B_HEADS * LANES), jnp.float32),
            pltpu.VMEM((A_Q + B_Q, LANES), jnp.float32),
            pltpu.VMEM((KEY_CHUNK, B_HEADS * LANES), jnp.float32),
            pltpu.VMEM((s, B_HEADS * LANES), jnp.bfloat16),
            pltpu.VMEM((2 * LANES, B_HEADS * LANES), jnp.bfloat16),
            pltpu.VMEM((s, B_HEADS * LANES), jnp.float32),
        ],
        compiler_params=pltpu.CompilerParams(
            dimension_semantics=("arbitrary", "arbitrary"),
            vmem_limit_bytes=VMEM_LIMIT),
        name="attention",
    )(sinks, qT, kN, vT, wT, g_a, g_b)


def _post_kernel(y_ref, x_ref, gt1_ref, wo_ref, g1_ref, b1_ref, sc_ref, sh_ref, gt2_ref,
                 wg_ref, wu_ref, wd_ref, g2_ref, b2_ref, o_ref, *, alpha):
    tm = x_ref.shape[0]
    halves = [slice(i * tm // 2, (i + 1) * tm // 2) for i in range(2)]
    proj = [lax.dot_general(y_ref[:, r], wo_ref[...], (((0,), (0,)), ((), ())),
                            preferred_element_type=jnp.float32) for r in halves]
    x1, gate, up = [], [], []
    for r, pr in zip(halves, proj):
        x1.append(_layer_norm(alpha * x_ref[r, :] + (1.0 + gt1_ref[...]) * pr, g1_ref[...], b1_ref[...]))
        h = (x1[-1] * (1.0 + sc_ref[...]) + sh_ref[...]).astype(jnp.bfloat16)
        gate.append(jnp.dot(h, wg_ref[...], preferred_element_type=jnp.float32))
        up.append(jnp.dot(h, wu_ref[...], preferred_element_type=jnp.float32))
    y2 = [jnp.dot((_silu(g) * u).astype(jnp.bfloat16), wd_ref[...], preferred_element_type=jnp.float32)
          for g, u in zip(gate, up)]
    for r, xh, yh in zip(halves, x1, y2):
        o_ref[r, :] = _layer_norm(alpha * xh + (1.0 + gt2_ref[...]) * yh, g2_ref[...], b2_ref[...])


def _post_attention(y, x, gt1, w_o, g1, b1, sc2, sh2, gt2, w_gate, w_up, w_down, g2, b2, *, alpha, tm):
    bsz, s, d = x.shape
    dy = y.shape[1]
    dff = w_gate.shape[-1]
    resident = pl.Buffered(1)
    tile = lambda width: pl.BlockSpec((None, tm, width), lambda b_, i: (b_, i, 0))
    per_batch = pl.BlockSpec((None, 1, d), lambda b_, i: (b_, 0, 0))
    row = pl.BlockSpec((1, d), lambda b_, i: (0, 0))
    weight = lambda r, c: pl.BlockSpec((r, c), lambda b_, i: (0, 0), pipeline_mode=resident)
    return pl.pallas_call(
        functools.partial(_post_kernel, alpha=alpha),
        out_shape=jax.ShapeDtypeStruct((bsz, s, d), jnp.float32),
        grid=(bsz, s // tm),
        in_specs=[pl.BlockSpec((None, dy, tm), lambda b_, i: (b_, 0, i)), tile(d), per_batch, weight(dy, d), row, row,
                  per_batch, per_batch, per_batch,
                  weight(d, dff), weight(d, dff), weight(dff, d), row, row],
        out_specs=tile(d),
        compiler_params=pltpu.CompilerParams(
            dimension_semantics=("arbitrary", "arbitrary"),
            vmem_limit_bytes=VMEM_LIMIT),
        name="out_proj_ffn",
    )(y, x, gt1, w_o, g1, b1, sc2, sh2, gt2, w_gate, w_up, w_down, g2, b2)


def _regroup_w_in(w_in_l):
    d = w_in_l.shape[0]
    splits = np.cumsum([A_Q, A_KV, A_KV, B_Q, B_KV, B_KV, I_Q, IDX_DIM, IDX_HEADS])
    qa, ka, va, qb, kb, vb, qi, ki, wi = jnp.split(w_in_l, splits[:-1], axis=1)
    zk = jnp.zeros((d, K_ROWS - A_KV - B_KV - IDX_DIM), w_in_l.dtype)
    zw = jnp.zeros((d, 2 * W_ROWS - IDX_HEADS), w_in_l.dtype)
    cols = jnp.concatenate([qa, qb, qi, ka, kb, ki, zk, va, vb, wi, zw], axis=1)
    return cols.T.astype(jnp.bfloat16)


def kernel(x, c, positions, w_ada, b_ada, w_in, attn_sinks, g_a, g_b, w_o,
           ln1_g, ln1_b, w_gate, w_up, w_down, ln2_g, ln2_b):
    bsz, s, d = x.shape
    depth = w_ada.shape[0]
    alpha = (2.0 * depth) ** 0.25
    bf = jnp.bfloat16

    mod = _adaln_mod(c, w_ada, b_ada)
    inv = (ROPE_THETA ** (-jnp.arange(HALF, dtype=jnp.float32) / HALF)).reshape(HALF, 1)
    pos3 = positions.reshape(bsz, 1, s)

    for l in range(depth):
        m = mod[l].reshape(bsz, 1, 6 * d)
        sh1, sc1, gt1, sh2, sc2, gt2 = [m[:, :, i * d:(i + 1) * d] for i in range(6)]
        qT, kN, vT, wT = _inproj(x, sc1, sh1, pos3, inv, _regroup_w_in(w_in[l]), tm=1024)
        y = _attention(attn_sinks[l], qT, kN, vT, wT,
                       jnp.broadcast_to(g_a[l][:, None], (A_Q, LANES)),
                       jnp.broadcast_to(g_b[l][:, None], (B_Q, LANES)))
        x = _post_attention(y, x, gt1, w_o[l].astype(bf), ln1_g[l].reshape(1, d), ln1_b[l].reshape(1, d),
                            sc2, sh2, gt2, w_gate[l].astype(bf), w_up[l].astype(bf), w_down[l].astype(bf),
                            ln2_g[l].reshape(1, d), ln2_b[l].reshape(1, d), alpha=alpha, tm=512)
    return x
```

```python
import functools

import jax
import jax.numpy as jnp
import numpy as np
from jax import lax
from jax.experimental import pallas as pl
from jax.experimental.pallas import tpu as pltpu

HEAD_DIM = 64
HALF = HEAD_DIM // 2
A_HEADS = 8
A_KV_HEADS = 2
B_HEADS = 8
B_KV_HEADS = 2
IDX_HEADS = 8
IDX_DIM = 64
WINDOW = 128
BLOCK = 128
TOPK_MAX = 256
ROPE_THETA = 10000.0
LN_EPS = 1e-5
RMS_EPS = 1e-6

A_Q = A_HEADS * HEAD_DIM
A_KV = A_KV_HEADS * HEAD_DIM
B_Q = B_HEADS * HEAD_DIM
B_KV = B_KV_HEADS * HEAD_DIM
I_Q = IDX_HEADS * IDX_DIM
GROUP = A_HEADS // A_KV_HEADS

LANES = 128
KEY_CHUNK = 256
PIECE = 256
ONES_ROWS = 16
O_ROWS = HEAD_DIM + 8
NEG_BIG = -1e30
INT_MIN = -(2 ** 31)
VMEM_LIMIT = 56 * 1024 * 1024

Q_ROWS = A_Q + B_Q + I_Q
K_ROWS = 3 * LANES
V_ROWS = A_KV + B_KV
W_ROWS = 8
P_ROWS = Q_ROWS + K_ROWS + V_ROWS + 2 * W_ROWS


def _silu(x):
    return x * (1.0 / (1.0 + jnp.exp(-x)))


def _layer_norm(z, g, b):
    mu = jnp.mean(z, axis=-1, keepdims=True)
    zc = z - mu
    var = jnp.mean(zc * zc, axis=-1, keepdims=True)
    return zc * lax.rsqrt(var + LN_EPS) * g + b


def _mod_kernel(c_ref, w_ref, b_ref, o_ref):
    c_act = _silu(c_ref[...]).astype(jnp.bfloat16)
    o_ref[...] = jnp.dot(c_act, w_ref[...].astype(jnp.bfloat16),
                         preferred_element_type=jnp.float32) + b_ref[...]


def _adaln_mod(c, w_ada, b_ada):
    depth, d, d6 = w_ada.shape
    bsz = c.shape[0]
    tn = 1024
    return pl.pallas_call(
        _mod_kernel,
        out_shape=jax.ShapeDtypeStruct((depth, bsz, d6), jnp.float32),
        grid=(depth, d6 // tn),
        in_specs=[
            pl.BlockSpec((bsz, d), lambda l, j: (0, 0)),
            pl.BlockSpec((None, d, tn), lambda l, j: (l, 0, j)),
            pl.BlockSpec((None, 1, tn), lambda l, j: (l, 0, j)),
        ],
        out_specs=pl.BlockSpec((None, bsz, tn), lambda l, j: (l, 0, j)),
        compiler_params=pltpu.CompilerParams(
            dimension_semantics=("arbitrary", "arbitrary"),
            vmem_limit_bytes=VMEM_LIMIT),
        name="adaln_mod",
    )(c, w_ada, b_ada.reshape(depth, 1, d6))


def _rope_head(x, r, cos, sin, scale=None):
    x1, x2 = x[r:r + HALF], x[r + HALF:r + HEAD_DIM]
    o1 = x1 * cos - x2 * sin
    o2 = x2 * cos + x1 * sin
    if scale is not None:
        o1, o2 = o1 * scale, o2 * scale
    return o1, o2


def _inproj_kernel(x_ref, sc_ref, sh_ref, pos_ref, inv_ref, w_ref,
                   qT_ref, kN_ref, vT_ref, wT_ref, *, w_scale):
    tm = x_ref.shape[0]
    halves = [slice(i * tm // 2, (i + 1) * tm // 2) for i in range(2)]
    pTs = []
    for r in halves:
        h = x_ref[r, :] * (1.0 + sc_ref[...]) + sh_ref[...]
        pTs.append(lax.dot_general(w_ref[...], h.astype(jnp.bfloat16), (((1,), (1,)), ((), ())),
                                   preferred_element_type=jnp.float32))
    for i, (r, pT) in enumerate(zip(halves, pTs)):
        _inproj_epilogue(pT, pos_ref[:, r], inv_ref, qT_ref, kN_ref, vT_ref, wT_ref, r,
                         i * (tm // 2 // LANES), w_scale)


def _inproj_epilogue(pT, pos, inv_ref, qT_ref, kN_ref, vT_ref, wT_ref, tok, blk0, w_scale):
    ang = inv_ref[...] * pos.astype(jnp.float32)
    cos, sin = jnp.cos(ang), jnp.sin(ang)

    nblk = pT.shape[1] // LANES
    for hd in range(Q_ROWS // HEAD_DIM):
        r = hd * HEAD_DIM
        scale = HEAD_DIM ** -0.5 if r < A_Q + B_Q else None
        o1, o2 = _rope_head(pT, r, cos, sin, scale)
        o1, o2 = o1.astype(qT_ref.dtype), o2.astype(qT_ref.dtype)
        for j in range(nblk):
            qT_ref[blk0 + j, r:r + HALF, :] = o1[:, j * LANES:(j + 1) * LANES]
            qT_ref[blk0 + j, r + HALF:r + HEAD_DIM, :] = o2[:, j * LANES:(j + 1) * LANES]

    k_rows = []
    for hd in range(K_ROWS // HEAD_DIM):
        k_rows.extend(_rope_head(pT, Q_ROWS + hd * HEAD_DIM, cos, sin))
    kN_ref[tok, :] = jnp.concatenate(k_rows, axis=0).T.astype(kN_ref.dtype)

    v0 = Q_ROWS + K_ROWS
    v = pT[v0:v0 + V_ROWS].astype(vT_ref.dtype)
    w0 = v0 + V_ROWS
    w = pT[w0:w0 + W_ROWS] * w_scale
    for j in range(nblk):
        vT_ref[blk0 + j] = v[:, j * LANES:(j + 1) * LANES]
        wT_ref[blk0 + j] = w[:, j * LANES:(j + 1) * LANES]


def _inproj(x, sc, sh, pos3, inv, w_t, *, tm):
    bsz, s, d = x.shape
    kern = functools.partial(_inproj_kernel, w_scale=IDX_HEADS ** -0.5 * IDX_DIM ** -0.5)
    return pl.pallas_call(
        kern,
        out_shape=(
            jax.ShapeDtypeStruct((bsz, s // LANES, Q_ROWS, LANES), jnp.bfloat16),
            jax.ShapeDtypeStruct((bsz, s, K_ROWS), jnp.bfloat16),
            jax.ShapeDtypeStruct((bsz, s // LANES, V_ROWS, LANES), jnp.bfloat16),
            jax.ShapeDtypeStruct((bsz, s // LANES, W_ROWS, LANES), jnp.float32),
        ),
        grid=(bsz, s // tm),
        in_specs=[
            pl.BlockSpec((None, tm, d), lambda b, i: (b, i, 0)),
            pl.BlockSpec((None, 1, d), lambda b, i: (b, 0, 0)),
            pl.BlockSpec((None, 1, d), lambda b, i: (b, 0, 0)),
            pl.BlockSpec((None, 1, tm), lambda b, i: (b, 0, i)),
            pl.BlockSpec((HALF, 1), lambda b, i: (0, 0)),
            pl.BlockSpec((P_ROWS, d), lambda b, i: (0, 0)),
        ],
        out_specs=(
            pl.BlockSpec((None, tm // LANES, Q_ROWS, LANES), lambda b, i: (b, i, 0, 0)),
            pl.BlockSpec((None, tm, K_ROWS), lambda b, i: (b, i, 0)),
            pl.BlockSpec((None, tm // LANES, V_ROWS, LANES), lambda b, i: (b, i, 0, 0)),
            pl.BlockSpec((None, tm // LANES, W_ROWS, LANES), lambda b, i: (b, i, 0, 0)),
        ),
        compiler_params=pltpu.CompilerParams(
            dimension_semantics=("arbitrary", "arbitrary"),
            vmem_limit_bytes=VMEM_LIMIT),
        name="in_proj",
    )(x, sc, sh, pos3, inv, w_t)


def _ordered_to_f32(c):
    bits = jnp.where(c >= 0, c, c ^ jnp.int32(0x7FFFFFFF))
    return lax.bitcast_convert_type(bits, jnp.float32)


def _head_rhs(qT, row0):
    zeros = jnp.zeros((HEAD_DIM, LANES), qT.dtype)
    cols = []
    for h in range(A_HEADS):
        blk = qT[row0 + h * HEAD_DIM:row0 + (h + 1) * HEAD_DIM, :]
        g = h // GROUP
        cols.append(jnp.concatenate([blk, zeros] if g == 0 else [zeros, blk], axis=0))
    return jnp.concatenate(cols, axis=1)


def _attn_kernel(sink_ref, qT_ref, kN_ref, vT_ref, wT_ref, ga_ref, gb_ref,
                 wo_ref, wg_ref, wu_ref, wd_ref,
                 y_ref, wo_bf, wg_bf, wu_bf, wd_bf,
                 score_ref, thr_ref, jt_ref, m_ref, o_ref, yT_ref,
                 sa_ref, p_ref, rhs_ref, sq_ref,
                 *, seq, topk):
    for src, dst in ((wo_ref, wo_bf), (wg_ref, wg_bf), (wu_ref, wu_bf), (wd_ref, wd_bf)):
        dst[...] = src[...].astype(dst.dtype)

    n = pl.program_id(1)
    q0 = n * BLOCK
    nk = (n + 2) // 2
    qidx = q0 + lax.broadcasted_iota(jnp.int32, (KEY_CHUNK, LANES), 1)
    row_iota = lax.broadcasted_iota(jnp.int32, (KEY_CHUNK, LANES), 0)
    nh = B_HEADS

    def identity():
        return (lax.broadcasted_iota(jnp.int32, (LANES, LANES), 0)
                == lax.broadcasted_iota(jnp.int32, (LANES, LANES), 1)).astype(jnp.bfloat16)

    def mask_rhs(row0):
        return jnp.concatenate([_head_rhs(qT_ref[...], row0),
                                jnp.concatenate([identity()] * B_HEADS, axis=1)], axis=0)

    ones_rows = jnp.ones((ONES_ROWS, KEY_CHUNK), jnp.bfloat16)

    def finish_group(r0, width, g_ref):
        blk = yT_ref[r0:r0 + width, :]
        ms = jnp.sum(blk * blk, axis=0, keepdims=True) * (1.0 / width)
        y_ref[r0:r0 + width, :] = (blk * lax.rsqrt(ms + RMS_EPS) * g_ref[...]).astype(y_ref.dtype)

    def softmax_heads(s_all, m_of):
        p_heads, maxes = [], []
        for h in range(nh):
            sl = slice(h * LANES, (h + 1) * LANES)
            s_h = s_all[:, sl]
            m_old = m_of(h, sl)
            m_new = jnp.maximum(m_old, jnp.max(s_h, axis=0, keepdims=True))
            p_heads.append(jnp.exp(s_h - m_new).astype(jnp.bfloat16))
            maxes.append((m_old, m_new))
        return p_heads, maxes

    rhs_a = mask_rhs(0)
    wblk = jnp.maximum(n - 1, 0)
    w0 = pl.multiple_of(wblk * BLOCK, BLOCK)
    delta = qidx - (w0 + row_iota)
    band = (delta >= 0) & (delta < WINDOW)
    bias = jnp.where(band, 0.0, NEG_BIG).astype(jnp.bfloat16)
    lhs = jnp.concatenate([kN_ref[pl.ds(w0, 2 * BLOCK), 0:LANES], bias], axis=1)
    s_all = jnp.dot(lhs, rhs_a, preferred_element_type=jnp.float32)

    for h in range(IDX_HEADS):
        rhs_ref[0:IDX_DIM, h * LANES:(h + 1) * LANES] = (
            qT_ref[A_Q + B_Q + h * IDX_DIM:A_Q + B_Q + (h + 1) * IDX_DIM, :])
    sa_ref[...] = jnp.dot(kN_ref[0:KEY_CHUNK, 2 * LANES:2 * LANES + IDX_DIM], rhs_ref[0:IDX_DIM, :],
                          preferred_element_type=jnp.float32)

    p_heads, ms = softmax_heads(
        s_all, lambda h, sl: jnp.full((1, LANES), sink_ref[h], jnp.float32))
    for g in range(A_KV_HEADS):
        pT = jnp.concatenate(p_heads[g * GROUP:(g + 1) * GROUP], axis=1)
        rows = slice(g * HEAD_DIM, (g + 1) * HEAD_DIM)
        vg = jnp.concatenate([vT_ref[wblk, rows, :], vT_ref[wblk + 1, rows, :]], axis=1)
        vg = jnp.concatenate([vg, ones_rows], axis=0)
        og = jnp.dot(vg, pT, preferred_element_type=jnp.float32)
        for j in range(GROUP):
            h = g * GROUP + j
            sink, m = ms[h]
            csl = slice(j * LANES, (j + 1) * LANES)
            l = og[HEAD_DIM:HEAD_DIM + 1, csl] + jnp.exp(sink - m)
            yT_ref[h * HEAD_DIM:(h + 1) * HEAD_DIM, :] = og[0:HEAD_DIM, csl] * (1.0 / l)

    max_nk = (seq // BLOCK + 1) // 2
    n_piece = B_HEADS * LANES // PIECE
    heads_per_piece = PIECE // LANES

    def score_stream(nkk):
        def run():
            for c in range(nkk):
                rows = slice(c * KEY_CHUNK, (c + 1) * KEY_CHUNK)
                sc = jnp.zeros((KEY_CHUNK, LANES), jnp.float32)
                for j in range(n_piece):
                    cs = slice(j * PIECE, (j + 1) * PIECE)
                    if c == 0:
                        rel = sa_ref[:, cs]
                    else:
                        rel = jnp.dot(kN_ref[rows, 2 * LANES:2 * LANES + IDX_DIM], rhs_ref[0:IDX_DIM, cs],
                                      preferred_element_type=jnp.float32)
                    for hh in range(heads_per_piece):
                        h = j * heads_per_piece + hh
                        sc = sc + jnp.maximum(rel[:, hh * LANES:(hh + 1) * LANES], 0.0) * wT_ref[h:h + 1, :]
                kidx = c * KEY_CHUNK + row_iota
                score_ref[rows, :] = jnp.where(kidx <= qidx, sc, -jnp.inf)
        return run

    def fold_rows(x):
        return x.reshape(KEY_CHUNK // 8, 8, LANES).sum(axis=0)

    def count(pred_fn):
        def body(c, acc):
            r0 = pl.multiple_of(c * KEY_CHUNK, KEY_CHUNK)
            blk = score_ref[pl.ds(r0, KEY_CHUNK), :]
            return acc + fold_rows(pred_fn(blk, r0).astype(jnp.int32))
        acc = lax.fori_loop(0, nk, body, jnp.zeros((8, LANES), jnp.int32))
        return acc.sum(axis=0, keepdims=True)

    def masked_min(pred_fn):
        def body(c, acc):
            r0 = pl.multiple_of(c * KEY_CHUNK, KEY_CHUNK)
            blk = score_ref[pl.ds(r0, KEY_CHUNK), :]
            v = jnp.where(pred_fn(blk, r0), blk, jnp.inf)
            return jnp.minimum(acc, v.reshape(KEY_CHUNK // 8, 8, LANES).min(axis=0))
        acc = lax.fori_loop(0, nk, body, jnp.full((8, LANES), jnp.inf, jnp.float32))
        return acc.min(axis=0, keepdims=True)

    def search_static(nkk):
        def count_ge(cf):
            acc = jnp.zeros((8, LANES), jnp.int32)
            for c in range(nkk):
                blk = score_ref[c * KEY_CHUNK:(c + 1) * KEY_CHUNK, :]
                acc = acc + fold_rows((blk >= cf).astype(jnp.int32))
            return acc.sum(axis=0, keepdims=True)

        def run():
            c0 = count_ge(0.0)
            nonneg = c0 >= topk
            pfx0 = jnp.where(nonneg, jnp.int32(0), jnp.int32(INT_MIN))
            st0 = (pfx0, jnp.where(nonneg, c0, jnp.int32(nkk * KEY_CHUNK)),
                   _ordered_to_f32(pfx0 + jnp.int32(1 << 30)))

            def bit_body(i, st):
                pfx, c_pfx, cf = st
                bit = jnp.int32(1) << (30 - i)
                cand = pfx + bit
                f_take = _ordered_to_f32(cand + (bit >> 1))
                f_keep = _ordered_to_f32(pfx + (bit >> 1))
                cnt = count_ge(cf)
                take = cnt >= topk
                return (jnp.where(take, cand, pfx), jnp.where(take, cnt, c_pfx),
                        jnp.where(take, f_take, f_keep))

            pfx, c_pfx, _ = lax.fori_loop(0, 31, bit_body, st0)
            return _ordered_to_f32(pfx), c_pfx

        return run

    def score_and_search(nkk):
        def run():
            score_stream(nkk)()
            if nkk == 1:
                return (jnp.full((1, LANES), -jnp.inf, jnp.float32), jnp.zeros((1, LANES), jnp.int32))
            return search_static(nkk)()
        return run

    lo, c_lo = lax.switch(nk - 1, [score_and_search(j) for j in range(1, max_nk + 1)])
    thr_ref[...] = lo
    jt_ref[...] = jnp.full((1, LANES), jnp.where(n >= 2, seq, -1), jnp.int32)
    unresolved = jnp.max(jnp.where(c_lo > topk, 1, 0), axis=1, keepdims=True)[0, 0] > 0

    rhs_ref[...] = mask_rhs(A_Q)

    @pl.when(unresolved)
    def _resolve_ties():
        def count_gt(bv):
            return count(lambda blk, r0: blk > bv)

        b0 = masked_min(lambda blk, r0: blk >= lo)

        def w_cond(st):
            return jnp.max(st[1], axis=1, keepdims=True)[0, 0] >= topk

        def w_body(st):
            bv, cgt = st
            nxt = masked_min(lambda blk, r0: blk > bv)
            bn = jnp.where(cgt >= topk, nxt, bv)
            return bn, count_gt(bn)

        bv, cgt = lax.while_loop(w_cond, w_body, (b0, count_gt(b0)))
        thr_ref[...] = bv
        need = (topk - cgt).astype(jnp.float32)

        tri = (lax.broadcasted_iota(jnp.int32, (KEY_CHUNK, KEY_CHUNK), 0)
               >= lax.broadcasted_iota(jnp.int32, (KEY_CHUNK, KEY_CHUNK), 1)).astype(jnp.bfloat16)

        def tie_body(c, st):
            seen, jt_f = st
            r0 = pl.multiple_of(c * KEY_CHUNK, KEY_CHUNK)
            tie = score_ref[pl.ds(r0, KEY_CHUNK), :] == bv
            upto = jnp.dot(tri, jnp.where(tie, 1.0, 0.0).astype(jnp.bfloat16),
                           preferred_element_type=jnp.float32) + seen
            kidx_f = (r0 + row_iota).astype(jnp.float32)
            kept = jnp.where(tie, jnp.where(upto <= need, kidx_f, -1.0), -1.0)
            return upto[KEY_CHUNK - 1:KEY_CHUNK, :], jnp.maximum(jt_f, jnp.max(kept, axis=0, keepdims=True))

        _, jt_f = lax.fori_loop(0, nk, tie_body, (jnp.zeros((1, LANES), jnp.float32),
                                                  jnp.full((1, LANES), -1.0, jnp.float32)))
        jt_ref[...] = jt_f.astype(jnp.int32)

    thr = thr_ref[...]
    jt = jt_ref[...]

    def qk_stream(nkk):
        def run():
            m_fin = [jnp.full((1, LANES), NEG_BIG, jnp.float32) for _ in range(nh)]
            for c in range(nkk):
                rows = slice(c * KEY_CHUNK, (c + 1) * KEY_CHUNK)
                sc = score_ref[rows, :]
                kidx = c * KEY_CHUNK + row_iota
                sel = (sc > thr) | ((sc == thr) & (kidx <= jt))
                bias = jnp.where(sel, 0.0, NEG_BIG).astype(jnp.bfloat16)
                lhs = jnp.concatenate([kN_ref[rows, LANES:2 * LANES], bias], axis=1)
                for j in range(n_piece):
                    cs = slice(j * PIECE, (j + 1) * PIECE)
                    s = jnp.dot(lhs, rhs_ref[:, cs], preferred_element_type=jnp.float32)
                    sq_ref[rows, cs] = s
                    for hh in range(heads_per_piece):
                        h = j * heads_per_piece + hh
                        m_fin[h] = jnp.maximum(
                            m_fin[h], jnp.max(s[:, hh * LANES:(hh + 1) * LANES], axis=0, keepdims=True))
            for h in range(nh):
                m_ref[:, h * LANES:(h + 1) * LANES] = m_fin[h]
        return run

    def softmax_chunk(c, carry):
        rows = pl.ds(pl.multiple_of(c * KEY_CHUNK, KEY_CHUNK), KEY_CHUNK)
        p_ref[rows, :] = jnp.exp(sq_ref[rows, :] - m_ref[...]).astype(p_ref.dtype)
        return carry

    def pv_stream(nkk):
        def run():
            keys = nkk * KEY_CHUNK
            ones = jnp.ones((ONES_ROWS, keys), jnp.bfloat16)
            for g in range(B_KV_HEADS):
                rows = slice(A_KV + g * HEAD_DIM, A_KV + (g + 1) * HEAD_DIM)
                csl = slice(g * GROUP * LANES, (g + 1) * GROUP * LANES)
                vg = jnp.concatenate([vT_ref[t, rows, :] for t in range(keys // LANES)], axis=1)
                vg = jnp.concatenate([vg, ones], axis=0)
                pv = jnp.dot(vg, p_ref[0:keys, csl], preferred_element_type=jnp.float32)
                o_ref[:, csl] = pv[0:O_ROWS]
        return run

    lax.switch(nk - 1, [qk_stream(j) for j in range(1, max_nk + 1)])
    lax.fori_loop(0, nk, softmax_chunk, 0)
    lax.switch(nk - 1, [pv_stream(j) for j in range(1, max_nk + 1)])

    ob = o_ref[0:HEAD_DIM, :] * (1.0 / o_ref[HEAD_DIM:HEAD_DIM + 1, :])
    for h in range(nh):
        yT_ref[A_Q + h * HEAD_DIM:A_Q + (h + 1) * HEAD_DIM, :] = ob[:, h * LANES:(h + 1) * LANES]

    finish_group(0, A_Q, ga_ref)
    finish_group(A_Q, B_Q, gb_ref)


def _attention(sinks, qT, kN, vT, wT, g_a, g_b, weights, layer):
    bsz, s, _ = kN.shape
    nb = s // BLOCK
    topk = min(TOPK_MAX, s // 4)
    kern = functools.partial(_attn_kernel, seq=s, topk=topk)

    def slab(w):
        rows, cols = w.shape[1:]
        r = min(rows, -(-pl.cdiv(rows, bsz * nb) // 16) * 16)
        last = pl.cdiv(rows, r) - 1
        return (pl.BlockSpec((None, r, cols), lambda b, n: (layer, jnp.minimum(b * nb + n, last), 0)),
                pl.BlockSpec((r, cols), lambda b, n: (jnp.minimum(b * nb + n, last), 0)),
                jax.ShapeDtypeStruct((rows, cols), jnp.bfloat16))

    w_in_specs, w_out_specs, w_out_shapes = zip(*[slab(w) for w in weights])
    return pl.pallas_call(
        kern,
        out_shape=(jax.ShapeDtypeStruct((bsz, A_Q + B_Q, s), jnp.bfloat16),) + w_out_shapes,
        grid=(bsz, nb),
        in_specs=[
            pl.BlockSpec(memory_space=pltpu.SMEM),
            pl.BlockSpec((None, None, Q_ROWS, BLOCK), lambda b, n: (b, n, 0, 0)),
            pl.BlockSpec((None, s, K_ROWS), lambda b, n: (b, 0, 0)),
            pl.BlockSpec((None, nb, V_ROWS, LANES), lambda b, n: (b, 0, 0, 0)),
            pl.BlockSpec((None, None, W_ROWS, BLOCK), lambda b, n: (b, n, 0, 0)),
            pl.BlockSpec((A_Q, LANES), lambda b, n: (0, 0)),
            pl.BlockSpec((B_Q, LANES), lambda b, n: (0, 0)),
            *w_in_specs,
        ],
        out_specs=(pl.BlockSpec((None, A_Q + B_Q, BLOCK), lambda b, n: (b, 0, n)),) + w_out_specs,
        scratch_shapes=[
            pltpu.VMEM((s, LANES), jnp.float32),
            pltpu.VMEM((1, LANES), jnp.float32),
            pltpu.VMEM((1, LANES), jnp.int32),
            pltpu.VMEM((1, B_HEADS * LANES), jnp.float32),
            pltpu.VMEM((O_ROWS, B_HEADS * LANES), jnp.float32),
            pltpu.VMEM((A_Q + B_Q, LANES), jnp.float32),
            pltpu.VMEM((KEY_CHUNK, B_HEADS * LANES), jnp.float32),
            pltpu.VMEM((s, B_HEADS * LANES), jnp.bfloat16),
            pltpu.VMEM((2 * LANES, B_HEADS * LANES), jnp.bfloat16),
            pltpu.VMEM((s, B_HEADS * LANES), jnp.float32),
        ],
        compiler_params=pltpu.CompilerParams(
            dimension_semantics=("arbitrary", "arbitrary"),
            vmem_limit_bytes=VMEM_LIMIT),
        name="attention",
    )(sinks, qT, kN, vT, wT, g_a, g_b, *weights)


def _post_kernel(y_ref, x_ref, gt1_ref, wo_ref, g1_ref, b1_ref, sc_ref, sh_ref, gt2_ref,
                 wg_ref, wu_ref, wd_ref, g2_ref, b2_ref, o_ref, *, alpha):
    tm = x_ref.shape[0]
    halves = [slice(i * tm // 2, (i + 1) * tm // 2) for i in range(2)]
    proj = [lax.dot_general(y_ref[:, r], wo_ref[...], (((0,), (0,)), ((), ())),
                            preferred_element_type=jnp.float32) for r in halves]
    x1, gate, up = [], [], []
    for r, pr in zip(halves, proj):
        x1.append(_layer_norm(alpha * x_ref[r, :] + (1.0 + gt1_ref[...]) * pr, g1_ref[...], b1_ref[...]))
        h = (x1[-1] * (1.0 + sc_ref[...]) + sh_ref[...]).astype(jnp.bfloat16)
        gate.append(jnp.dot(h, wg_ref[...], preferred_element_type=jnp.float32))
        up.append(jnp.dot(h, wu_ref[...], preferred_element_type=jnp.float32))
    y2 = [jnp.dot((_silu(g) * u).astype(jnp.bfloat16), wd_ref[...], preferred_element_type=jnp.float32)
          for g, u in zip(gate, up)]
    for r, xh, yh in zip(halves, x1, y2):
        o_ref[r, :] = _layer_norm(alpha * xh + (1.0 + gt2_ref[...]) * yh, g2_ref[...], b2_ref[...])


def _post_attention(y, x, gt1, w_o, g1, b1, sc2, sh2, gt2, w_gate, w_up, w_down, g2, b2, *, alpha, tm):
    bsz, s, d = x.shape
    dy = y.shape[1]
    dff = w_gate.shape[-1]
    resident = pl.Buffered(1)
    tile = lambda width: pl.BlockSpec((None, tm, width), lambda b_, i: (b_, i, 0))
    per_batch = pl.BlockSpec((None, 1, d), lambda b_, i: (b_, 0, 0))
    row = pl.BlockSpec((1, d), lambda b_, i: (0, 0))
    weight = lambda r, c: pl.BlockSpec((r, c), lambda b_, i: (0, 0), pipeline_mode=resident)
    return pl.pallas_call(
        functools.partial(_post_kernel, alpha=alpha),
        out_shape=jax.ShapeDtypeStruct((bsz, s, d), jnp.float32),
        grid=(bsz, s // tm),
        in_specs=[pl.BlockSpec((None, dy, tm), lambda b_, i: (b_, 0, i)), tile(d), per_batch, weight(dy, d), row, row,
                  per_batch, per_batch, per_batch,
                  weight(d, dff), weight(d, dff), weight(dff, d), row, row],
        out_specs=tile(d),
        compiler_params=pltpu.CompilerParams(
            dimension_semantics=("arbitrary", "arbitrary"),
            vmem_limit_bytes=VMEM_LIMIT),
        name="out_proj_ffn",
    )(y, x, gt1, w_o, g1, b1, sc2, sh2, gt2, w_gate, w_up, w_down, g2, b2)


def _regroup_w_in(w_in_l):
    d = w_in_l.shape[0]
    splits = np.cumsum([A_Q, A_KV, A_KV, B_Q, B_KV, B_KV, I_Q, IDX_DIM, IDX_HEADS])
    qa, ka, va, qb, kb, vb, qi, ki, wi = jnp.split(w_in_l, splits[:-1], axis=1)
    zk = jnp.zeros((d, K_ROWS - A_KV - B_KV - IDX_DIM), w_in_l.dtype)
    zw = jnp.zeros((d, 2 * W_ROWS - IDX_HEADS), w_in_l.dtype)
    cols = jnp.concatenate([qa, qb, qi, ka, kb, ki, zk, va, vb, wi, zw], axis=1)
    return cols.T.astype(jnp.bfloat16)


def kernel(x, c, positions, w_ada, b_ada, w_in, attn_sinks, g_a, g_b, w_o,
           ln1_g, ln1_b, w_gate, w_up, w_down, ln2_g, ln2_b):
    bsz, s, d = x.shape
    depth = w_ada.shape[0]
    alpha = (2.0 * depth) ** 0.25

    mod = _adaln_mod(c, w_ada, b_ada)
    inv = (ROPE_THETA ** (-jnp.arange(HALF, dtype=jnp.float32) / HALF)).reshape(HALF, 1)
    pos3 = positions.reshape(bsz, 1, s)

    for l in range(depth):
        m = mod[l].reshape(bsz, 1, 6 * d)
        sh1, sc1, gt1, sh2, sc2, gt2 = [m[:, :, i * d:(i + 1) * d] for i in range(6)]
        qT, kN, vT, wT = _inproj(x, sc1, sh1, pos3, inv, _regroup_w_in(w_in[l]), tm=1024)
        y, wo_l, wg_l, wu_l, wd_l = _attention(attn_sinks[l], qT, kN, vT, wT,
                                               jnp.broadcast_to(g_a[l][:, None], (A_Q, LANES)),
                                               jnp.broadcast_to(g_b[l][:, None], (B_Q, LANES)),
                                               (w_o, w_gate, w_up, w_down), l)
        x = _post_attention(y, x, gt1, wo_l, ln1_g[l].reshape(1, d), ln1_b[l].reshape(1, d),
                            sc2, sh2, gt2, wg_l, wu_l, wd_l,
                            ln2_g[l].reshape(1, d), ln2_b[l].reshape(1, d), alpha=alpha, tm=512)
    return x
```
